```python
import math
import jax, jax.numpy as jnp
from jax import lax
import numpy as np

D_MODEL = 1024
BATCH = 16
SEQ = 4096
DEPTH = 4

MLA_HEADS = 8
MLA_Q_LORA = 256
MLA_KV_LORA = 128
MLA_NOPE = 64
MLA_ROPE = 32
MLA_V = 64
MLA_QK = MLA_NOPE + MLA_ROPE
ROPE_THETA = 10000.0
Q_BLOCK = 128
DIL_HEADS = 8
DIL_HEAD_DIM = 64
DIL_PATTERNS = ((128, 1), (512, 4), (2048, 16))
REL_BUCKETS = 32
REL_MAX_DIST = 1024
FFN_HIDDEN = -(-8 * D_MODEL // (3 * 256)) * 256
EPS = 1e-6
NEG_INF = -1e30
IN_SIZES = (MLA_Q_LORA, MLA_KV_LORA, MLA_ROPE, 3 * DIL_HEADS * DIL_HEAD_DIM, D_MODEL, D_MODEL)
IN_COLS = sum(IN_SIZES)
IN_SPLIT = tuple(int(s) for s in np.cumsum(IN_SIZES)[:-1])

kernel_name = 'hybrid_mla_dilated_encoder'


def rmsnorm(x, g):
    x32 = x.astype(jnp.float32)
    r = lax.rsqrt(jnp.mean(x32 * x32, axis=-1, keepdims=True) + EPS)
    return (x32 * r).astype(x.dtype) * g


def modulate(h, shift, scale):
    return h * (1.0 + scale[:, None, :]) + shift[:, None, :]


def rope_tables(positions, dtype):
    half = MLA_ROPE // 2
    inv = ROPE_THETA ** (-jnp.arange(half, dtype=jnp.float32) / half)
    ang = positions.astype(jnp.float32)[..., None] * inv
    return jnp.cos(ang)[:, :, None, :].astype(dtype), jnp.sin(ang)[:, :, None, :].astype(dtype)


def apply_rope(x, cos, sin):
    half = x.shape[-1] // 2
    x1, x2 = x[..., :half], x[..., half:]
    return jnp.concatenate([x1 * cos - x2 * sin, x2 * cos + x1 * sin], axis=-1)


def t5_bucket(rel):
    nb = REL_BUCKETS // 2
    max_exact = nb // 2
    ret = jnp.where(rel > 0, nb, 0)
    n = jnp.abs(rel)
    nf = jnp.maximum(n, 1).astype(jnp.float32)
    large = max_exact + (jnp.log(nf / max_exact) / math.log(REL_MAX_DIST / max_exact) * (nb - max_exact)).astype(jnp.int32)
    large = jnp.minimum(large, nb - 1)
    return ret + jnp.where(n < max_exact, n, large)


def mla_attention(q_c, kv_c, k_rope, cos, sin, q_a_norm, w_q_b, kv_a_norm, w_kv_b, q_norm, k_norm):
    B, S, _ = q_c.shape
    q = (rmsnorm(q_c, q_a_norm) @ w_q_b).reshape(B, S, MLA_HEADS, MLA_QK)
    kv = (rmsnorm(kv_c, kv_a_norm) @ w_kv_b).reshape(B, S, MLA_HEADS, MLA_NOPE + MLA_V)
    k_nope, v = kv[..., :MLA_NOPE], kv[..., MLA_NOPE:]
    k = jnp.concatenate([k_nope, jnp.broadcast_to(k_rope[:, :, None, :], (B, S, MLA_HEADS, MLA_ROPE))], axis=-1)
    q = rmsnorm(q, q_norm)
    k = rmsnorm(k, k_norm)
    q = jnp.concatenate([q[..., :MLA_NOPE], apply_rope(q[..., MLA_NOPE:], cos, sin)], axis=-1)
    k = jnp.concatenate([k[..., :MLA_NOPE], apply_rope(k[..., MLA_NOPE:], cos, sin)], axis=-1)
    scale = MLA_QK ** -0.5
    n_blk = S // Q_BLOCK
    qb = q.reshape(B, n_blk, Q_BLOCK, MLA_HEADS, MLA_QK).transpose(1, 0, 2, 3, 4)

    def attend(qi):
        s = jnp.einsum('bqhd,bkhd->bhqk', qi, k).astype(jnp.float32) * scale
        p = jax.nn.softmax(s, axis=-1).astype(v.dtype)
        return jnp.einsum('bhqk,bkhd->bqhd', p, v)

    o = lax.map(attend, qb)
    return o.transpose(1, 0, 2, 3, 4).reshape(B, S, MLA_HEADS * MLA_V)


def strided_band_attention(q, k, v, dil, half, rel_bias):
    B, S, H, dh = q.shape
    L = S // dil
    W = half
    n = -(-L // W)
    Lp = n * W

    def to_sub(t):
        return t.reshape(B, L, dil, H, dh).transpose(0, 2, 3, 1, 4)

    qb = jnp.pad(to_sub(q), ((0, 0), (0, 0), (0, 0), (0, Lp - L), (0, 0))).reshape(B, dil, H, n, W, dh)
    kv_pad = ((0, 0), (0, 0), (0, 0), (W, Lp - L + W), (0, 0))

    def bands(t):
        tb = jnp.pad(to_sub(t), kv_pad).reshape(B, dil, H, n + 2, W, dh)
        return jnp.concatenate([tb[:, :, :, :-2], tb[:, :, :, 1:-1], tb[:, :, :, 2:]], axis=-2)

    kb, vb = bands(k), bands(v)
    s = jnp.einsum('brhnqd,brhnkd->brhnqk', qb, kb).astype(jnp.float32) * (dh ** -0.5)
    qi = jnp.arange(W)[:, None]
    ki = jnp.arange(3 * W)[None, :]
    rel = ki - W - qi
    key_idx = jnp.arange(n)[:, None, None] * W + ki[None] - W
    valid = (jnp.abs(rel) <= W)[None] & (key_idx >= 0) & (key_idx < L)
    bias = rel_bias[t5_bucket(rel * dil)].astype(jnp.float32).transpose(2, 0, 1)
    s = jnp.where(valid, s + bias[:, None], NEG_INF)
    m = jnp.max(s, axis=-1, keepdims=True)
    p = jnp.exp(s - m)
    den = jnp.sum(p, axis=-1)
    o = jnp.einsum('brhnqk,brhnkd->brhnqd', p, vb.astype(jnp.float32)) / den[..., None]
    lse = m[..., 0] + jnp.log(den)
    o = o.reshape(B, dil, H, Lp, dh)[:, :, :, :L].transpose(0, 3, 1, 2, 4).reshape(B, S, H, dh)
    lse = lse.reshape(B, dil, H, Lp)[..., :L].transpose(0, 3, 1, 2).reshape(B, S, H)
    return o, lse


def dilated_attention(qkv, rel_bias, q_norm, k_norm):
    B, S, _ = qkv.shape
    qkv = qkv.reshape(B, S, 3, DIL_HEADS, DIL_HEAD_DIM)
    q = rmsnorm(qkv[:, :, 0], q_norm)
    k = rmsnorm(qkv[:, :, 1], k_norm)
    v = qkv[:, :, 2]
    outs, lses = [], []
    for window, dil in DIL_PATTERNS:
        o, lse = strided_band_attention(q, k, v, dil, window // (2 * dil), rel_bias)
        outs.append(o)
        lses.append(lse)
    w = jax.nn.softmax(jnp.stack(lses, axis=0), axis=0)
    y = jnp.einsum('pbsh,pbshd->bshd', w, jnp.stack(outs, axis=0))
    return y.reshape(B, S, DIL_HEADS * DIL_HEAD_DIM).astype(qkv.dtype)


def swiglu(h, w_gate, w_up, w_down):
    return (jax.nn.silu(h @ w_gate) * (h @ w_up)) @ w_down


def setup_inputs(seed: int = 0) -> dict:
    key = jax.random.key(seed)
    ks = jax.random.split(key, 24)
    f32 = jnp.float32

    def w(k, shape, fan_in):
        return jax.random.normal(k, shape, f32) * (fan_in ** -0.5)

    def gain(k, shape):
        return 1.0 + 0.02 * jax.random.normal(k, shape, f32)

    return {
        'x': jax.random.normal(ks[0], (BATCH, SEQ, D_MODEL), f32),
        'c': jax.random.normal(ks[1], (BATCH, D_MODEL), f32),
        'positions': jax.random.randint(ks[2], (BATCH, 1), 0, 1024, dtype=jnp.int32) + jnp.arange(SEQ, dtype=jnp.int32)[None, :],
        'rel_bias': 0.5 * jax.random.normal(ks[3], (REL_BUCKETS, DIL_HEADS), f32),
        'norm1_g': gain(ks[4], (DEPTH, D_MODEL)),
        'norm2_g': gain(ks[5], (DEPTH, D_MODEL)),
        'ada_w': 0.5 * w(ks[6], (DEPTH, D_MODEL, 6 * D_MODEL), D_MODEL),
        'ada_b': 0.02 * jax.random.normal(ks[7], (DEPTH, 6 * D_MODEL), f32),
        'w_in': w(ks[8], (DEPTH, D_MODEL, IN_COLS), D_MODEL),
        'q_a_norm': gain(ks[9], (DEPTH, MLA_Q_LORA)),
        'w_q_b': w(ks[10], (DEPTH, MLA_Q_LORA, MLA_HEADS * MLA_QK), MLA_Q_LORA),
        'kv_a_norm': gain(ks[11], (DEPTH, MLA_KV_LORA)),
        'w_kv_b': w(ks[12], (DEPTH, MLA_KV_LORA, MLA_HEADS * (MLA_NOPE + MLA_V)), MLA_KV_LORA),
        'q_norm_a': gain(ks[13], (DEPTH, MLA_QK)),
        'k_norm_a': gain(ks[14], (DEPTH, MLA_QK)),
        'q_norm_b': gain(ks[15], (DEPTH, DIL_HEAD_DIM)),
        'k_norm_b': gain(ks[16], (DEPTH, DIL_HEAD_DIM)),
        'w_branch_a': w(ks[17], (DEPTH, MLA_HEADS * MLA_V, D_MODEL), MLA_HEADS * MLA_V),
        'w_branch_b': w(ks[18], (DEPTH, DIL_HEADS * DIL_HEAD_DIM, D_MODEL), DIL_HEADS * DIL_HEAD_DIM),
        'w_out': w(ks[19], (DEPTH, D_MODEL, D_MODEL), D_MODEL),
        'w_ffn_gate': w(ks[20], (DEPTH, D_MODEL, FFN_HIDDEN), D_MODEL),
        'w_ffn_up': w(ks[21], (DEPTH, D_MODEL, FFN_HIDDEN), D_MODEL),
        'w_ffn_down': w(ks[22], (DEPTH, FFN_HIDDEN, D_MODEL), FFN_HIDDEN),
    }


def reference(x, c, positions, rel_bias, norm1_g, norm2_g, ada_w, ada_b, w_in, q_a_norm, w_q_b,
              kv_a_norm, w_kv_b, q_norm_a, k_norm_a, q_norm_b, k_norm_b, w_branch_a, w_branch_b,
              w_out, w_ffn_gate, w_ffn_up, w_ffn_down):
    cos, sin = rope_tables(positions, x.dtype)
    c_act = jax.nn.silu(c)
    for l in range(DEPTH):
        mod = c_act @ ada_w[l] + ada_b[l]
        sh1, sc1, g1, sh2, sc2, g2 = jnp.split(mod, 6, axis=-1)
        h = modulate(rmsnorm(x, norm1_g[l]), sh1, sc1)
        proj = h @ w_in[l]
        q_c, kv_c, k_r, qkv_b, gate_a, gate_b = jnp.split(proj, IN_SPLIT, axis=-1)
        y_a = mla_attention(q_c, kv_c, k_r, cos, sin, q_a_norm[l], w_q_b[l], kv_a_norm[l], w_kv_b[l],
                            q_norm_a[l], k_norm_a[l])
        y_b = dilated_attention(qkv_b, rel_bias, q_norm_b[l], k_norm_b[l])
        merged = jax.nn.sigmoid(gate_a) * (y_a @ w_branch_a[l]) + jax.nn.sigmoid(gate_b) * (y_b @ w_branch_b[l])
        x = x + g1[:, None, :] * (merged @ w_out[l])
        h = modulate(rmsnorm(x, norm2_g[l]), sh2, sc2)
        x = x + g2[:, None, :] * swiglu(h, w_ffn_gate[l], w_ffn_up[l], w_ffn_down[l])
    return x
```

```python
import functools
import math

import jax
import jax.numpy as jnp
import numpy as np
from jax import lax
from jax.experimental import pallas as pl
from jax.experimental.pallas import tpu as pltpu

F32 = jnp.float32
BF16 = jnp.bfloat16

LANES = 128
VMEM_LIMIT_BYTES = 56 * 1024 * 1024

D_MODEL = 1024
MLA_HEADS = 8
MLA_Q_LORA = 256
MLA_KV_LORA = 128
MLA_NOPE = 64
MLA_ROPE = 32
MLA_V = 64
MLA_QK = MLA_NOPE + MLA_ROPE
ROPE_THETA = 10000.0
DIL_HEADS = 8
DIL_HEAD_DIM = 64
DIL_PATTERNS = ((128, 1), (512, 4), (2048, 16))
DIL_HALF = 64
REL_BUCKETS = 32
REL_MAX_DIST = 1024
EPS = 1e-6
NEG_INF = -1e30

_C_QC = 0
_C_KVC = _C_QC + MLA_Q_LORA
_C_KR = _C_KVC + MLA_KV_LORA
_C_QKVB = _C_KR + MLA_ROPE
_C_GATE = _C_QKVB + 3 * DIL_HEADS * DIL_HEAD_DIM
PRE_COLS = MLA_Q_LORA + MLA_KV_LORA + LANES + 3 * DIL_HEADS * DIL_HEAD_DIM
HEAD_PAD = LANES

DIL_BQ = 2 * DIL_HALF
DIL_BK = 4 * DIL_HALF


def _cparams(sem):
    return pltpu.CompilerParams(dimension_semantics=sem, vmem_limit_bytes=VMEM_LIMIT_BYTES)


def _resident(shape, index_map):
    return pl.BlockSpec(shape, index_map, pipeline_mode=pl.Buffered(1))


def _ada_kernel(c_ref, w_ref, b_ref, o_ref):
    c = c_ref[...]
    act = (c * jax.nn.sigmoid(c)).astype(BF16)
    o_ref[0] = jnp.dot(act, w_ref[0].astype(BF16), preferred_element_type=F32) + b_ref[0]


def _ada_call(c, ada_w, ada_b):
    depth, d, six_d = ada_w.shape
    b = c.shape[0]
    n_chunk = six_d // d
    return pl.pallas_call(
        _ada_kernel,
        grid=(depth, n_chunk),
        in_specs=[
            pl.BlockSpec((b, d), lambda l, j: (0, 0)),
            pl.BlockSpec((1, d, d), lambda l, j: (l, 0, j)),
            pl.BlockSpec((1, 1, d), lambda l, j: (l, 0, j)),
        ],
        out_specs=pl.BlockSpec((1, b, d), lambda l, j: (l, 0, j)),
        out_shape=jax.ShapeDtypeStruct((depth, b, six_d), F32),
        compiler_params=_cparams(("arbitrary", "arbitrary")),
        name="ada_mod",
    )(c, ada_w, ada_b.reshape(depth, 1, six_d))


def _rope_kernel(pos_ref, inv_ref, c_ref, s_ref):
    ang = pos_ref[0].astype(F32) * inv_ref[...]
    cosv = jnp.cos(ang)
    sinv = jnp.sin(ang)
    lane = lax.broadcasted_iota(jnp.int32, ang.shape, 1)
    half = MLA_ROPE // 2
    c_ref[0] = jnp.where(lane < MLA_NOPE, 1.0, jnp.where(lane < MLA_QK, cosv, 0.0))
    s_ref[0] = jnp.where(lane < MLA_NOPE, 0.0,
                         jnp.where(lane < MLA_NOPE + half, -sinv, jnp.where(lane < MLA_QK, sinv, 0.0)))


def _rope_call(positions):
    b, s = positions.shape
    half = MLA_ROPE // 2
    inv = ROPE_THETA ** (-jnp.arange(half, dtype=F32) / half)
    inv_lane = jnp.zeros((1, LANES), F32)
    inv_lane = inv_lane.at[0, MLA_NOPE:MLA_NOPE + half].set(inv).at[0, MLA_NOPE + half:MLA_QK].set(inv)
    ts = 512
    spec = pl.BlockSpec((1, ts, LANES), lambda i, j: (i, j, 0))
    return pl.pallas_call(
        _rope_kernel,
        grid=(b, s // ts),
        in_specs=[pl.BlockSpec((1, ts, 1), lambda i, j: (i, j, 0)),
                  pl.BlockSpec((1, LANES), lambda i, j: (0, 0))],
        out_specs=[spec, spec],
        out_shape=[jax.ShapeDtypeStruct((b, s, LANES), F32)] * 2,
        compiler_params=_cparams(("arbitrary", "arbitrary")),
        name="rope_tables",
    )(positions.reshape(b, s, 1), inv_lane)


def _t5_bucket(rel):
    nb = REL_BUCKETS // 2
    max_exact = nb // 2
    ret = jnp.where(rel > 0, nb, 0)
    n = jnp.abs(rel)
    nf = jnp.maximum(n, 1).astype(F32)
    large = max_exact + (jnp.log(nf / max_exact) / math.log(REL_MAX_DIST / max_exact) * (nb - max_exact)).astype(jnp.int32)
    large = jnp.minimum(large, nb - 1)
    return ret + jnp.where(n < max_exact, n, large)


def _bias_kernel(bucket_ref, rb_ref, o_ref):
    h = pl.program_id(1)
    bucket = bucket_ref[0]
    acc = jnp.full(bucket.shape, NEG_INF, F32)
    for bkt in range(REL_BUCKETS):
        acc = jnp.where(bucket == bkt, rb_ref[bkt, h], acc)
    o_ref[0, 0] = acc


def _bias_call(rel_bias):
    qi = jnp.arange(DIL_BQ, dtype=jnp.int32)[:, None]
    kj = jnp.arange(DIL_BK, dtype=jnp.int32)[None, :]
    tabs = []
    for _, dil in DIL_PATTERNS:
        for variant in range(3):
            rel = kj - variant * DIL_HALF - qi
            tabs.append(jnp.where(jnp.abs(rel) <= DIL_HALF, _t5_bucket(rel * dil), -1))
    bucket = jnp.stack(tabs).astype(jnp.int32)
    n_tab = bucket.shape[0]
    return pl.pallas_call(
        _bias_kernel,
        grid=(n_tab, DIL_HEADS),
        in_specs=[pl.BlockSpec((1, DIL_BQ, DIL_BK), lambda t, h: (t, 0, 0)),
                  pl.BlockSpec(memory_space=pltpu.SMEM)],
        out_specs=pl.BlockSpec((1, 1, DIL_BQ, DIL_BK), lambda t, h: (t, h, 0, 0)),
        out_shape=jax.ShapeDtypeStruct((n_tab, DIL_HEADS, DIL_BQ, DIL_BK), F32),
        compiler_params=_cparams(("arbitrary", "arbitrary")),
        name="bias_tables",
    )(bucket, rel_bias)


def _modulated_norm(x, gain, shift, scale):
    r = lax.rsqrt(jnp.mean(x * x, axis=-1, keepdims=True) + EPS)
    return ((x * r) * gain) * (1.0 + scale) + shift


def _rotate(x, cos_t, sin_t, low_half):
    half = MLA_ROPE // 2
    partner = jnp.where(low_half, pltpu.roll(x, LANES - half, 1), pltpu.roll(x, half, 1))
    return x * cos_t + partner * sin_t


def _pre_kernel(x_ref, mod_ref, g1_ref, cos_ref, sin_ref, w_ref, qan_ref, wqb_ref, kvan_ref, wk_ref, wv_ref,
                gq_ref, gk_ref, gqd_ref, gkd_ref,
                q_out, k_out, v_out, qd_out, kd_out, vd_out):
    x = x_ref[0]
    h = _modulated_norm(x, g1_ref[...], mod_ref[0, 0:1, :], mod_ref[0, 1:2, :])
    proj = jnp.dot(h.astype(BF16), w_ref[...], preferred_element_type=F32)

    c0 = MLA_Q_LORA
    c1 = c0 + MLA_KV_LORA
    c2 = c1 + LANES
    qc = proj[:, :c0]
    qcn = (qc * lax.rsqrt(jnp.mean(qc * qc, axis=-1, keepdims=True) + EPS)) * qan_ref[...]
    q_raw = jnp.dot(qcn.astype(BF16), wqb_ref[...], preferred_element_type=F32)
    kvc = proj[:, c0:c1]
    kvn = ((kvc * lax.rsqrt(jnp.mean(kvc * kvc, axis=-1, keepdims=True) + EPS)) * kvan_ref[...]).astype(BF16)
    k_raw = jnp.dot(kvn, wk_ref[...], preferred_element_type=F32)
    v_raw = jnp.dot(kvn, wv_ref[...], preferred_element_type=F32)
    k_rope = proj[:, c1:c2]

    cos_t = cos_ref[0]
    sin_t = sin_ref[0]
    lane = lax.broadcasted_iota(jnp.int32, cos_t.shape, 1)
    low_half = lane < MLA_NOPE + MLA_ROPE // 2
    inv_qk = 1.0 / MLA_QK
    for hd in range(MLA_HEADS):
        sl = slice(hd * HEAD_PAD, (hd + 1) * HEAD_PAD)
        qh = q_raw[:, sl]
        qn = (qh * lax.rsqrt(jnp.sum(qh * qh, axis=-1, keepdims=True) * inv_qk + EPS)) * gq_ref[...]
        q_out[0, :, sl] = _rotate(qn, cos_t, sin_t, low_half).astype(BF16)
        kh = k_raw[:, sl] + k_rope
        kn = (kh * lax.rsqrt(jnp.sum(kh * kh, axis=-1, keepdims=True) * inv_qk + EPS)) * gk_ref[...]
        k_out[0, :, sl] = _rotate(kn, cos_t, sin_t, low_half).astype(BF16)
    v_out[0] = v_raw.astype(BF16)

    n_dil = DIL_HEADS * DIL_HEAD_DIM
    first = lane < DIL_HEAD_DIM
    inv_dh = 1.0 / DIL_HEAD_DIM

    def pair_norm(t, gain):
        sq = t * t
        s0 = jnp.sum(jnp.where(first, sq, 0.0), axis=-1, keepdims=True)
        s1 = jnp.sum(jnp.where(first, 0.0, sq), axis=-1, keepdims=True)
        return (t * lax.rsqrt(jnp.where(first, s0, s1) * inv_dh + EPS)) * gain

    for hp in range(n_dil // LANES):
        sl = slice(hp * LANES, (hp + 1) * LANES)
        qd_out[0, :, sl] = pair_norm(proj[:, c2 + hp * LANES:c2 + (hp + 1) * LANES], gqd_ref[...]).astype(BF16)
        kd_out[0, :, sl] = pair_norm(proj[:, c2 + n_dil + hp * LANES:c2 + n_dil + (hp + 1) * LANES],
                                     gkd_ref[...]).astype(BF16)
    vd_out[0] = proj[:, c2 + 2 * n_dil:c2 + 3 * n_dil].astype(BF16)


def _pre_call(x, mod_l, g1, cos_t, sin_t, w_pre, qan, wqb, kvan, wk, wv, gq, gk, gqd, gkd, tm):
    b, s, d = x.shape
    hp = MLA_HEADS * HEAD_PAD
    n_dil = DIL_HEADS * DIL_HEAD_DIM
    tok = lambda i, j: (i, j, 0)
    const2 = lambda i, j: (0, 0)
    row = lambda n: _resident((1, n), const2)
    return pl.pallas_call(
        _pre_kernel,
        grid=(b, s // tm),
        in_specs=[
            pl.BlockSpec((1, tm, d), tok),
            pl.BlockSpec((1, 6, d), lambda i, j: (i, 0, 0)),
            row(d),
            pl.BlockSpec((1, tm, LANES), tok),
            pl.BlockSpec((1, tm, LANES), tok),
            _resident((d, PRE_COLS), const2),
            row(MLA_Q_LORA),
            _resident((MLA_Q_LORA, hp), const2),
            row(MLA_KV_LORA),
            _resident((MLA_KV_LORA, hp), const2),
            _resident((MLA_KV_LORA, hp), const2),
            row(LANES), row(LANES), row(LANES), row(LANES),
        ],
        out_specs=[
            pl.BlockSpec((1, tm, hp), tok), pl.BlockSpec((1, tm, hp), tok), pl.BlockSpec((1, tm, hp), tok),
            pl.BlockSpec((1, tm, n_dil), tok), pl.BlockSpec((1, tm, n_dil), tok), pl.BlockSpec((1, tm, n_dil), tok),
        ],
        out_shape=[jax.ShapeDtypeStruct((b, s, hp), BF16)] * 3 + [jax.ShapeDtypeStruct((b, s, n_dil), BF16)] * 3,
        compiler_params=_cparams(("arbitrary", "arbitrary")),
        name="pre_proj",
    )(x, mod_l, g1, cos_t, sin_t, w_pre, qan, wqb, kvan, wk, wv, gq, gk, gqd, gkd)


def _mla_kernel(q_ref, k_ref, v_ref, o_ref, *, tk):
    q = q_ref[0]
    tq = q.shape[0]
    n_k = k_ref.shape[1] // tk

    def body(j, carry):
        m, l, acc = carry
        ks = pl.multiple_of(j * tk, tk)
        k = k_ref[0, pl.ds(ks, tk), :]
        v = v_ref[0, pl.ds(ks, tk), :]
        s = lax.dot_general(q, k, (((1,), (1,)), ((), ())), preferred_element_type=F32)
        m_new = jnp.maximum(m, jnp.max(s, axis=-1, keepdims=True))
        alpha = jnp.exp(m - m_new)
        p = jnp.exp(s - m_new)
        l = alpha * l + jnp.sum(p, axis=-1, keepdims=True)
        acc = alpha * acc + jnp.dot(p.astype(BF16), v, preferred_element_type=F32)
        return m_new, l, acc

    init = (jnp.full((tq, 1), -jnp.inf, F32), jnp.zeros((tq, 1), F32), jnp.zeros((tq, HEAD_PAD), F32))
    _, l, acc = lax.fori_loop(0, n_k, body, init)
    o_ref[0] = (acc / l).astype(BF16)


def _mla_call(q, k, v, tq, tk):
    b, s, _ = q.shape
    return pl.pallas_call(
        functools.partial(_mla_kernel, tk=tk),
        grid=(b, MLA_HEADS, s // tq),
        in_specs=[
            pl.BlockSpec((1, tq, HEAD_PAD), lambda i, h, j: (i, j, h)),
            pl.BlockSpec((1, s, HEAD_PAD), lambda i, h, j: (i, 0, h)),
            pl.BlockSpec((1, s, HEAD_PAD), lambda i, h, j: (i, 0, h)),
        ],
        out_specs=pl.BlockSpec((1, tq, HEAD_PAD), lambda i, h, j: (i, j, h)),
        out_shape=jax.ShapeDtypeStruct(q.shape, BF16),
        compiler_params=_cparams(("arbitrary", "arbitrary", "arbitrary")),
        name="mla_attn",
    )(q, k, v)


def _dil_kernel(q_ref, k_ref, v_ref, bias_ref, o_ref, qf, kf, vf, m_s, l_s, n_s):
    s_len = q_ref.shape[1]
    qf[...] = q_ref[0].astype(F32)
    kf[...] = k_ref[0].astype(F32)
    vf[...] = v_ref[0].astype(F32)
    lane = lax.broadcasted_iota(jnp.int32, (DIL_BQ, LANES), 1)
    first = lane < DIL_HEAD_DIM

    def band_tile(q_t, k_t, v_t, tab):
        parts = []
        for hh in range(2):
            sel = first if hh == 0 else jnp.logical_not(first)
            qm = jnp.where(sel, q_t, jnp.zeros_like(q_t))
            s = lax.dot_general(qm, k_t, (((1,), (1,)), ((), ())), preferred_element_type=F32)
            s = s + bias_ref[tab, hh]
            m = jnp.max(s, axis=-1, keepdims=True)
            p = jnp.exp(s - m)
            l = jnp.sum(p, axis=-1, keepdims=True)
            pv = jnp.dot(p.astype(BF16), v_t, preferred_element_type=F32)
            parts.append((m, l, pv))
        (m0, l0, n0), (m1, l1, n1) = parts
        return jnp.where(first, m0, m1), jnp.where(first, l0, l1), jnp.where(first, n0, n1)

    def variant_of(n, n_blk):
        return jnp.where(n == 0, 0, jnp.where(n == n_blk - 1, 2, 1))

    n_blk0 = s_len // DIL_BQ

    def body0(n, carry):
        i0 = pl.multiple_of(n * DIL_BQ, DIL_BQ)
        ks = pl.multiple_of(jnp.clip(i0 - DIL_HALF, 0, s_len - DIL_BK), DIL_HALF)
        m, l, num = band_tile(q_ref[0, pl.ds(i0, DIL_BQ), :], k_ref[0, pl.ds(ks, DIL_BK), :],
                              v_ref[0, pl.ds(ks, DIL_BK), :], variant_of(n, n_blk0))
        m_s[pl.ds(i0, DIL_BQ), :] = m
        l_s[pl.ds(i0, DIL_BQ), :] = l
        n_s[pl.ds(i0, DIL_BQ), :] = num
        return carry

    lax.fori_loop(0, n_blk0, body0, 0)

    for p_idx in range(1, len(DIL_PATTERNS)):
        dil = DIL_PATTERNS[p_idx][1]
        sub_len = s_len // dil
        n_blk = sub_len // DIL_BQ

        def body(t, carry, dil=dil, sub_len=sub_len, n_blk=n_blk, p_idx=p_idx):
            r = t // n_blk
            n = t - r * n_blk
            i0 = n * DIL_BQ
            ks = jnp.clip(i0 - DIL_HALF, 0, sub_len - DIL_BK)
            rows_q = pl.ds(r + dil * i0, DIL_BQ, stride=dil)
            rows_k = pl.ds(r + dil * ks, DIL_BK, stride=dil)
            m_t, l_t, n_t = band_tile(qf[rows_q, :].astype(BF16), kf[rows_k, :].astype(BF16),
                                      vf[rows_k, :].astype(BF16), 3 * p_idx + variant_of(n, n_blk))
            m_o = m_s[rows_q, :]
            m_n = jnp.maximum(m_o, m_t)
            a = jnp.exp(m_o - m_n)
            c = jnp.exp(m_t - m_n)
            m_s[rows_q, :] = m_n
            l_s[rows_q, :] = a * l_s[rows_q, :] + c * l_t
            n_s[rows_q, :] = a * n_s[rows_q, :] + c * n_t
            return carry

        lax.fori_loop(0, dil * n_blk, body, 0)

    o_ref[0] = (n_s[...] / l_s[...]).astype(BF16)


def _dil_call(qd, kd, vd, bias_tabs):
    b, s, n_dil = qd.shape
    n_pair = n_dil // LANES
    n_tab = bias_tabs.shape[0]
    blk = pl.BlockSpec((1, s, LANES), lambda hp, i: (i, 0, hp))
    return pl.pallas_call(
        _dil_kernel,
        grid=(n_pair, b),
        in_specs=[blk, blk, blk,
                  pl.BlockSpec((n_tab, 2, DIL_BQ, DIL_BK), lambda hp, i: (0, hp, 0, 0))],
        out_specs=blk,
        out_shape=jax.ShapeDtypeStruct(qd.shape, BF16),
        scratch_shapes=[pltpu.VMEM((s, LANES), F32)] * 6,
        compiler_params=_cparams(("arbitrary", "arbitrary")),
        name="dil_attn",
    )(qd, kd, vd, bias_tabs)


def _post_kernel(x_ref, ya_ref, yb_ref, mod_ref, g1_ref, wg_ref, wa_ref, wb_ref, wo_ref, o_ref):
    x = x_ref[0]
    h = _modulated_norm(x, g1_ref[...], mod_ref[0, 0:1, :], mod_ref[0, 1:2, :])
    gates = jnp.dot(h.astype(BF16), wg_ref[...], preferred_element_type=F32)
    d = x.shape[-1]
    a = jnp.dot(ya_ref[0], wa_ref[...], preferred_element_type=F32)
    bb = jnp.dot(yb_ref[0], wb_ref[...], preferred_element_type=F32)
    merged = jax.nn.sigmoid(gates[:, :d]) * a + jax.nn.sigmoid(gates[:, d:]) * bb
    upd = jnp.dot(merged.astype(BF16), wo_ref[...], preferred_element_type=F32)
    o_ref[0] = x + mod_ref[0, 2:3, :] * upd


def _post_call(x, ya, yb, mod_l, g1, wg, wa, wb, wo, tm):
    b, s, d = x.shape
    tok = lambda i, j: (i, j, 0)
    const2 = lambda i, j: (0, 0)
    return pl.pallas_call(
        _post_kernel,
        grid=(b, s // tm),
        in_specs=[
            pl.BlockSpec((1, tm, d), tok),
            pl.BlockSpec((1, tm, ya.shape[-1]), tok),
            pl.BlockSpec((1, tm, yb.shape[-1]), tok),
            pl.BlockSpec((1, 6, d), lambda i, j: (i, 0, 0)),
            _resident((1, d), const2),
            _resident(wg.shape, const2),
            _resident(wa.shape, const2),
            _resident(wb.shape, const2),
            _resident(wo.shape, const2),
        ],
        out_specs=pl.BlockSpec((1, tm, d), tok),
        out_shape=jax.ShapeDtypeStruct(x.shape, F32),
        compiler_params=_cparams(("arbitrary", "arbitrary")),
        name="attn_out",
    )(x, ya, yb, mod_l, g1, wg, wa, wb, wo)


def _ffn_kernel(x_ref, mod_ref, g2_ref, wg_ref, wu_ref, wd_ref, o_ref):
    x = x_ref[0]
    h = _modulated_norm(x, g2_ref[...], mod_ref[0, 3:4, :], mod_ref[0, 4:5, :]).astype(BF16)
    g = jnp.dot(h, wg_ref[...], preferred_element_type=F32)
    u = jnp.dot(h, wu_ref[...], preferred_element_type=F32)
    act = (g * jax.nn.sigmoid(g)) * u
    upd = jnp.dot(act.astype(BF16), wd_ref[...], preferred_element_type=F32)
    o_ref[0] = x + mod_ref[0, 5:6, :] * upd


def _ffn_call(x, mod_l, g2, wg, wu, wd, tm):
    b, s, d = x.shape
    tok = lambda i, j: (i, j, 0)
    const2 = lambda i, j: (0, 0)
    return pl.pallas_call(
        _ffn_kernel,
        grid=(b, s // tm),
        in_specs=[
            pl.BlockSpec((1, tm, d), tok),
            pl.BlockSpec((1, 6, d), lambda i, j: (i, 0, 0)),
            _resident((1, d), const2),
            _resident(wg.shape, const2),
            _resident(wu.shape, const2),
            _resident(wd.shape, const2),
        ],
        out_specs=pl.BlockSpec((1, tm, d), tok),
        out_shape=jax.ShapeDtypeStruct(x.shape, F32),
        compiler_params=_cparams(("arbitrary", "arbitrary")),
        name="swiglu",
    )(x, mod_l, g2, wg, wu, wd)


def _pad_heads(w, width):
    rows = w.shape[0]
    w = w.reshape(rows, MLA_HEADS, width)
    return jnp.pad(w, ((0, 0), (0, 0), (0, HEAD_PAD - width))).reshape(rows, MLA_HEADS * HEAD_PAD)


def _pad_lanes(g, scale=1.0):
    return jnp.pad(g * scale, (0, LANES - g.shape[0])).reshape(1, LANES)


def kernel(x, c, positions, rel_bias, norm1_g, norm2_g, ada_w, ada_b, w_in, q_a_norm, w_q_b, kv_a_norm, w_kv_b,
           q_norm_a, k_norm_a, q_norm_b, k_norm_b, w_branch_a, w_branch_b, w_out, w_ffn_gate, w_ffn_up,
           w_ffn_down):
    depth = w_in.shape[0]
    b, s, d = x.shape

    mod = _ada_call(c, ada_w, ada_b).reshape(depth, b, 6, d)
    cos_t, sin_t = _rope_call(positions)
    bias_tabs = _bias_call(rel_bias)

    for l in range(depth):
        wl = w_in[l]
        k_rope_cols = jnp.pad(wl[:, _C_KR:_C_QKVB], ((0, 0), (MLA_NOPE, LANES - MLA_QK)))
        w_pre = jnp.concatenate([wl[:, _C_QC:_C_KR], k_rope_cols, wl[:, _C_QKVB:_C_GATE]], axis=1).astype(BF16)
        w_gates = wl[:, _C_GATE:].astype(BF16)
        wqb = _pad_heads(w_q_b[l], MLA_QK).astype(BF16)
        wkv = w_kv_b[l].reshape(MLA_KV_LORA, MLA_HEADS, MLA_NOPE + MLA_V)
        wk = _pad_heads(wkv[:, :, :MLA_NOPE].reshape(MLA_KV_LORA, -1), MLA_NOPE).astype(BF16)
        wv = _pad_heads(wkv[:, :, MLA_NOPE:].reshape(MLA_KV_LORA, -1), MLA_V).astype(BF16)
        gq = _pad_lanes(q_norm_a[l], MLA_QK ** -0.5)
        gk = _pad_lanes(k_norm_a[l])
        gqd = jnp.tile(q_norm_b[l] * DIL_HEAD_DIM ** -0.5, 2).reshape(1, LANES)
        gkd = jnp.tile(k_norm_b[l], 2).reshape(1, LANES)
        wa = jnp.pad(w_branch_a[l].reshape(MLA_HEADS, MLA_V, d), ((0, 0), (0, HEAD_PAD - MLA_V), (0, 0)))
        wa = wa.reshape(MLA_HEADS * HEAD_PAD, d).astype(BF16)
        g1 = norm1_g[l].reshape(1, d)
        g2 = norm2_g[l].reshape(1, d)

        q, k, v, qd, kd, vd = _pre_call(x, mod[l], g1, cos_t, sin_t, w_pre, q_a_norm[l].reshape(1, -1), wqb,
                                        kv_a_norm[l].reshape(1, -1), wk, wv, gq, gk, gqd, gkd, tm=512)
        ya = _mla_call(q, k, v, tq=512, tk=512)
        yb = _dil_call(qd, kd, vd, bias_tabs)
        x = _post_call(x, ya, yb, mod[l], g1, w_gates, wa, w_branch_b[l].astype(BF16), w_out[l].astype(BF16), tm=512)
        x = _ffn_call(x, mod[l], g2, w_ffn_gate[l].astype(BF16), w_ffn_up[l].astype(BF16),
                      w_ffn_down[l].astype(BF16), tm=512)
    return x
```

```python
import functools
import math

import jax
import jax.numpy as jnp
import numpy as np
from jax import lax
from jax.experimental import pallas as pl
from jax.experimental.pallas import tpu as pltpu

F32 = jnp.float32
BF16 = jnp.bfloat16

LANES = 128
VMEM_LIMIT_BYTES = 56 * 1024 * 1024

D_MODEL = 1024
MLA_HEADS = 8
MLA_Q_LORA = 256
MLA_KV_LORA = 128
MLA_NOPE = 64
MLA_ROPE = 32
MLA_V = 64
MLA_QK = MLA_NOPE + MLA_ROPE
ROPE_THETA = 10000.0
DIL_HEADS = 8
DIL_HEAD_DIM = 64
DIL_PATTERNS = ((128, 1), (512, 4), (2048, 16))
DIL_HALF = 64
REL_BUCKETS = 32
REL_MAX_DIST = 1024
EPS = 1e-6
NEG_INF = -1e30
LOG2E = math.log2(math.e)

_C_QC = 0
_C_KVC = _C_QC + MLA_Q_LORA
_C_KR = _C_KVC + MLA_KV_LORA
_C_QKVB = _C_KR + MLA_ROPE
_C_GATE = _C_QKVB + 3 * DIL_HEADS * DIL_HEAD_DIM
PRE_COLS = MLA_Q_LORA + MLA_KV_LORA + LANES + 3 * DIL_HEADS * DIL_HEAD_DIM
HEAD_PAD = LANES

DIL_BQ = 2 * DIL_HALF
DIL_BK = 4 * DIL_HALF
DIL_UNROLL = 8


def _cparams(sem):
    return pltpu.CompilerParams(dimension_semantics=sem, vmem_limit_bytes=VMEM_LIMIT_BYTES)


def _resident(shape, index_map):
    return pl.BlockSpec(shape, index_map, pipeline_mode=pl.Buffered(1))


def _ada_kernel(c_ref, w_ref, b_ref, o_ref):
    c = c_ref[...]
    act = (c * jax.nn.sigmoid(c)).astype(BF16)
    o_ref[0] = jnp.dot(act, w_ref[0].astype(BF16), preferred_element_type=F32) + b_ref[0]


def _ada_call(c, ada_w, ada_b):
    depth, d, six_d = ada_w.shape
    b = c.shape[0]
    n_chunk = six_d // d
    return pl.pallas_call(
        _ada_kernel,
        grid=(depth, n_chunk),
        in_specs=[
            pl.BlockSpec((b, d), lambda l, j: (0, 0)),
            pl.BlockSpec((1, d, d), lambda l, j: (l, 0, j)),
            pl.BlockSpec((1, 1, d), lambda l, j: (l, 0, j)),
        ],
        out_specs=pl.BlockSpec((1, b, d), lambda l, j: (l, 0, j)),
        out_shape=jax.ShapeDtypeStruct((depth, b, six_d), F32),
        compiler_params=_cparams(("arbitrary", "arbitrary")),
        name="ada_mod",
    )(c, ada_w, ada_b.reshape(depth, 1, six_d))


def _rope_kernel(pos_ref, inv_ref, c_ref, s_ref):
    ang = pos_ref[0].astype(F32) * inv_ref[...]
    cosv = jnp.cos(ang)
    sinv = jnp.sin(ang)
    lane = lax.broadcasted_iota(jnp.int32, ang.shape, 1)
    half = MLA_ROPE // 2
    c_ref[0] = jnp.where(lane < MLA_NOPE, 1.0, jnp.where(lane < MLA_QK, cosv, 0.0))
    s_ref[0] = jnp.where(lane < MLA_NOPE, 0.0,
                         jnp.where(lane < MLA_NOPE + half, -sinv, jnp.where(lane < MLA_QK, sinv, 0.0)))


def _rope_call(positions):
    b, s = positions.shape
    half = MLA_ROPE // 2
    inv = ROPE_THETA ** (-jnp.arange(half, dtype=F32) / half)
    inv_lane = jnp.zeros((1, LANES), F32)
    inv_lane = inv_lane.at[0, MLA_NOPE:MLA_NOPE + half].set(inv).at[0, MLA_NOPE + half:MLA_QK].set(inv)
    ts = 512
    spec = pl.BlockSpec((1, ts, LANES), lambda i, j: (i, j, 0))
    return pl.pallas_call(
        _rope_kernel,
        grid=(b, s // ts),
        in_specs=[pl.BlockSpec((1, ts, 1), lambda i, j: (i, j, 0)),
                  pl.BlockSpec((1, LANES), lambda i, j: (0, 0))],
        out_specs=[spec, spec],
        out_shape=[jax.ShapeDtypeStruct((b, s, LANES), F32)] * 2,
        compiler_params=_cparams(("arbitrary", "arbitrary")),
        name="rope_tables",
    )(positions.reshape(b, s, 1), inv_lane)


def _t5_bucket(rel):
    nb = REL_BUCKETS // 2
    max_exact = nb // 2
    ret = jnp.where(rel > 0, nb, 0)
    n = jnp.abs(rel)
    nf = jnp.maximum(n, 1).astype(F32)
    large = max_exact + (jnp.log(nf / max_exact) / math.log(REL_MAX_DIST / max_exact) * (nb - max_exact)).astype(jnp.int32)
    large = jnp.minimum(large, nb - 1)
    return ret + jnp.where(n < max_exact, n, large)


def _bias_kernel(bucket_ref, rb_ref, o_ref):
    h = pl.program_id(1)
    bucket = bucket_ref[0]
    acc = jnp.full(bucket.shape, NEG_INF, F32)
    for bkt in range(REL_BUCKETS):
        acc = jnp.where(bucket == bkt, rb_ref[bkt, h], acc)
    o_ref[0, 0] = acc * LOG2E


def _bias_call(rel_bias):
    qi = jnp.arange(DIL_BQ, dtype=jnp.int32)[:, None]
    kj = jnp.arange(DIL_BK, dtype=jnp.int32)[None, :]
    tabs = []
    for _, dil in DIL_PATTERNS:
        for variant in range(3):
            rel = kj - variant * DIL_HALF - qi
            tabs.append(jnp.where(jnp.abs(rel) <= DIL_HALF, _t5_bucket(rel * dil), -1))
    bucket = jnp.stack(tabs).astype(jnp.int32)
    n_tab = bucket.shape[0]
    return pl.pallas_call(
        _bias_kernel,
        grid=(n_tab, DIL_HEADS),
        in_specs=[pl.BlockSpec((1, DIL_BQ, DIL_BK), lambda t, h: (t, 0, 0)),
                  pl.BlockSpec(memory_space=pltpu.SMEM)],
        out_specs=pl.BlockSpec((1, 1, DIL_BQ, DIL_BK), lambda t, h: (t, h, 0, 0)),
        out_shape=jax.ShapeDtypeStruct((n_tab, DIL_HEADS, DIL_BQ, DIL_BK), F32),
        compiler_params=_cparams(("arbitrary", "arbitrary")),
        name="bias_tables",
    )(bucket, rel_bias)


def _modulated_norm(x, gain, shift, scale):
    r = lax.rsqrt(jnp.mean(x * x, axis=-1, keepdims=True) + EPS)
    return ((x * r) * gain) * (1.0 + scale) + shift


PRE_SUB = 256


def _group_sumsq(t, ones_ref):
    return jnp.dot((t * t).astype(BF16), ones_ref[...], preferred_element_type=F32)


def _pre_kernel(x_ref, mod_ref, g1_ref, cos_ref, sin_ref, w_ref, qan_ref, wqq_ref, kvan_ref, wkv_ref,
                gains_ref, ones_mla_ref, ones_dil_ref,
                q_out, k_out, v_out, qd_out, kd_out, vd_out):
    tm = x_ref.shape[1]
    hp = MLA_HEADS * HEAD_PAD
    n_dil = DIL_HEADS * DIL_HEAD_DIM
    grp = 2 * LANES
    c0 = MLA_Q_LORA
    c1 = c0 + MLA_KV_LORA
    c2 = c1 + LANES
    half = MLA_ROPE // 2
    inv_qk = 1.0 / MLA_QK
    inv_dh = 1.0 / DIL_HEAD_DIM
    lane = lax.broadcasted_iota(jnp.int32, (PRE_SUB, LANES), 1)
    low_half = lane < MLA_NOPE + half

    for sub in range(tm // PRE_SUB):
        rs = slice(sub * PRE_SUB, (sub + 1) * PRE_SUB)
        x = x_ref[0, rs, :]
        h = _modulated_norm(x, g1_ref[...], mod_ref[0, 0:1, :], mod_ref[0, 1:2, :])
        proj = jnp.dot(h.astype(BF16), w_ref[...], preferred_element_type=F32)

        qc = proj[:, :c0]
        qcn = (qc * lax.rsqrt(jnp.mean(qc * qc, axis=-1, keepdims=True) + EPS)) * qan_ref[...]
        qq = jnp.dot(qcn.astype(BF16), wqq_ref[...], preferred_element_type=F32)
        kvc = proj[:, c0:c1]
        kvn = (kvc * lax.rsqrt(jnp.mean(kvc * kvc, axis=-1, keepdims=True) + EPS)) * kvan_ref[...]
        kv = jnp.dot(kvn.astype(BF16), wkv_ref[...], preferred_element_type=F32)
        k_rope = proj[:, c1:c2]
        k_rope_sw = jnp.where(low_half, pltpu.roll(k_rope, LANES - half, 1), pltpu.roll(k_rope, half, 1))

        cos2 = jnp.concatenate([cos_ref[0, rs, :]] * 2, axis=1)
        sin2 = jnp.concatenate([sin_ref[0, rs, :]] * 2, axis=1)
        aq = gains_ref[0:1, :] * cos2
        bq = gains_ref[1:2, :] * sin2
        ak = gains_ref[2:3, :] * cos2
        bk = gains_ref[3:4, :] * sin2
        kr2 = jnp.concatenate([k_rope] * 2, axis=1)
        krs2 = jnp.concatenate([k_rope_sw] * 2, axis=1)
        for g in range(hp // grp):
            sl = slice(g * grp, (g + 1) * grp)
            qh = qq[:, sl]
            rq = lax.rsqrt(_group_sumsq(qh, ones_mla_ref) * inv_qk + EPS)
            q_out[0, rs, sl] = (rq * (qh * aq + qq[:, hp + g * grp:hp + (g + 1) * grp] * bq)).astype(BF16)
            kh = kv[:, sl] + kr2
            rk = lax.rsqrt(_group_sumsq(kh, ones_mla_ref) * inv_qk + EPS)
            k_out[0, rs, sl] = (rk * (kh * ak + krs2 * bk)).astype(BF16)
        v_out[0, rs, :] = kv[:, hp:].astype(BF16)

        for g in range(n_dil // grp):
            sl = slice(g * grp, (g + 1) * grp)
            qd = proj[:, c2 + g * grp:c2 + (g + 1) * grp]
            rqd = lax.rsqrt(_group_sumsq(qd, ones_dil_ref) * inv_dh + EPS)
            qd_out[0, rs, sl] = ((qd * rqd) * gains_ref[4:5, :]).astype(BF16)
            kd = proj[:, c2 + n_dil + g * grp:c2 + n_dil + (g + 1) * grp]
            rkd = lax.rsqrt(_group_sumsq(kd, ones_dil_ref) * inv_dh + EPS)
            kd_out[0, rs, sl] = ((kd * rkd) * gains_ref[5:6, :]).astype(BF16)
        vd_out[0, rs, :] = proj[:, c2 + 2 * n_dil:c2 + 3 * n_dil].astype(BF16)


def _pre_call(x, mod_l, g1, cos_t, sin_t, w_pre, qan, wqq, kvan, wkv, gains, ones_mla, ones_dil, tm):
    b, s, d = x.shape
    hp = MLA_HEADS * HEAD_PAD
    n_dil = DIL_HEADS * DIL_HEAD_DIM
    tok = lambda i, j: (i, j, 0)
    const2 = lambda i, j: (0, 0)
    row = lambda n: _resident((1, n), const2)
    return pl.pallas_call(
        _pre_kernel,
        grid=(b, s // tm),
        in_specs=[
            pl.BlockSpec((1, tm, d), tok),
            pl.BlockSpec((1, 6, d), lambda i, j: (i, 0, 0)),
            row(d),
            pl.BlockSpec((1, tm, LANES), tok),
            pl.BlockSpec((1, tm, LANES), tok),
            _resident((d, PRE_COLS), const2),
            row(MLA_Q_LORA),
            _resident((MLA_Q_LORA, 2 * hp), const2),
            row(MLA_KV_LORA),
            _resident((MLA_KV_LORA, 2 * hp), const2),
            _resident(gains.shape, const2),
            _resident(ones_mla.shape, const2),
            _resident(ones_dil.shape, const2),
        ],
        out_specs=[
            pl.BlockSpec((1, tm, hp), tok), pl.BlockSpec((1, tm, hp), tok), pl.BlockSpec((1, tm, hp), tok),
            pl.BlockSpec((1, tm, n_dil), tok), pl.BlockSpec((1, tm, n_dil), tok), pl.BlockSpec((1, tm, n_dil), tok),
        ],
        out_shape=[jax.ShapeDtypeStruct((b, s, hp), BF16)] * 3 + [jax.ShapeDtypeStruct((b, s, n_dil), BF16)] * 3,
        compiler_params=_cparams(("arbitrary", "arbitrary")),
        name="pre_proj",
    )(x, mod_l, g1, cos_t, sin_t, w_pre, qan, wqq, kvan, wkv, gains, ones_mla, ones_dil)


def _mla_kernel(q_ref, k_ref, v_ref, o_ref, *, tk):
    q = q_ref[0]
    tq = q.shape[0]
    n_k = k_ref.shape[1] // tk

    def body(j, carry):
        m, l, acc = carry
        ks = pl.multiple_of(j * tk, tk)
        k = k_ref[0, pl.ds(ks, tk), :]
        v = v_ref[0, pl.ds(ks, tk), :]
        s = lax.dot_general(q, k, (((1,), (1,)), ((), ())), preferred_element_type=F32)
        m_new = jnp.maximum(m, jnp.max(s, axis=-1, keepdims=True))
        alpha = jnp.exp2(m - m_new)
        p = jnp.exp2(s - m_new)
        l = alpha * l + jnp.sum(p, axis=-1, keepdims=True)
        acc = alpha * acc + jnp.dot(p.astype(BF16), v, preferred_element_type=F32)
        return m_new, l, acc

    init = (jnp.full((tq, 1), -jnp.inf, F32), jnp.zeros((tq, 1), F32), jnp.zeros((tq, HEAD_PAD), F32))
    _, l, acc = lax.fori_loop(0, n_k, body, init, unroll=True)
    o_ref[0] = (acc / l).astype(BF16)


def _mla_call(q, k, v, tq, tk):
    b, s, _ = q.shape
    return pl.pallas_call(
        functools.partial(_mla_kernel, tk=tk),
        grid=(b, MLA_HEADS, s // tq),
        in_specs=[
            pl.BlockSpec((1, tq, HEAD_PAD), lambda i, h, j: (i, j, h)),
            pl.BlockSpec((1, s, HEAD_PAD), lambda i, h, j: (i, 0, h)),
            pl.BlockSpec((1, s, HEAD_PAD), lambda i, h, j: (i, 0, h)),
        ],
        out_specs=pl.BlockSpec((1, tq, HEAD_PAD), lambda i, h, j: (i, j, h)),
        out_shape=jax.ShapeDtypeStruct(q.shape, BF16),
        compiler_params=_cparams(("arbitrary", "arbitrary", "arbitrary")),
        name="mla_attn",
    )(q, k, v)


def _dil_kernel(q_ref, k_ref, v_ref, bias_ref, o_ref, qf, kf, vf, m_s, l_s, n_s):
    s_len = q_ref.shape[1]
    qf[...] = q_ref[0].astype(F32)
    kf[...] = k_ref[0].astype(F32)
    vf[...] = v_ref[0].astype(F32)
    lane = lax.broadcasted_iota(jnp.int32, (DIL_BQ, LANES), 1)
    first = lane < DIL_HEAD_DIM

    def band_tile(q_t, k_t, v_t, tab):
        parts = []
        for hh in range(2):
            sel = first if hh == 0 else jnp.logical_not(first)
            qm = jnp.where(sel, q_t, jnp.zeros_like(q_t))
            s = lax.dot_general(qm, k_t, (((1,), (1,)), ((), ())), preferred_element_type=F32)
            s = s + bias_ref[tab, hh]
            m = jnp.max(s, axis=-1, keepdims=True)
            p = jnp.exp2(s - m)
            l = jnp.sum(p, axis=-1, keepdims=True)
            pv = jnp.dot(p.astype(BF16), v_t, preferred_element_type=F32)
            parts.append((m, l, pv))
        (m0, l0, n0), (m1, l1, n1) = parts
        return jnp.where(first, m0, m1), jnp.where(first, l0, l1), jnp.where(first, n0, n1)

    def variant_of(n, n_blk):
        return jnp.where(n == 0, 0, jnp.where(n == n_blk - 1, 2, 1))

    n_blk0 = s_len // DIL_BQ

    def body0(n, carry):
        i0 = pl.multiple_of(n * DIL_BQ, DIL_BQ)
        ks = pl.multiple_of(jnp.clip(i0 - DIL_HALF, 0, s_len - DIL_BK), DIL_HALF)
        m, l, num = band_tile(q_ref[0, pl.ds(i0, DIL_BQ), :], k_ref[0, pl.ds(ks, DIL_BK), :],
                              v_ref[0, pl.ds(ks, DIL_BK), :], variant_of(n, n_blk0))
        m_s[pl.ds(i0, DIL_BQ), :] = m
        l_s[pl.ds(i0, DIL_BQ), :] = l
        n_s[pl.ds(i0, DIL_BQ), :] = num
        return carry

    lax.fori_loop(0, n_blk0, body0, 0, unroll=DIL_UNROLL)

    for p_idx in range(1, len(DIL_PATTERNS)):
        dil = DIL_PATTERNS[p_idx][1]
        sub_len = s_len // dil
        n_blk = sub_len // DIL_BQ

        def body(t, carry, dil=dil, sub_len=sub_len, n_blk=n_blk, p_idx=p_idx):
            r = t // n_blk
            n = t - r * n_blk
            i0 = n * DIL_BQ
            ks = jnp.clip(i0 - DIL_HALF, 0, sub_len - DIL_BK)
            rows_q = pl.ds(r + dil * i0, DIL_BQ, stride=dil)
            rows_k = pl.ds(r + dil * ks, DIL_BK, stride=dil)
            m_t, l_t, n_t = band_tile(qf[rows_q, :].astype(BF16), kf[rows_k, :].astype(BF16),
                                      vf[rows_k, :].astype(BF16), 3 * p_idx + variant_of(n, n_blk))
            m_o = m_s[rows_q, :]
            m_n = jnp.maximum(m_o, m_t)
            a = jnp.exp2(m_o - m_n)
            c = jnp.exp2(m_t - m_n)
            m_s[rows_q, :] = m_n
            l_s[rows_q, :] = a * l_s[rows_q, :] + c * l_t
            n_s[rows_q, :] = a * n_s[rows_q, :] + c * n_t
            return carry

        lax.fori_loop(0, dil * n_blk, body, 0, unroll=DIL_UNROLL)

    o_ref[0] = (n_s[...] / l_s[...]).astype(BF16)


def _dil_call(qd, kd, vd, bias_tabs):
    b, s, n_dil = qd.shape
    n_pair = n_dil // LANES
    n_tab = bias_tabs.shape[0]
    blk = pl.BlockSpec((1, s, LANES), lambda hp, i: (i, 0, hp))
    return pl.pallas_call(
        _dil_kernel,
        grid=(n_pair, b),
        in_specs=[blk, blk, blk,
                  pl.BlockSpec((n_tab, 2, DIL_BQ, DIL_BK), lambda hp, i: (0, hp, 0, 0))],
        out_specs=blk,
        out_shape=jax.ShapeDtypeStruct(qd.shape, BF16),
        scratch_shapes=[pltpu.VMEM((s, LANES), F32)] * 6,
        compiler_params=_cparams(("arbitrary", "arbitrary")),
        name="dil_attn",
    )(qd, kd, vd, bias_tabs)


def _post_kernel(x_ref, ya_ref, yb_ref, mod_ref, g1_ref, wg_ref, wa_ref, wb_ref, wo_ref, o_ref):
    x = x_ref[0]
    h = _modulated_norm(x, g1_ref[...], mod_ref[0, 0:1, :], mod_ref[0, 1:2, :])
    gates = jnp.dot(h.astype(BF16), wg_ref[...], preferred_element_type=F32)
    d = x.shape[-1]
    a = jnp.dot(ya_ref[0], wa_ref[...], preferred_element_type=F32)
    bb = jnp.dot(yb_ref[0], wb_ref[...], preferred_element_type=F32)
    merged = jax.nn.sigmoid(gates[:, :d]) * a + jax.nn.sigmoid(gates[:, d:]) * bb
    upd = jnp.dot(merged.astype(BF16), wo_ref[...], preferred_element_type=F32)
    o_ref[0] = x + mod_ref[0, 2:3, :] * upd


def _post_call(x, ya, yb, mod_l, g1, wg, wa, wb, wo, tm):
    b, s, d = x.shape
    tok = lambda i, j: (i, j, 0)
    const2 = lambda i, j: (0, 0)
    return pl.pallas_call(
        _post_kernel,
        grid=(b, s // tm),
        in_specs=[
            pl.BlockSpec((1, tm, d), tok),
            pl.BlockSpec((1, tm, ya.shape[-1]), tok),
            pl.BlockSpec((1, tm, yb.shape[-1]), tok),
            pl.BlockSpec((1, 6, d), lambda i, j: (i, 0, 0)),
            _resident((1, d), const2),
            _resident(wg.shape, const2),
            _resident(wa.shape, const2),
            _resident(wb.shape, const2),
            _resident(wo.shape, const2),
        ],
        out_specs=pl.BlockSpec((1, tm, d), tok),
        out_shape=jax.ShapeDtypeStruct(x.shape, F32),
        compiler_params=_cparams(("arbitrary", "arbitrary")),
        name="attn_out",
    )(x, ya, yb, mod_l, g1, wg, wa, wb, wo)


def _ffn_kernel(x_ref, mod_ref, g2_ref, wg_ref, wu_ref, wd_ref, o_ref):
    x = x_ref[0]
    h = _modulated_norm(x, g2_ref[...], mod_ref[0, 3:4, :], mod_ref[0, 4:5, :]).astype(BF16)
    g = jnp.dot(h, wg_ref[...], preferred_element_type=F32)
    u = jnp.dot(h, wu_ref[...], preferred_element_type=F32)
    act = (g * jax.nn.sigmoid(g)) * u
    upd = jnp.dot(act.astype(BF16), wd_ref[...], preferred_element_type=F32)
    o_ref[0] = x + mod_ref[0, 5:6, :] * upd


def _ffn_call(x, mod_l, g2, wg, wu, wd, tm):
    b, s, d = x.shape
    tok = lambda i, j: (i, j, 0)
    const2 = lambda i, j: (0, 0)
    return pl.pallas_call(
        _ffn_kernel,
        grid=(b, s // tm),
        in_specs=[
            pl.BlockSpec((1, tm, d), tok),
            pl.BlockSpec((1, 6, d), lambda i, j: (i, 0, 0)),
            _resident((1, d), const2),
            _resident(wg.shape, const2),
            _resident(wu.shape, const2),
            _resident(wd.shape, const2),
        ],
        out_specs=pl.BlockSpec((1, tm, d), tok),
        out_shape=jax.ShapeDtypeStruct(x.shape, F32),
        compiler_params=_cparams(("arbitrary", "arbitrary")),
        name="swiglu",
    )(x, mod_l, g2, wg, wu, wd)


def _pad_heads(w, width):
    rows = w.shape[0]
    w = w.reshape(rows, MLA_HEADS, width)
    return jnp.pad(w, ((0, 0), (0, 0), (0, HEAD_PAD - width))).reshape(rows, MLA_HEADS * HEAD_PAD)


def _pad_lanes(g):
    return jnp.pad(g, (0, LANES - g.shape[0]))


def _swap_rope(g):
    half = MLA_ROPE // 2
    return jnp.concatenate([jnp.zeros((MLA_NOPE,), g.dtype), g[MLA_NOPE + half:], g[MLA_NOPE:MLA_NOPE + half]])


def _block_ones(block, size=2 * LANES):
    idx = np.arange(size) // block
    return jnp.asarray(idx[:, None] == idx[None, :], BF16)


def kernel(x, c, positions, rel_bias, norm1_g, norm2_g, ada_w, ada_b, w_in, q_a_norm, w_q_b, kv_a_norm, w_kv_b,
           q_norm_a, k_norm_a, q_norm_b, k_norm_b, w_branch_a, w_branch_b, w_out, w_ffn_gate, w_ffn_up,
           w_ffn_down):
    depth = w_in.shape[0]
    b, s, d = x.shape

    mod = _ada_call(c, ada_w, ada_b).reshape(depth, b, 6, d)
    cos_t, sin_t = _rope_call(positions)
    bias_tabs = _bias_call(rel_bias)
    ones_mla = _block_ones(HEAD_PAD)
    ones_dil = _block_ones(DIL_HEAD_DIM)
    half = MLA_ROPE // 2

    for l in range(depth):
        wl = w_in[l]
        k_rope_cols = jnp.pad(wl[:, _C_KR:_C_QKVB], ((0, 0), (MLA_NOPE, LANES - MLA_QK)))
        w_pre = jnp.concatenate([wl[:, _C_QC:_C_KR], k_rope_cols, wl[:, _C_QKVB:_C_GATE]], axis=1).astype(BF16)
        w_gates = wl[:, _C_GATE:].astype(BF16)
        wq3 = w_q_b[l].reshape(MLA_Q_LORA, MLA_HEADS, MLA_QK)
        lo, hi = wq3[..., MLA_NOPE:MLA_NOPE + half], wq3[..., MLA_NOPE + half:]
        wq_sw = jnp.concatenate([jnp.zeros_like(wq3[..., :MLA_NOPE]), hi, lo], axis=-1)
        wqq = jnp.concatenate([_pad_heads(w_q_b[l], MLA_QK),
                               _pad_heads(wq_sw.reshape(MLA_Q_LORA, -1), MLA_QK)], axis=1).astype(BF16)
        wkv3 = w_kv_b[l].reshape(MLA_KV_LORA, MLA_HEADS, MLA_NOPE + MLA_V)
        wkv = jnp.concatenate([_pad_heads(wkv3[:, :, :MLA_NOPE].reshape(MLA_KV_LORA, -1), MLA_NOPE),
                               _pad_heads(wkv3[:, :, MLA_NOPE:].reshape(MLA_KV_LORA, -1), MLA_V)], axis=1).astype(BF16)
        gq = q_norm_a[l] * (MLA_QK ** -0.5 * LOG2E)
        gk = k_norm_a[l]
        gains = jnp.stack([
            jnp.tile(_pad_lanes(gq), 2), jnp.tile(_pad_lanes(_swap_rope(gq)), 2),
            jnp.tile(_pad_lanes(gk), 2), jnp.tile(_pad_lanes(_swap_rope(gk)), 2),
            jnp.tile(q_norm_b[l] * (DIL_HEAD_DIM ** -0.5 * LOG2E), 4), jnp.tile(k_norm_b[l], 4),
            jnp.zeros((2 * LANES,), F32), jnp.zeros((2 * LANES,), F32)])
        wa = jnp.pad(w_branch_a[l].reshape(MLA_HEADS, MLA_V, d), ((0, 0), (0, HEAD_PAD - MLA_V), (0, 0)))
        wa = wa.reshape(MLA_HEADS * HEAD_PAD, d).astype(BF16)
        g1 = norm1_g[l].reshape(1, d)
        g2 = norm2_g[l].reshape(1, d)

        q, k, v, qd, kd, vd = _pre_call(x, mod[l], g1, cos_t, sin_t, w_pre, q_a_norm[l].reshape(1, -1), wqq,
                                        kv_a_norm[l].reshape(1, -1), wkv, gains, ones_mla, ones_dil, tm=512)
        ya = _mla_call(q, k, v, tq=512, tk=512)
        yb = _dil_call(qd, kd, vd, bias_tabs)
        x = _post_call(x, ya, yb, mod[l], g1, w_gates, wa, w_branch_b[l].astype(BF16), w_out[l].astype(BF16), tm=512)
        x = _ffn_call(x, mod[l], g2, w_ffn_gate[l].astype(BF16), w_ffn_up[l].astype(BF16),
                      w_ffn_down[l].astype(BF16), tm=512)
    return x
```

```python
import functools
import math

import jax
import jax.numpy as jnp
import numpy as np
from jax import lax
from jax.experimental import pallas as pl
from jax.experimental.pallas import tpu as pltpu

F32 = jnp.float32
BF16 = jnp.bfloat16

LANES = 128
VMEM_LIMIT_BYTES = 56 * 1024 * 1024

D_MODEL = 1024
MLA_HEADS = 8
MLA_Q_LORA = 256
MLA_KV_LORA = 128
MLA_NOPE = 64
MLA_ROPE = 32
MLA_V = 64
MLA_QK = MLA_NOPE + MLA_ROPE
ROPE_THETA = 10000.0
DIL_HEADS = 8
DIL_HEAD_DIM = 64
DIL_PATTERNS = ((128, 1), (512, 4), (2048, 16))
DIL_HALF = 64
REL_BUCKETS = 32
REL_MAX_DIST = 1024
EPS = 1e-6
NEG_INF = -1e30
LOG2E = math.log2(math.e)
MLA_SAFE_BOUND = 60.0
MLA_SUB = 512

_C_QC = 0
_C_KVC = _C_QC + MLA_Q_LORA
_C_KR = _C_KVC + MLA_KV_LORA
_C_QKVB = _C_KR + MLA_ROPE
_C_GATE = _C_QKVB + 3 * DIL_HEADS * DIL_HEAD_DIM
PRE_COLS = MLA_Q_LORA + MLA_KV_LORA + LANES + 3 * DIL_HEADS * DIL_HEAD_DIM
HEAD_PAD = LANES

DIL_BQ = 2 * DIL_HALF
DIL_BK = 4 * DIL_HALF
DIL_UNROLL = 8


def _cparams(sem):
    return pltpu.CompilerParams(dimension_semantics=sem, vmem_limit_bytes=VMEM_LIMIT_BYTES)


def _resident(shape, index_map):
    return pl.BlockSpec(shape, index_map, pipeline_mode=pl.Buffered(1))


def _ada_kernel(c_ref, w_ref, b_ref, o_ref):
    c = c_ref[...]
    act = (c * jax.nn.sigmoid(c)).astype(BF16)
    o_ref[0] = jnp.dot(act, w_ref[0].astype(BF16), preferred_element_type=F32) + b_ref[0]


def _ada_call(c, ada_w, ada_b):
    depth, d, six_d = ada_w.shape
    b = c.shape[0]
    n_chunk = six_d // d
    return pl.pallas_call(
        _ada_kernel,
        grid=(depth, n_chunk),
        in_specs=[
            pl.BlockSpec((b, d), lambda l, j: (0, 0)),
            pl.BlockSpec((1, d, d), lambda l, j: (l, 0, j)),
            pl.BlockSpec((1, 1, d), lambda l, j: (l, 0, j)),
        ],
        out_specs=pl.BlockSpec((1, b, d), lambda l, j: (l, 0, j)),
        out_shape=jax.ShapeDtypeStruct((depth, b, six_d), F32),
        compiler_params=_cparams(("arbitrary", "arbitrary")),
        name="ada_mod",
    )(c, ada_w, ada_b.reshape(depth, 1, six_d))


def _rope_kernel(pos_ref, inv_ref, c_ref, s_ref):
    ang = pos_ref[0].astype(F32) * inv_ref[...]
    cosv = jnp.cos(ang)
    sinv = jnp.sin(ang)
    lane = lax.broadcasted_iota(jnp.int32, ang.shape, 1)
    half = MLA_ROPE // 2
    c_ref[0] = jnp.where(lane < MLA_NOPE, 1.0, jnp.where(lane < MLA_QK, cosv, 0.0))
    s_ref[0] = jnp.where(lane < MLA_NOPE, 0.0,
                         jnp.where(lane < MLA_NOPE + half, -sinv, jnp.where(lane < MLA_QK, sinv, 0.0)))


def _rope_call(positions):
    b, s = positions.shape
    half = MLA_ROPE // 2
    inv = ROPE_THETA ** (-jnp.arange(half, dtype=F32) / half)
    inv_lane = jnp.zeros((1, LANES), F32)
    inv_lane = inv_lane.at[0, MLA_NOPE:MLA_NOPE + half].set(inv).at[0, MLA_NOPE + half:MLA_QK].set(inv)
    ts = 512
    spec = pl.BlockSpec((1, ts, LANES), lambda i, j: (i, j, 0))
    return pl.pallas_call(
        _rope_kernel,
        grid=(b, s // ts),
        in_specs=[pl.BlockSpec((1, ts, 1), lambda i, j: (i, j, 0)),
                  pl.BlockSpec((1, LANES), lambda i, j: (0, 0))],
        out_specs=[spec, spec],
        out_shape=[jax.ShapeDtypeStruct((b, s, LANES), F32)] * 2,
        compiler_params=_cparams(("arbitrary", "arbitrary")),
        name="rope_tables",
    )(positions.reshape(b, s, 1), inv_lane)


def _t5_bucket(rel):
    nb = REL_BUCKETS // 2
    max_exact = nb // 2
    ret = jnp.where(rel > 0, nb, 0)
    n = jnp.abs(rel)
    nf = jnp.maximum(n, 1).astype(F32)
    large = max_exact + (jnp.log(nf / max_exact) / math.log(REL_MAX_DIST / max_exact) * (nb - max_exact)).astype(jnp.int32)
    large = jnp.minimum(large, nb - 1)
    return ret + jnp.where(n < max_exact, n, large)


def _bias_kernel(bucket_ref, rb_ref, o_ref):
    h = pl.program_id(1)
    bucket = bucket_ref[0]
    acc = jnp.full(bucket.shape, NEG_INF, F32)
    for bkt in range(REL_BUCKETS):
        acc = jnp.where(bucket == bkt, rb_ref[bkt, h], acc)
    o_ref[0, 0] = acc * LOG2E


def _bias_call(rel_bias):
    qi = jnp.arange(DIL_BQ, dtype=jnp.int32)[:, None]
    kj = jnp.arange(DIL_BK, dtype=jnp.int32)[None, :]
    tabs = []
    for _, dil in DIL_PATTERNS:
        for variant in range(3):
            rel = kj - variant * DIL_HALF - qi
            tabs.append(jnp.where(jnp.abs(rel) <= DIL_HALF, _t5_bucket(rel * dil), -1))
    bucket = jnp.stack(tabs).astype(jnp.int32)
    n_tab = bucket.shape[0]
    return pl.pallas_call(
        _bias_kernel,
        grid=(n_tab, DIL_HEADS),
        in_specs=[pl.BlockSpec((1, DIL_BQ, DIL_BK), lambda t, h: (t, 0, 0)),
                  pl.BlockSpec(memory_space=pltpu.SMEM)],
        out_specs=pl.BlockSpec((1, 1, DIL_BQ, DIL_BK), lambda t, h: (t, h, 0, 0)),
        out_shape=jax.ShapeDtypeStruct((n_tab, DIL_HEADS, DIL_BQ, DIL_BK), F32),
        compiler_params=_cparams(("arbitrary", "arbitrary")),
        name="bias_tables",
    )(bucket, rel_bias)


def _modulated_norm(x, gain, shift, scale):
    r = lax.rsqrt(jnp.mean(x * x, axis=-1, keepdims=True) + EPS)
    return ((x * r) * gain) * (1.0 + scale) + shift


PRE_SUB = 256


def _group_sumsq(t, ones_ref):
    return jnp.dot((t * t).astype(BF16), ones_ref[...], preferred_element_type=F32)


def _pre_kernel(x_ref, mod_ref, g1_ref, cos_ref, sin_ref, w_ref, qan_ref, wqq_ref, kvan_ref, wkv_ref,
                gains_ref, ones_mla_ref, ones_dil_ref,
                q_out, k_out, v_out, qd_out, kd_out, vd_out):
    tm = x_ref.shape[1]
    hp = MLA_HEADS * HEAD_PAD
    n_dil = DIL_HEADS * DIL_HEAD_DIM
    grp = 2 * LANES
    c0 = MLA_Q_LORA
    c1 = c0 + MLA_KV_LORA
    c2 = c1 + LANES
    half = MLA_ROPE // 2
    inv_qk = 1.0 / MLA_QK
    inv_dh = 1.0 / DIL_HEAD_DIM
    lane = lax.broadcasted_iota(jnp.int32, (PRE_SUB, LANES), 1)
    low_half = lane < MLA_NOPE + half

    for sub in range(tm // PRE_SUB):
        rs = slice(sub * PRE_SUB, (sub + 1) * PRE_SUB)
        x = x_ref[0, rs, :]
        h = _modulated_norm(x, g1_ref[...], mod_ref[0, 0:1, :], mod_ref[0, 1:2, :])
        proj = jnp.dot(h.astype(BF16), w_ref[...], preferred_element_type=F32)

        qc = proj[:, :c0]
        qcn = (qc * lax.rsqrt(jnp.mean(qc * qc, axis=-1, keepdims=True) + EPS)) * qan_ref[...]
        qq = jnp.dot(qcn.astype(BF16), wqq_ref[...], preferred_element_type=F32)
        kvc = proj[:, c0:c1]
        kvn = (kvc * lax.rsqrt(jnp.mean(kvc * kvc, axis=-1, keepdims=True) + EPS)) * kvan_ref[...]
        kv = jnp.dot(kvn.astype(BF16), wkv_ref[...], preferred_element_type=F32)
        k_rope = proj[:, c1:c2]
        k_rope_sw = jnp.where(low_half, pltpu.roll(k_rope, LANES - half, 1), pltpu.roll(k_rope, half, 1))

        cos2 = jnp.concatenate([cos_ref[0, rs, :]] * 2, axis=1)
        sin2 = jnp.concatenate([sin_ref[0, rs, :]] * 2, axis=1)
        aq = gains_ref[0:1, :] * cos2
        bq = gains_ref[1:2, :] * sin2
        ak = gains_ref[2:3, :] * cos2
        bk = gains_ref[3:4, :] * sin2
        kr2 = jnp.concatenate([k_rope] * 2, axis=1)
        krs2 = jnp.concatenate([k_rope_sw] * 2, axis=1)
        for g in range(hp // grp):
            sl = slice(g * grp, (g + 1) * grp)
            qh = qq[:, sl]
            rq = lax.rsqrt(_group_sumsq(qh, ones_mla_ref) * inv_qk + EPS)
            q_out[0, rs, sl] = (rq * (qh * aq + qq[:, hp + g * grp:hp + (g + 1) * grp] * bq)
                                + gains_ref[6:7, :]).astype(BF16)
            kh = kv[:, sl] + kr2
            rk = lax.rsqrt(_group_sumsq(kh, ones_mla_ref) * inv_qk + EPS)
            k_out[0, rs, sl] = (rk * (kh * ak + krs2 * bk) + gains_ref[7:8, :]).astype(BF16)
            v_out[0, rs, sl] = (kv[:, hp + g * grp:hp + (g + 1) * grp] + gains_ref[8:9, :]).astype(BF16)

        for g in range(n_dil // grp):
            sl = slice(g * grp, (g + 1) * grp)
            qd = proj[:, c2 + g * grp:c2 + (g + 1) * grp]
            rqd = lax.rsqrt(_group_sumsq(qd, ones_dil_ref) * inv_dh + EPS)
            qd_out[0, rs, sl] = ((qd * rqd) * gains_ref[4:5, :]).astype(BF16)
            kd = proj[:, c2 + n_dil + g * grp:c2 + n_dil + (g + 1) * grp]
            rkd = lax.rsqrt(_group_sumsq(kd, ones_dil_ref) * inv_dh + EPS)
            kd_out[0, rs, sl] = ((kd * rkd) * gains_ref[5:6, :]).astype(BF16)
        vd_out[0, rs, :] = proj[:, c2 + 2 * n_dil:c2 + 3 * n_dil].astype(BF16)


def _pre_call(x, mod_l, g1, cos_t, sin_t, w_pre, qan, wqq, kvan, wkv, gains, ones_mla, ones_dil, tm):
    b, s, d = x.shape
    hp = MLA_HEADS * HEAD_PAD
    n_dil = DIL_HEADS * DIL_HEAD_DIM
    tok = lambda i, j: (i, j, 0)
    const2 = lambda i, j: (0, 0)
    row = lambda n: _resident((1, n), const2)
    return pl.pallas_call(
        _pre_kernel,
        grid=(b, s // tm),
        in_specs=[
            pl.BlockSpec((1, tm, d), tok),
            pl.BlockSpec((1, 6, d), lambda i, j: (i, 0, 0)),
            row(d),
            pl.BlockSpec((1, tm, LANES), tok),
            pl.BlockSpec((1, tm, LANES), tok),
            _resident((d, PRE_COLS), const2),
            row(MLA_Q_LORA),
            _resident((MLA_Q_LORA, 2 * hp), const2),
            row(MLA_KV_LORA),
            _resident((MLA_KV_LORA, 2 * hp), const2),
            _resident(gains.shape, const2),
            _resident(ones_mla.shape, const2),
            _resident(ones_dil.shape, const2),
        ],
        out_specs=[
            pl.BlockSpec((1, tm, hp), tok), pl.BlockSpec((1, tm, hp), tok), pl.BlockSpec((1, tm, hp), tok),
            pl.BlockSpec((1, tm, n_dil), tok), pl.BlockSpec((1, tm, n_dil), tok), pl.BlockSpec((1, tm, n_dil), tok),
        ],
        out_shape=[jax.ShapeDtypeStruct((b, s, hp), BF16)] * 3 + [jax.ShapeDtypeStruct((b, s, n_dil), BF16)] * 3,
        compiler_params=_cparams(("arbitrary", "arbitrary")),
        name="pre_proj",
    )(x, mod_l, g1, cos_t, sin_t, w_pre, qan, wqq, kvan, wkv, gains, ones_mla, ones_dil)


def _mla_online_kernel(q_ref, k_ref, v_ref, o_ref, *, tk):
    q = q_ref[0]
    tq = q.shape[0]
    n_k = k_ref.shape[1] // tk

    def body(j, carry):
        m, acc = carry
        ks = pl.multiple_of(j * tk, tk)
        k = k_ref[0, pl.ds(ks, tk), :]
        v = v_ref[0, pl.ds(ks, tk), :]
        s = lax.dot_general(q, k, (((1,), (1,)), ((), ())), preferred_element_type=F32)
        m_new = jnp.maximum(m, jnp.max(s, axis=-1, keepdims=True))
        p = jnp.exp2(s - m_new)
        acc = jnp.exp2(m - m_new) * acc + jnp.dot(p.astype(BF16), v, preferred_element_type=F32)
        return m_new, acc

    init = (jnp.full((tq, 1), -jnp.inf, F32), jnp.zeros((tq, HEAD_PAD), F32))
    _, acc = lax.fori_loop(0, n_k, body, init, unroll=True)
    o_ref[0] = (acc / acc[:, MLA_V:MLA_V + 1]).astype(BF16)


def _mla_bounded_kernel(q_ref, k_ref, v_ref, o_ref, *, tk):
    n_k = k_ref.shape[1] // tk
    for qs in range(q_ref.shape[1] // MLA_SUB):
        rows = slice(qs * MLA_SUB, (qs + 1) * MLA_SUB)
        q = q_ref[0, rows, :]
        acc = jnp.zeros((MLA_SUB, HEAD_PAD), F32)
        for j in range(n_k):
            k = k_ref[0, j * tk:(j + 1) * tk, :]
            v = v_ref[0, j * tk:(j + 1) * tk, :]
            s = lax.dot_general(q, k, (((1,), (1,)), ((), ())), preferred_element_type=F32)
            acc = acc + jnp.dot(jnp.exp2(s).astype(BF16), v, preferred_element_type=F32)
        o_ref[0, rows, :] = (acc / acc[:, MLA_V:MLA_V + 1]).astype(BF16)


def _mla_call(q, k, v, tq, tk, bounded):
    b, s, _ = q.shape
    return pl.pallas_call(
        functools.partial(_mla_bounded_kernel if bounded else _mla_online_kernel, tk=tk),
        grid=(b, MLA_HEADS, s // tq),
        in_specs=[
            pl.BlockSpec((1, tq, HEAD_PAD), lambda i, h, j: (i, j, h)),
            pl.BlockSpec((1, s, HEAD_PAD), lambda i, h, j: (i, 0, h)),
            pl.BlockSpec((1, s, HEAD_PAD), lambda i, h, j: (i, 0, h)),
        ],
        out_specs=pl.BlockSpec((1, tq, HEAD_PAD), lambda i, h, j: (i, j, h)),
        out_shape=jax.ShapeDtypeStruct(q.shape, BF16),
        compiler_params=_cparams(("arbitrary", "arbitrary", "arbitrary")),
        name="mla_attn_bounded" if bounded else "mla_attn_online",
    )(q, k, v)


def _dil_kernel(q_ref, k_ref, v_ref, bias_ref, o_ref, qf, kf, vf, m_s, l_s, n_s):
    s_len = q_ref.shape[1]
    qf[...] = q_ref[0].astype(F32)
    kf[...] = k_ref[0].astype(F32)
    vf[...] = v_ref[0].astype(F32)
    lane = lax.broadcasted_iota(jnp.int32, (DIL_BQ, LANES), 1)
    first = lane < DIL_HEAD_DIM

    def band_tile(q_t, k_t, v_t, tab):
        parts = []
        for hh in range(2):
            sel = first if hh == 0 else jnp.logical_not(first)
            qm = jnp.where(sel, q_t, jnp.zeros_like(q_t))
            s = lax.dot_general(qm, k_t, (((1,), (1,)), ((), ())), preferred_element_type=F32)
            s = s + bias_ref[tab, hh]
            m = jnp.max(s, axis=-1, keepdims=True)
            p = jnp.exp2(s - m)
            l = jnp.sum(p, axis=-1, keepdims=True)
            pv = jnp.dot(p.astype(BF16), v_t, preferred_element_type=F32)
            parts.append((m, l, pv))
        (m0, l0, n0), (m1, l1, n1) = parts
        return jnp.where(first, m0, m1), jnp.where(first, l0, l1), jnp.where(first, n0, n1)

    def variant_of(n, n_blk):
        return jnp.where(n == 0, 0, jnp.where(n == n_blk - 1, 2, 1))

    n_blk0 = s_len // DIL_BQ

    def body0(n, carry):
        i0 = pl.multiple_of(n * DIL_BQ, DIL_BQ)
        ks = pl.multiple_of(jnp.clip(i0 - DIL_HALF, 0, s_len - DIL_BK), DIL_HALF)
        m, l, num = band_tile(q_ref[0, pl.ds(i0, DIL_BQ), :], k_ref[0, pl.ds(ks, DIL_BK), :],
                              v_ref[0, pl.ds(ks, DIL_BK), :], variant_of(n, n_blk0))
        m_s[pl.ds(i0, DIL_BQ), :] = m
        l_s[pl.ds(i0, DIL_BQ), :] = l
        n_s[pl.ds(i0, DIL_BQ), :] = num
        return carry

    lax.fori_loop(0, n_blk0, body0, 0, unroll=DIL_UNROLL)

    for p_idx in range(1, len(DIL_PATTERNS)):
        dil = DIL_PATTERNS[p_idx][1]
        sub_len = s_len // dil
        n_blk = sub_len // DIL_BQ

        def body(t, carry, dil=dil, sub_len=sub_len, n_blk=n_blk, p_idx=p_idx):
            r = t // n_blk
            n = t - r * n_blk
            i0 = n * DIL_BQ
            ks = jnp.clip(i0 - DIL_HALF, 0, sub_len - DIL_BK)
            rows_q = pl.ds(r + dil * i0, DIL_BQ, stride=dil)
            rows_k = pl.ds(r + dil * ks, DIL_BK, stride=dil)
            m_t, l_t, n_t = band_tile(qf[rows_q, :].astype(BF16), kf[rows_k, :].astype(BF16),
                                      vf[rows_k, :].astype(BF16), 3 * p_idx + variant_of(n, n_blk))
            m_o = m_s[rows_q, :]
            m_n = jnp.maximum(m_o, m_t)
            a = jnp.exp2(m_o - m_n)
            c = jnp.exp2(m_t - m_n)
            m_s[rows_q, :] = m_n
            l_s[rows_q, :] = a * l_s[rows_q, :] + c * l_t
            n_s[rows_q, :] = a * n_s[rows_q, :] + c * n_t
            return carry

        lax.fori_loop(0, dil * n_blk, body, 0, unroll=DIL_UNROLL)

    o_ref[0] = (n_s[...] / l_s[...]).astype(BF16)


def _dil_call(qd, kd, vd, bias_tabs):
    b, s, n_dil = qd.shape
    n_pair = n_dil // LANES
    n_tab = bias_tabs.shape[0]
    blk = pl.BlockSpec((1, s, LANES), lambda hp, i: (i, 0, hp))
    return pl.pallas_call(
        _dil_kernel,
        grid=(n_pair, b),
        in_specs=[blk, blk, blk,
                  pl.BlockSpec((n_tab, 2, DIL_BQ, DIL_BK), lambda hp, i: (0, hp, 0, 0))],
        out_specs=blk,
        out_shape=jax.ShapeDtypeStruct(qd.shape, BF16),
        scratch_shapes=[pltpu.VMEM((s, LANES), F32)] * 6,
        compiler_params=_cparams(("arbitrary", "arbitrary")),
        name="dil_attn",
    )(qd, kd, vd, bias_tabs)


def _post_kernel(x_ref, ya_ref, yb_ref, mod_ref, g1_ref, wg_ref, wa_ref, wb_ref, wo_ref, o_ref):
    x = x_ref[0]
    h = _modulated_norm(x, g1_ref[...], mod_ref[0, 0:1, :], mod_ref[0, 1:2, :])
    gates = jnp.dot(h.astype(BF16), wg_ref[...], preferred_element_type=F32)
    d = x.shape[-1]
    a = jnp.dot(ya_ref[0], wa_ref[...], preferred_element_type=F32)
    bb = jnp.dot(yb_ref[0], wb_ref[...], preferred_element_type=F32)
    merged = jax.nn.sigmoid(gates[:, :d]) * a + jax.nn.sigmoid(gates[:, d:]) * bb
    upd = jnp.dot(merged.astype(BF16), wo_ref[...], preferred_element_type=F32)
    o_ref[0] = x + mod_ref[0, 2:3, :] * upd


def _post_call(x, ya, yb, mod_l, g1, wg, wa, wb, wo, tm):
    b, s, d = x.shape
    tok = lambda i, j: (i, j, 0)
    const2 = lambda i, j: (0, 0)
    return pl.pallas_call(
        _post_kernel,
        grid=(b, s // tm),
        in_specs=[
            pl.BlockSpec((1, tm, d), tok),
            pl.BlockSpec((1, tm, ya.shape[-1]), tok),
            pl.BlockSpec((1, tm, yb.shape[-1]), tok),
            pl.BlockSpec((1, 6, d), lambda i, j: (i, 0, 0)),
            _resident((1, d), const2),
            _resident(wg.shape, const2),
            _resident(wa.shape, const2),
            _resident(wb.shape, const2),
            _resident(wo.shape, const2),
        ],
        out_specs=pl.BlockSpec((1, tm, d), tok),
        out_shape=jax.ShapeDtypeStruct(x.shape, F32),
        compiler_params=_cparams(("arbitrary", "arbitrary")),
        name="attn_out",
    )(x, ya, yb, mod_l, g1, wg, wa, wb, wo)


def _ffn_kernel(x_ref, mod_ref, g2_ref, wg_ref, wu_ref, wd_ref, o_ref):
    x = x_ref[0]
    h = _modulated_norm(x, g2_ref[...], mod_ref[0, 3:4, :], mod_ref[0, 4:5, :]).astype(BF16)
    g = jnp.dot(h, wg_ref[...], preferred_element_type=F32)
    u = jnp.dot(h, wu_ref[...], preferred_element_type=F32)
    act = (g * jax.nn.sigmoid(g)) * u
    upd = jnp.dot(act.astype(BF16), wd_ref[...], preferred_element_type=F32)
    o_ref[0] = x + mod_ref[0, 5:6, :] * upd


def _ffn_call(x, mod_l, g2, wg, wu, wd, tm):
    b, s, d = x.shape
    tok = lambda i, j: (i, j, 0)
    const2 = lambda i, j: (0, 0)
    return pl.pallas_call(
        _ffn_kernel,
        grid=(b, s // tm),
        in_specs=[
            pl.BlockSpec((1, tm, d), tok),
            pl.BlockSpec((1, 6, d), lambda i, j: (i, 0, 0)),
            _resident((1, d), const2),
            _resident(wg.shape, const2),
            _resident(wu.shape, const2),
            _resident(wd.shape, const2),
        ],
        out_specs=pl.BlockSpec((1, tm, d), tok),
        out_shape=jax.ShapeDtypeStruct(x.shape, F32),
        compiler_params=_cparams(("arbitrary", "arbitrary")),
        name="swiglu",
    )(x, mod_l, g2, wg, wu, wd)


def _pad_heads(w, width):
    rows = w.shape[0]
    w = w.reshape(rows, MLA_HEADS, width)
    return jnp.pad(w, ((0, 0), (0, 0), (0, HEAD_PAD - width))).reshape(rows, MLA_HEADS * HEAD_PAD)


def _pad_lanes(g):
    return jnp.pad(g, (0, LANES - g.shape[0]))


def _swap_rope(g):
    half = MLA_ROPE // 2
    return jnp.concatenate([jnp.zeros((MLA_NOPE,), g.dtype), g[MLA_NOPE + half:], g[MLA_NOPE:MLA_NOPE + half]])


def _block_ones(block, size=2 * LANES):
    idx = np.arange(size) // block
    return jnp.asarray(idx[:, None] == idx[None, :], BF16)


def kernel(x, c, positions, rel_bias, norm1_g, norm2_g, ada_w, ada_b, w_in, q_a_norm, w_q_b, kv_a_norm, w_kv_b,
           q_norm_a, k_norm_a, q_norm_b, k_norm_b, w_branch_a, w_branch_b, w_out, w_ffn_gate, w_ffn_up,
           w_ffn_down):
    depth = w_in.shape[0]
    b, s, d = x.shape

    mod = _ada_call(c, ada_w, ada_b).reshape(depth, b, 6, d)
    cos_t, sin_t = _rope_call(positions)
    bias_tabs = _bias_call(rel_bias)
    ones_mla = _block_ones(HEAD_PAD)
    ones_dil = _block_ones(DIL_HEAD_DIM)
    half = MLA_ROPE // 2
    lane2 = np.arange(2 * LANES) % LANES
    spare_qk = jnp.asarray(lane2 == MLA_QK, F32)
    spare_v = jnp.asarray(lane2 == MLA_V, F32)

    for l in range(depth):
        wl = w_in[l]
        k_rope_cols = jnp.pad(wl[:, _C_KR:_C_QKVB], ((0, 0), (MLA_NOPE, LANES - MLA_QK)))
        w_pre = jnp.concatenate([wl[:, _C_QC:_C_KR], k_rope_cols, wl[:, _C_QKVB:_C_GATE]], axis=1).astype(BF16)
        w_gates = wl[:, _C_GATE:].astype(BF16)
        wq3 = w_q_b[l].reshape(MLA_Q_LORA, MLA_HEADS, MLA_QK)
        lo, hi = wq3[..., MLA_NOPE:MLA_NOPE + half], wq3[..., MLA_NOPE + half:]
        wq_sw = jnp.concatenate([jnp.zeros_like(wq3[..., :MLA_NOPE]), hi, lo], axis=-1)
        wqq = jnp.concatenate([_pad_heads(w_q_b[l], MLA_QK),
                               _pad_heads(wq_sw.reshape(MLA_Q_LORA, -1), MLA_QK)], axis=1).astype(BF16)
        wkv3 = w_kv_b[l].reshape(MLA_KV_LORA, MLA_HEADS, MLA_NOPE + MLA_V)
        wkv = jnp.concatenate([_pad_heads(wkv3[:, :, :MLA_NOPE].reshape(MLA_KV_LORA, -1), MLA_NOPE),
                               _pad_heads(wkv3[:, :, MLA_NOPE:].reshape(MLA_KV_LORA, -1), MLA_V)], axis=1).astype(BF16)
        gq = q_norm_a[l] * (MLA_QK ** -0.5 * LOG2E)
        gk = k_norm_a[l]
        score_bound = MLA_QK * jnp.max(jnp.abs(gq)) * jnp.max(jnp.abs(gk))
        gains = jnp.stack([
            jnp.tile(_pad_lanes(gq), 2), jnp.tile(_pad_lanes(_swap_rope(gq)), 2),
            jnp.tile(_pad_lanes(gk), 2), jnp.tile(_pad_lanes(_swap_rope(gk)), 2),
            jnp.tile(q_norm_b[l] * (DIL_HEAD_DIM ** -0.5 * LOG2E), 4), jnp.tile(k_norm_b[l], 4),
            -score_bound * spare_qk, spare_qk, spare_v] + [jnp.zeros((2 * LANES,), F32)] * 7)
        wa = jnp.pad(w_branch_a[l].reshape(MLA_HEADS, MLA_V, d), ((0, 0), (0, HEAD_PAD - MLA_V), (0, 0)))
        wa = wa.reshape(MLA_HEADS * HEAD_PAD, d).astype(BF16)
        g1 = norm1_g[l].reshape(1, d)
        g2 = norm2_g[l].reshape(1, d)

        q, k, v, qd, kd, vd = _pre_call(x, mod[l], g1, cos_t, sin_t, w_pre, q_a_norm[l].reshape(1, -1), wqq,
                                        kv_a_norm[l].reshape(1, -1), wkv, gains, ones_mla, ones_dil, tm=512)
        ya = lax.cond(score_bound <= MLA_SAFE_BOUND,
                      functools.partial(_mla_call, tq=4096, tk=256, bounded=True),
                      functools.partial(_mla_call, tq=512, tk=1024, bounded=False), q, k, v)
        yb = _dil_call(qd, kd, vd, bias_tabs)
        x = _post_call(x, ya, yb, mod[l], g1, w_gates, wa, w_branch_b[l].astype(BF16), w_out[l].astype(BF16), tm=512)
        x = _ffn_call(x, mod[l], g2, w_ffn_gate[l].astype(BF16), w_ffn_up[l].astype(BF16),
                      w_ffn_down[l].astype(BF16), tm=512)
    return x
```

```python
import functools
import math

import jax
import jax.numpy as jnp
import numpy as np
from jax import lax
from jax.experimental import pallas as pl
from jax.experimental.pallas import tpu as pltpu

F32 = jnp.float32
BF16 = jnp.bfloat16

LANES = 128
VMEM_LIMIT_BYTES = 56 * 1024 * 1024

D_MODEL = 1024
MLA_HEADS = 8
MLA_Q_LORA = 256
MLA_KV_LORA = 128
MLA_NOPE = 64
MLA_ROPE = 32
MLA_V = 64
MLA_QK = MLA_NOPE + MLA_ROPE
ROPE_THETA = 10000.0
DIL_HEADS = 8
DIL_HEAD_DIM = 64
DIL_PATTERNS = ((128, 1), (512, 4), (2048, 16))
DIL_HALF = 64
REL_BUCKETS = 32
REL_MAX_DIST = 1024
EPS = 1e-6
NEG_INF = -1e30
LOG2E = math.log2(math.e)
MLA_SAFE_BOUND = 60.0
MLA_SUB = 512

_C_QC = 0
_C_KVC = _C_QC + MLA_Q_LORA
_C_KR = _C_KVC + MLA_KV_LORA
_C_QKVB = _C_KR + MLA_ROPE
_C_GATE = _C_QKVB + 3 * DIL_HEADS * DIL_HEAD_DIM
PRE_COLS = MLA_Q_LORA + MLA_KV_LORA + LANES + 3 * DIL_HEADS * DIL_HEAD_DIM
HEAD_PAD = LANES

DIL_BQ = 2 * DIL_HALF
DIL_BK = 4 * DIL_HALF
DIL_UNROLL = 8
DIL_GROUP = 3 * LANES
DIL_SAFE_RANGE = 120.0


def _cparams(sem):
    return pltpu.CompilerParams(dimension_semantics=sem, vmem_limit_bytes=VMEM_LIMIT_BYTES)


def _resident(shape, index_map):
    return pl.BlockSpec(shape, index_map, pipeline_mode=pl.Buffered(1))


def _ada_kernel(c_ref, w_ref, b_ref, o_ref):
    c = c_ref[...]
    act = (c * jax.nn.sigmoid(c)).astype(BF16)
    o_ref[0] = jnp.dot(act, w_ref[0].astype(BF16), preferred_element_type=F32) + b_ref[0]


def _ada_call(c, ada_w, ada_b):
    depth, d, six_d = ada_w.shape
    b = c.shape[0]
    n_chunk = six_d // d
    return pl.pallas_call(
        _ada_kernel,
        grid=(depth, n_chunk),
        in_specs=[
            pl.BlockSpec((b, d), lambda l, j: (0, 0)),
            pl.BlockSpec((1, d, d), lambda l, j: (l, 0, j)),
            pl.BlockSpec((1, 1, d), lambda l, j: (l, 0, j)),
        ],
        out_specs=pl.BlockSpec((1, b, d), lambda l, j: (l, 0, j)),
        out_shape=jax.ShapeDtypeStruct((depth, b, six_d), F32),
        compiler_params=_cparams(("arbitrary", "arbitrary")),
        name="ada_mod",
    )(c, ada_w, ada_b.reshape(depth, 1, six_d))


def _rope_kernel(pos_ref, inv_ref, c_ref, s_ref):
    ang = pos_ref[0].astype(F32) * inv_ref[...]
    cosv = jnp.cos(ang)
    sinv = jnp.sin(ang)
    lane = lax.broadcasted_iota(jnp.int32, ang.shape, 1)
    half = MLA_ROPE // 2
    c_ref[0] = jnp.where(lane < MLA_NOPE, 1.0, jnp.where(lane < MLA_QK, cosv, 0.0))
    s_ref[0] = jnp.where(lane < MLA_NOPE, 0.0,
                         jnp.where(lane < MLA_NOPE + half, -sinv, jnp.where(lane < MLA_QK, sinv, 0.0)))


def _rope_call(positions):
    b, s = positions.shape
    half = MLA_ROPE // 2
    inv = ROPE_THETA ** (-jnp.arange(half, dtype=F32) / half)
    inv_lane = jnp.zeros((1, LANES), F32)
    inv_lane = inv_lane.at[0, MLA_NOPE:MLA_NOPE + half].set(inv).at[0, MLA_NOPE + half:MLA_QK].set(inv)
    ts = 512
    spec = pl.BlockSpec((1, ts, LANES), lambda i, j: (i, j, 0))
    return pl.pallas_call(
        _rope_kernel,
        grid=(b, s // ts),
        in_specs=[pl.BlockSpec((1, ts, 1), lambda i, j: (i, j, 0)),
                  pl.BlockSpec((1, LANES), lambda i, j: (0, 0))],
        out_specs=[spec, spec],
        out_shape=[jax.ShapeDtypeStruct((b, s, LANES), F32)] * 2,
        compiler_params=_cparams(("arbitrary", "arbitrary")),
        name="rope_tables",
    )(positions.reshape(b, s, 1), inv_lane)


def _t5_bucket(rel):
    nb = REL_BUCKETS // 2
    max_exact = nb // 2
    ret = jnp.where(rel > 0, nb, 0)
    n = jnp.abs(rel)
    nf = jnp.maximum(n, 1).astype(F32)
    large = max_exact + (jnp.log(nf / max_exact) / math.log(REL_MAX_DIST / max_exact) * (nb - max_exact)).astype(jnp.int32)
    large = jnp.minimum(large, nb - 1)
    return ret + jnp.where(n < max_exact, n, large)


def _bias_kernel(bucket_ref, rb_ref, shift_ref, o_ref):
    h = pl.program_id(1)
    bucket = bucket_ref[0]
    acc = jnp.full(bucket.shape, NEG_INF, F32)
    for bkt in range(REL_BUCKETS):
        acc = jnp.where(bucket == bkt, rb_ref[bkt, h], acc)
    o_ref[0, 0] = acc * LOG2E - shift_ref[0]


def _bias_call(rel_bias, shift):
    qi = jnp.arange(DIL_BQ, dtype=jnp.int32)[:, None]
    kj = jnp.arange(DIL_BK, dtype=jnp.int32)[None, :]
    tabs = []
    for _, dil in DIL_PATTERNS:
        for variant in range(3):
            rel = kj - variant * DIL_HALF - qi
            tabs.append(jnp.where(jnp.abs(rel) <= DIL_HALF, _t5_bucket(rel * dil), -1))
    bucket = jnp.stack(tabs).astype(jnp.int32)
    n_tab = bucket.shape[0]
    return pl.pallas_call(
        _bias_kernel,
        grid=(n_tab, DIL_HEADS),
        in_specs=[pl.BlockSpec((1, DIL_BQ, DIL_BK), lambda t, h: (t, 0, 0)),
                  pl.BlockSpec(memory_space=pltpu.SMEM), pl.BlockSpec(memory_space=pltpu.SMEM)],
        out_specs=pl.BlockSpec((1, 1, DIL_BQ, DIL_BK), lambda t, h: (t, h, 0, 0)),
        out_shape=jax.ShapeDtypeStruct((n_tab, DIL_HEADS, DIL_BQ, DIL_BK), F32),
        compiler_params=_cparams(("arbitrary", "arbitrary")),
        name="bias_tables",
    )(bucket, rel_bias, jnp.reshape(shift, (1,)).astype(F32))


def _modulated_norm(x, gain, shift, scale):
    r = lax.rsqrt(jnp.mean(x * x, axis=-1, keepdims=True) + EPS)
    return ((x * r) * gain) * (1.0 + scale) + shift


PRE_SUB = 256


def _group_sumsq(t, ones_ref):
    return jnp.dot((t * t).astype(BF16), ones_ref[...], preferred_element_type=F32)


def _pre_kernel(x_ref, mod_ref, g1_ref, cos_ref, sin_ref, w_ref, qan_ref, wqq_ref, kvan_ref, wkv_ref,
                gains_ref, ones_mla_ref, ones_dil_ref,
                q_out, k_out, v_out, qkvd_out):
    tm = x_ref.shape[1]
    hp = MLA_HEADS * HEAD_PAD
    n_dil = DIL_HEADS * DIL_HEAD_DIM
    grp = 2 * LANES
    c0 = MLA_Q_LORA
    c1 = c0 + MLA_KV_LORA
    c2 = c1 + LANES
    half = MLA_ROPE // 2
    inv_qk = 1.0 / MLA_QK
    inv_dh = 1.0 / DIL_HEAD_DIM
    lane = lax.broadcasted_iota(jnp.int32, (PRE_SUB, LANES), 1)
    low_half = lane < MLA_NOPE + half

    for sub in range(tm // PRE_SUB):
        rs = slice(sub * PRE_SUB, (sub + 1) * PRE_SUB)
        x = x_ref[0, rs, :]
        h = _modulated_norm(x, g1_ref[...], mod_ref[0, 0:1, :], mod_ref[0, 1:2, :])
        proj = jnp.dot(h.astype(BF16), w_ref[...], preferred_element_type=F32)

        qc = proj[:, :c0]
        qcn = (qc * lax.rsqrt(jnp.mean(qc * qc, axis=-1, keepdims=True) + EPS)) * qan_ref[...]
        qq = jnp.dot(qcn.astype(BF16), wqq_ref[...], preferred_element_type=F32)
        kvc = proj[:, c0:c1]
        kvn = (kvc * lax.rsqrt(jnp.mean(kvc * kvc, axis=-1, keepdims=True) + EPS)) * kvan_ref[...]
        kv = jnp.dot(kvn.astype(BF16), wkv_ref[...], preferred_element_type=F32)
        k_rope = proj[:, c1:c2]
        k_rope_sw = jnp.where(low_half, pltpu.roll(k_rope, LANES - half, 1), pltpu.roll(k_rope, half, 1))

        cos2 = jnp.concatenate([cos_ref[0, rs, :]] * 2, axis=1)
        sin2 = jnp.concatenate([sin_ref[0, rs, :]] * 2, axis=1)
        aq = gains_ref[0:1, :] * cos2
        bq = gains_ref[1:2, :] * sin2
        ak = gains_ref[2:3, :] * cos2
        bk = gains_ref[3:4, :] * sin2
        kr2 = jnp.concatenate([k_rope] * 2, axis=1)
        krs2 = jnp.concatenate([k_rope_sw] * 2, axis=1)
        for g in range(hp // grp):
            sl = slice(g * grp, (g + 1) * grp)
            qh = qq[:, sl]
            rq = lax.rsqrt(_group_sumsq(qh, ones_mla_ref) * inv_qk + EPS)
            q_out[0, rs, sl] = (rq * (qh * aq + qq[:, hp + g * grp:hp + (g + 1) * grp] * bq)
                                + gains_ref[6:7, :]).astype(BF16)
            kh = kv[:, sl] + kr2
            rk = lax.rsqrt(_group_sumsq(kh, ones_mla_ref) * inv_qk + EPS)
            k_out[0, rs, sl] = (rk * (kh * ak + krs2 * bk) + gains_ref[7:8, :]).astype(BF16)
            v_out[0, rs, sl] = (kv[:, hp + g * grp:hp + (g + 1) * grp] + gains_ref[8:9, :]).astype(BF16)

        for g in range(n_dil // grp):
            qd = proj[:, c2 + g * grp:c2 + (g + 1) * grp]
            qd = ((qd * lax.rsqrt(_group_sumsq(qd, ones_dil_ref) * inv_dh + EPS)) * gains_ref[4:5, :]).astype(BF16)
            kd = proj[:, c2 + n_dil + g * grp:c2 + n_dil + (g + 1) * grp]
            kd = ((kd * lax.rsqrt(_group_sumsq(kd, ones_dil_ref) * inv_dh + EPS)) * gains_ref[5:6, :]).astype(BF16)
            vd = proj[:, c2 + 2 * n_dil + g * grp:c2 + 2 * n_dil + (g + 1) * grp].astype(BF16)
            for j in range(grp // LANES):
                base = (g * (grp // LANES) + j) * DIL_GROUP
                for t, val in enumerate((qd, kd, vd)):
                    qkvd_out[0, rs, base + t * LANES:base + (t + 1) * LANES] = val[:, j * LANES:(j + 1) * LANES]


def _pre_call(x, mod_l, g1, cos_t, sin_t, w_pre, qan, wqq, kvan, wkv, gains, ones_mla, ones_dil, tm):
    b, s, d = x.shape
    hp = MLA_HEADS * HEAD_PAD
    n_dil = DIL_HEADS * DIL_HEAD_DIM
    tok = lambda i, j: (i, j, 0)
    const2 = lambda i, j: (0, 0)
    row = lambda n: _resident((1, n), const2)
    return pl.pallas_call(
        _pre_kernel,
        grid=(b, s // tm),
        in_specs=[
            pl.BlockSpec((1, tm, d), tok),
            pl.BlockSpec((1, 6, d), lambda i, j: (i, 0, 0)),
            row(d),
            pl.BlockSpec((1, tm, LANES), tok),
            pl.BlockSpec((1, tm, LANES), tok),
            _resident((d, PRE_COLS), const2),
            row(MLA_Q_LORA),
            _resident((MLA_Q_LORA, 2 * hp), const2),
            row(MLA_KV_LORA),
            _resident((MLA_KV_LORA, 2 * hp), const2),
            _resident(gains.shape, const2),
            _resident(ones_mla.shape, const2),
            _resident(ones_dil.shape, const2),
        ],
        out_specs=[
            pl.BlockSpec((1, tm, hp), tok), pl.BlockSpec((1, tm, hp), tok), pl.BlockSpec((1, tm, hp), tok),
            pl.BlockSpec((1, tm, 3 * n_dil), tok),
        ],
        out_shape=[jax.ShapeDtypeStruct((b, s, hp), BF16)] * 3 + [jax.ShapeDtypeStruct((b, s, 3 * n_dil), BF16)],
        compiler_params=_cparams(("arbitrary", "arbitrary")),
        name="pre_proj",
    )(x, mod_l, g1, cos_t, sin_t, w_pre, qan, wqq, kvan, wkv, gains, ones_mla, ones_dil)


def _mla_online_kernel(q_ref, k_ref, v_ref, o_ref, *, tk):
    q = q_ref[0]
    tq = q.shape[0]
    n_k = k_ref.shape[1] // tk

    def body(j, carry):
        m, acc = carry
        ks = pl.multiple_of(j * tk, tk)
        k = k_ref[0, pl.ds(ks, tk), :]
        v = v_ref[0, pl.ds(ks, tk), :]
        s = lax.dot_general(q, k, (((1,), (1,)), ((), ())), preferred_element_type=F32)
        m_new = jnp.maximum(m, jnp.max(s, axis=-1, keepdims=True))
        p = jnp.exp2(s - m_new)
        acc = jnp.exp2(m - m_new) * acc + jnp.dot(p.astype(BF16), v, preferred_element_type=F32)
        return m_new, acc

    init = (jnp.full((tq, 1), -jnp.inf, F32), jnp.zeros((tq, HEAD_PAD), F32))
    _, acc = lax.fori_loop(0, n_k, body, init, unroll=True)
    o_ref[0] = (acc / acc[:, MLA_V:MLA_V + 1]).astype(BF16)


def _mla_bounded_kernel(q_ref, k_ref, v_ref, o_ref, *, tk):
    n_k = k_ref.shape[1] // tk
    for qs in range(q_ref.shape[1] // MLA_SUB):
        rows = slice(qs * MLA_SUB, (qs + 1) * MLA_SUB)
        q = q_ref[0, rows, :]
        acc = jnp.zeros((MLA_SUB, HEAD_PAD), F32)
        for j in range(n_k):
            k = k_ref[0, j * tk:(j + 1) * tk, :]
            v = v_ref[0, j * tk:(j + 1) * tk, :]
            s = lax.dot_general(q, k, (((1,), (1,)), ((), ())), preferred_element_type=F32)
            acc = acc + jnp.dot(jnp.exp2(s).astype(BF16), v, preferred_element_type=F32)
        o_ref[0, rows, :] = (acc / acc[:, MLA_V:MLA_V + 1]).astype(BF16)


def _mla_call(q, k, v, tq, tk, bounded):
    b, s, _ = q.shape
    return pl.pallas_call(
        functools.partial(_mla_bounded_kernel if bounded else _mla_online_kernel, tk=tk),
        grid=(b, MLA_HEADS, s // tq),
        in_specs=[
            pl.BlockSpec((1, tq, HEAD_PAD), lambda i, h, j: (i, j, h)),
            pl.BlockSpec((1, s, HEAD_PAD), lambda i, h, j: (i, 0, h)),
            pl.BlockSpec((1, s, HEAD_PAD), lambda i, h, j: (i, 0, h)),
        ],
        out_specs=pl.BlockSpec((1, tq, HEAD_PAD), lambda i, h, j: (i, j, h)),
        out_shape=jax.ShapeDtypeStruct(q.shape, BF16),
        compiler_params=_cparams(("arbitrary", "arbitrary", "arbitrary")),
        name="mla_attn_bounded" if bounded else "mla_attn_online",
    )(q, k, v)


def _dil_kernel(q_ref, k_ref, v_ref, bias_ref, o_ref, qf, kf, vf, m_s, l_s, n_s):
    s_len = q_ref.shape[1]
    qf[...] = q_ref[0].astype(F32)
    kf[...] = k_ref[0].astype(F32)
    vf[...] = v_ref[0].astype(F32)
    lane = lax.broadcasted_iota(jnp.int32, (DIL_BQ, LANES), 1)
    first = lane < DIL_HEAD_DIM

    def band_tile(q_t, k_t, v_t, tab):
        parts = []
        for hh in range(2):
            sel = first if hh == 0 else jnp.logical_not(first)
            qm = jnp.where(sel, q_t, jnp.zeros_like(q_t))
            s = lax.dot_general(qm, k_t, (((1,), (1,)), ((), ())), preferred_element_type=F32)
            s = s + bias_ref[tab, hh]
            m = jnp.max(s, axis=-1, keepdims=True)
            p = jnp.exp2(s - m)
            l = jnp.sum(p, axis=-1, keepdims=True)
            pv = jnp.dot(p.astype(BF16), v_t, preferred_element_type=F32)
            parts.append((m, l, pv))
        (m0, l0, n0), (m1, l1, n1) = parts
        return jnp.where(first, m0, m1), jnp.where(first, l0, l1), jnp.where(first, n0, n1)

    def variant_of(n, n_blk):
        return jnp.where(n == 0, 0, jnp.where(n == n_blk - 1, 2, 1))

    n_blk0 = s_len // DIL_BQ

    def body0(n, carry):
        i0 = pl.multiple_of(n * DIL_BQ, DIL_BQ)
        ks = pl.multiple_of(jnp.clip(i0 - DIL_HALF, 0, s_len - DIL_BK), DIL_HALF)
        m, l, num = band_tile(q_ref[0, pl.ds(i0, DIL_BQ), :], k_ref[0, pl.ds(ks, DIL_BK), :],
                              v_ref[0, pl.ds(ks, DIL_BK), :], variant_of(n, n_blk0))
        m_s[pl.ds(i0, DIL_BQ), :] = m
        l_s[pl.ds(i0, DIL_BQ), :] = l
        n_s[pl.ds(i0, DIL_BQ), :] = num
        return carry

    lax.fori_loop(0, n_blk0, body0, 0, unroll=DIL_UNROLL)

    for p_idx in range(1, len(DIL_PATTERNS)):
        dil = DIL_PATTERNS[p_idx][1]
        sub_len = s_len // dil
        n_blk = sub_len // DIL_BQ

        def body(t, carry, dil=dil, sub_len=sub_len, n_blk=n_blk, p_idx=p_idx):
            r = t // n_blk
            n = t - r * n_blk
            i0 = n * DIL_BQ
            ks = jnp.clip(i0 - DIL_HALF, 0, sub_len - DIL_BK)
            rows_q = pl.ds(r + dil * i0, DIL_BQ, stride=dil)
            rows_k = pl.ds(r + dil * ks, DIL_BK, stride=dil)
            m_t, l_t, n_t = band_tile(qf[rows_q, :].astype(BF16), kf[rows_k, :].astype(BF16),
                                      vf[rows_k, :].astype(BF16), 3 * p_idx + variant_of(n, n_blk))
            m_o = m_s[rows_q, :]
            m_n = jnp.maximum(m_o, m_t)
            a = jnp.exp2(m_o - m_n)
            c = jnp.exp2(m_t - m_n)
            m_s[rows_q, :] = m_n
            l_s[rows_q, :] = a * l_s[rows_q, :] + c * l_t
            n_s[rows_q, :] = a * n_s[rows_q, :] + c * n_t
            return carry

        lax.fori_loop(0, dil * n_blk, body, 0, unroll=DIL_UNROLL)

    o_ref[0] = (n_s[...] / l_s[...]).astype(BF16)


def _dil_online_call(qkvd, bias_tabs):
    b, s, width = qkvd.shape
    n_pair = width // DIL_GROUP
    n_tab = bias_tabs.shape[0]
    blk = lambda t: pl.BlockSpec((1, s, LANES), lambda hp, i: (i, 0, 3 * hp + t))
    return pl.pallas_call(
        _dil_kernel,
        grid=(n_pair, b),
        in_specs=[blk(0), blk(1), blk(2),
                  pl.BlockSpec((n_tab, 2, DIL_BQ, DIL_BK), lambda hp, i: (0, hp, 0, 0))],
        out_specs=pl.BlockSpec((1, s, LANES), lambda hp, i: (i, 0, hp)),
        out_shape=jax.ShapeDtypeStruct((b, s, n_pair * LANES), BF16),
        scratch_shapes=[pltpu.VMEM((s, LANES), F32)] * 6,
        compiler_params=_cparams(("arbitrary", "arbitrary")),
        name="dil_attn_online",
    )(qkvd, qkvd, qkvd, bias_tabs)


def _dil_bounded_kernel(*refs):
    dils = [d for _, d in DIL_PATTERNS]
    n_in = sum(dils)
    x_refs = refs[:n_in]
    bias_ref, o_ref = refs[n_in], refs[n_in + 1]
    states = refs[n_in + 2:]
    s_len = o_ref.shape[1]
    lane = lax.broadcasted_iota(jnp.int32, (DIL_BQ, LANES), 1)
    first = lane < DIL_HEAD_DIM

    def band_tile(x_ref, i0, ks, tab):
        q_t = x_ref[0, i0:i0 + DIL_BQ, 0:LANES]
        k_t = x_ref[0, ks:ks + DIL_BK, LANES:2 * LANES]
        v_t = x_ref[0, ks:ks + DIL_BK, 2 * LANES:3 * LANES]
        zero = jnp.zeros_like(q_t)
        qm = jnp.concatenate([jnp.where(first, q_t, zero), jnp.where(first, zero, q_t)], axis=0)
        s = lax.dot_general(qm, k_t, (((1,), (1,)), ((), ())), preferred_element_type=F32)
        p = jnp.exp2(s + bias_ref[tab].reshape(2 * DIL_BQ, DIL_BK))
        l = jnp.sum(p, axis=-1, keepdims=True)
        pv = jnp.dot(p.astype(BF16), v_t, preferred_element_type=F32)
        return jnp.where(first, l[:DIL_BQ], l[DIL_BQ:]), jnp.where(first, pv[:DIL_BQ], pv[DIL_BQ:])

    op = 0
    for p_idx, dil in enumerate(dils):
        sub_len = s_len // dil
        n_blk = sub_len // DIL_BQ
        l_s, n_s = states[2 * p_idx], states[2 * p_idx + 1]
        for r in range(dil):
            for n in range(n_blk):
                i0 = n * DIL_BQ
                ks = min(max(i0 - DIL_HALF, 0), sub_len - DIL_BK)
                variant = 0 if n == 0 else (2 if n == n_blk - 1 else 1)
                l_t, n_t = band_tile(x_refs[op + r], i0, ks, 3 * p_idx + variant)
                l_s[r * sub_len + i0:r * sub_len + i0 + DIL_BQ, :] = l_t
                n_s[r * sub_len + i0:r * sub_len + i0 + DIL_BQ, :] = n_t
        op += dil

    for p_idx in range(len(dils) - 1, 0, -1):
        ratio = dils[p_idx] // dils[p_idx - 1]
        fine_len = s_len // dils[p_idx - 1]
        sub_len = s_len // dils[p_idx]
        for src, dst in ((states[2 * p_idx], states[2 * p_idx - 2]), (states[2 * p_idx + 1], states[2 * p_idx - 1])):
            for r in range(dils[p_idx]):
                r_fine, c = r % dils[p_idx - 1], r // dils[p_idx - 1]
                rows = pl.ds(r_fine * fine_len + c, sub_len, stride=ratio)
                dst[rows, :] = dst[rows, :] + src[r * sub_len:(r + 1) * sub_len, :]
    o_ref[0] = (states[1][...] / states[0][...]).astype(BF16)


def _dil_bounded_call(qkvd, bias_tabs):
    b, s, width = qkvd.shape
    n_pair = width // DIL_GROUP
    n_tab = bias_tabs.shape[0]
    operands, specs = [], []
    for _, dil in DIL_PATTERNS:
        view = qkvd.reshape(b, s // dil, dil * width)
        for r in range(dil):
            operands.append(view)
            specs.append(pl.BlockSpec((1, s // dil, DIL_GROUP),
                                      lambda hp, i, r=r, n_pair=n_pair: (i, 0, r * n_pair + hp)))
    return pl.pallas_call(
        _dil_bounded_kernel,
        grid=(n_pair, b),
        in_specs=specs + [pl.BlockSpec((n_tab, 2, DIL_BQ, DIL_BK), lambda hp, i: (0, hp, 0, 0))],
        out_specs=pl.BlockSpec((1, s, LANES), lambda hp, i: (i, 0, hp)),
        out_shape=jax.ShapeDtypeStruct((b, s, n_pair * LANES), BF16),
        scratch_shapes=[pltpu.VMEM((s, LANES), F32)] * (2 * len(DIL_PATTERNS)),
        compiler_params=_cparams(("arbitrary", "arbitrary")),
        name="dil_attn_bounded",
    )(*operands, bias_tabs)


def _post_kernel(x_ref, ya_ref, yb_ref, mod_ref, g1_ref, wg_ref, wa_ref, wb_ref, wo_ref, o_ref):
    x = x_ref[0]
    h = _modulated_norm(x, g1_ref[...], mod_ref[0, 0:1, :], mod_ref[0, 1:2, :])
    gates = jnp.dot(h.astype(BF16), wg_ref[...], preferred_element_type=F32)
    d = x.shape[-1]
    a = jnp.dot(ya_ref[0], wa_ref[...], preferred_element_type=F32)
    bb = jnp.dot(yb_ref[0], wb_ref[...], preferred_element_type=F32)
    merged = jax.nn.sigmoid(gates[:, :d]) * a + jax.nn.sigmoid(gates[:, d:]) * bb
    upd = jnp.dot(merged.astype(BF16), wo_ref[...], preferred_element_type=F32)
    o_ref[0] = x + mod_ref[0, 2:3, :] * upd


def _post_call(x, ya, yb, mod_l, g1, wg, wa, wb, wo, tm):
    b, s, d = x.shape
    tok = lambda i, j: (i, j, 0)
    const2 = lambda i, j: (0, 0)
    return pl.pallas_call(
        _post_kernel,
        grid=(b, s // tm),
        in_specs=[
            pl.BlockSpec((1, tm, d), tok),
            pl.BlockSpec((1, tm, ya.shape[-1]), tok),
            pl.BlockSpec((1, tm, yb.shape[-1]), tok),
            pl.BlockSpec((1, 6, d), lambda i, j: (i, 0, 0)),
            _resident((1, d), const2),
            _resident(wg.shape, const2),
            _resident(wa.shape, const2),
            _resident(wb.shape, const2),
            _resident(wo.shape, const2),
        ],
        out_specs=pl.BlockSpec((1, tm, d), tok),
        out_shape=jax.ShapeDtypeStruct(x.shape, F32),
        compiler_params=_cparams(("arbitrary", "arbitrary")),
        name="attn_out",
    )(x, ya, yb, mod_l, g1, wg, wa, wb, wo)


def _ffn_kernel(x_ref, mod_ref, g2_ref, wg_ref, wu_ref, wd_ref, o_ref):
    x = x_ref[0]
    h = _modulated_norm(x, g2_ref[...], mod_ref[0, 3:4, :], mod_ref[0, 4:5, :]).astype(BF16)
    g = jnp.dot(h, wg_ref[...], preferred_element_type=F32)
    u = jnp.dot(h, wu_ref[...], preferred_element_type=F32)
    act = (g * jax.nn.sigmoid(g)) * u
    upd = jnp.dot(act.astype(BF16), wd_ref[...], preferred_element_type=F32)
    o_ref[0] = x + mod_ref[0, 5:6, :] * upd


def _ffn_call(x, mod_l, g2, wg, wu, wd, tm):
    b, s, d = x.shape
    tok = lambda i, j: (i, j, 0)
    const2 = lambda i, j: (0, 0)
    return pl.pallas_call(
        _ffn_kernel,
        grid=(b, s // tm),
        in_specs=[
            pl.BlockSpec((1, tm, d), tok),
            pl.BlockSpec((1, 6, d), lambda i, j: (i, 0, 0)),
            _resident((1, d), const2),
            _resident(wg.shape, const2),
            _resident(wu.shape, const2),
            _resident(wd.shape, const2),
        ],
        out_specs=pl.BlockSpec((1, tm, d), tok),
        out_shape=jax.ShapeDtypeStruct(x.shape, F32),
        compiler_params=_cparams(("arbitrary", "arbitrary")),
        name="swiglu",
    )(x, mod_l, g2, wg, wu, wd)


def _pad_heads(w, width):
    rows = w.shape[0]
    w = w.reshape(rows, MLA_HEADS, width)
    return jnp.pad(w, ((0, 0), (0, 0), (0, HEAD_PAD - width))).reshape(rows, MLA_HEADS * HEAD_PAD)


def _pad_lanes(g):
    return jnp.pad(g, (0, LANES - g.shape[0]))


def _swap_rope(g):
    half = MLA_ROPE // 2
    return jnp.concatenate([jnp.zeros((MLA_NOPE,), g.dtype), g[MLA_NOPE + half:], g[MLA_NOPE:MLA_NOPE + half]])


def _block_ones(block, size=2 * LANES):
    idx = np.arange(size) // block
    return jnp.asarray(idx[:, None] == idx[None, :], BF16)


def kernel(x, c, positions, rel_bias, norm1_g, norm2_g, ada_w, ada_b, w_in, q_a_norm, w_q_b, kv_a_norm, w_kv_b,
           q_norm_a, k_norm_a, q_norm_b, k_norm_b, w_branch_a, w_branch_b, w_out, w_ffn_gate, w_ffn_up,
           w_ffn_down):
    depth = w_in.shape[0]
    b, s, d = x.shape

    mod = _ada_call(c, ada_w, ada_b).reshape(depth, b, 6, d)
    cos_t, sin_t = _rope_call(positions)
    dil_qk_bound = DIL_HEAD_DIM * jnp.max(jnp.abs(q_norm_b * (DIL_HEAD_DIM ** -0.5 * LOG2E))) * jnp.max(jnp.abs(k_norm_b))
    dil_shift = dil_qk_bound + jnp.max(rel_bias) * LOG2E
    dil_range = 2.0 * dil_qk_bound + (jnp.max(rel_bias) - jnp.min(rel_bias)) * LOG2E
    bias_tabs = _bias_call(rel_bias, jnp.zeros((), F32))
    bias_tabs_shifted = _bias_call(rel_bias, dil_shift)
    ones_mla = _block_ones(HEAD_PAD)
    ones_dil = _block_ones(DIL_HEAD_DIM)
    half = MLA_ROPE // 2
    lane2 = np.arange(2 * LANES) % LANES
    spare_qk = jnp.asarray(lane2 == MLA_QK, F32)
    spare_v = jnp.asarray(lane2 == MLA_V, F32)

    for l in range(depth):
        wl = w_in[l]
        k_rope_cols = jnp.pad(wl[:, _C_KR:_C_QKVB], ((0, 0), (MLA_NOPE, LANES - MLA_QK)))
        w_pre = jnp.concatenate([wl[:, _C_QC:_C_KR], k_rope_cols, wl[:, _C_QKVB:_C_GATE]], axis=1).astype(BF16)
        w_gates = wl[:, _C_GATE:].astype(BF16)
        wq3 = w_q_b[l].reshape(MLA_Q_LORA, MLA_HEADS, MLA_QK)
        lo, hi = wq3[..., MLA_NOPE:MLA_NOPE + half], wq3[..., MLA_NOPE + half:]
        wq_sw = jnp.concatenate([jnp.zeros_like(wq3[..., :MLA_NOPE]), hi, lo], axis=-1)
        wqq = jnp.concatenate([_pad_heads(w_q_b[l], MLA_QK),
                               _pad_heads(wq_sw.reshape(MLA_Q_LORA, -1), MLA_QK)], axis=1).astype(BF16)
        wkv3 = w_kv_b[l].reshape(MLA_KV_LORA, MLA_HEADS, MLA_NOPE + MLA_V)
        wkv = jnp.concatenate([_pad_heads(wkv3[:, :, :MLA_NOPE].reshape(MLA_KV_LORA, -1), MLA_NOPE),
                               _pad_heads(wkv3[:, :, MLA_NOPE:].reshape(MLA_KV_LORA, -1), MLA_V)], axis=1).astype(BF16)
        gq = q_norm_a[l] * (MLA_QK ** -0.5 * LOG2E)
        gk = k_norm_a[l]
        score_bound = MLA_QK * jnp.max(jnp.abs(gq)) * jnp.max(jnp.abs(gk))
        gains = jnp.stack([
            jnp.tile(_pad_lanes(gq), 2), jnp.tile(_pad_lanes(_swap_rope(gq)), 2),
            jnp.tile(_pad_lanes(gk), 2), jnp.tile(_pad_lanes(_swap_rope(gk)), 2),
            jnp.tile(q_norm_b[l] * (DIL_HEAD_DIM ** -0.5 * LOG2E), 4), jnp.tile(k_norm_b[l], 4),
            -score_bound * spare_qk, spare_qk, spare_v] + [jnp.zeros((2 * LANES,), F32)] * 7)
        wa = jnp.pad(w_branch_a[l].reshape(MLA_HEADS, MLA_V, d), ((0, 0), (0, HEAD_PAD - MLA_V), (0, 0)))
        wa = wa.reshape(MLA_HEADS * HEAD_PAD, d).astype(BF16)
        g1 = norm1_g[l].reshape(1, d)
        g2 = norm2_g[l].reshape(1, d)

        q, k, v, qkvd = _pre_call(x, mod[l], g1, cos_t, sin_t, w_pre, q_a_norm[l].reshape(1, -1), wqq,
                                        kv_a_norm[l].reshape(1, -1), wkv, gains, ones_mla, ones_dil, tm=512)
        ya = lax.cond(score_bound <= MLA_SAFE_BOUND,
                      functools.partial(_mla_call, tq=4096, tk=256, bounded=True),
                      functools.partial(_mla_call, tq=512, tk=1024, bounded=False), q, k, v)
        yb = lax.cond(dil_range <= DIL_SAFE_RANGE,
                      lambda a, shifted, plain: _dil_bounded_call(a, shifted),
                      lambda a, shifted, plain: _dil_online_call(a, plain),
                      qkvd, bias_tabs_shifted, bias_tabs)
        x = _post_call(x, ya, yb, mod[l], g1, w_gates, wa, w_branch_b[l].astype(BF16), w_out[l].astype(BF16), tm=512)
        x = _ffn_call(x, mod[l], g2, w_ffn_gate[l].astype(BF16), w_ffn_up[l].astype(BF16),
                      w_ffn_down[l].astype(BF16), tm=512)
    return x
```

```python
import functools
import math

import jax
import jax.numpy as jnp
import numpy as np
from jax import lax
from jax.experimental import pallas as pl
from jax.experimental.pallas import tpu as pltpu

F32 = jnp.float32
BF16 = jnp.bfloat16

LANES = 128
VMEM_LIMIT_BYTES = 56 * 1024 * 1024

D_MODEL = 1024
MLA_HEADS = 8
MLA_Q_LORA = 256
MLA_KV_LORA = 128
MLA_NOPE = 64
MLA_ROPE = 32
MLA_V = 64
MLA_QK = MLA_NOPE + MLA_ROPE
ROPE_THETA = 10000.0
DIL_HEADS = 8
DIL_HEAD_DIM = 64
DIL_PATTERNS = ((128, 1), (512, 4), (2048, 16))
DIL_HALF = 64
REL_BUCKETS = 32
REL_MAX_DIST = 1024
EPS = 1e-6
NEG_INF = -1e30
LOG2E = math.log2(math.e)
MLA_SAFE_BOUND = 60.0
MLA_SUB = 512

_C_QC = 0
_C_KVC = _C_QC + MLA_Q_LORA
_C_KR = _C_KVC + MLA_KV_LORA
_C_QKVB = _C_KR + MLA_ROPE
_C_GATE = _C_QKVB + 3 * DIL_HEADS * DIL_HEAD_DIM
PRE_COLS = MLA_Q_LORA + MLA_KV_LORA + LANES + 3 * DIL_HEADS * DIL_HEAD_DIM
HEAD_PAD = LANES

DIL_BQ = 2 * DIL_HALF
DIL_BK = 4 * DIL_HALF
DIL_CLASSES = 4
DIL_GROUP = 3 * LANES
DIL_SAFE_RANGE = 120.0


def _cparams(sem):
    return pltpu.CompilerParams(dimension_semantics=sem, vmem_limit_bytes=VMEM_LIMIT_BYTES)


def _resident(shape, index_map):
    return pl.BlockSpec(shape, index_map, pipeline_mode=pl.Buffered(1))


def _ada_kernel(c_ref, w_ref, b_ref, o_ref):
    c = c_ref[...]
    act = (c * jax.nn.sigmoid(c)).astype(BF16)
    o_ref[0] = jnp.dot(act, w_ref[0].astype(BF16), preferred_element_type=F32) + b_ref[0]


def _ada_call(c, ada_w, ada_b):
    depth, d, six_d = ada_w.shape
    b = c.shape[0]
    n_chunk = six_d // d
    return pl.pallas_call(
        _ada_kernel,
        grid=(depth, n_chunk),
        in_specs=[
            pl.BlockSpec((b, d), lambda l, j: (0, 0)),
            pl.BlockSpec((1, d, d), lambda l, j: (l, 0, j)),
            pl.BlockSpec((1, 1, d), lambda l, j: (l, 0, j)),
        ],
        out_specs=pl.BlockSpec((1, b, d), lambda l, j: (l, 0, j)),
        out_shape=jax.ShapeDtypeStruct((depth, b, six_d), F32),
        compiler_params=_cparams(("arbitrary", "arbitrary")),
        name="ada_mod",
    )(c, ada_w, ada_b.reshape(depth, 1, six_d))


def _rope_kernel(pos_ref, inv_ref, c_ref, s_ref):
    ang = pos_ref[0].astype(F32) * inv_ref[...]
    cosv = jnp.cos(ang)
    sinv = jnp.sin(ang)
    lane = lax.broadcasted_iota(jnp.int32, ang.shape, 1)
    half = MLA_ROPE // 2
    c_ref[0] = jnp.where(lane < MLA_NOPE, 1.0, jnp.where(lane < MLA_QK, cosv, 0.0))
    s_ref[0] = jnp.where(lane < MLA_NOPE, 0.0,
                         jnp.where(lane < MLA_NOPE + half, -sinv, jnp.where(lane < MLA_QK, sinv, 0.0)))


def _rope_call(positions):
    b, s = positions.shape
    half = MLA_ROPE // 2
    inv = ROPE_THETA ** (-jnp.arange(half, dtype=F32) / half)
    inv_lane = jnp.zeros((1, LANES), F32)
    inv_lane = inv_lane.at[0, MLA_NOPE:MLA_NOPE + half].set(inv).at[0, MLA_NOPE + half:MLA_QK].set(inv)
    ts = 512
    spec = pl.BlockSpec((1, ts, LANES), lambda i, j: (i, j, 0))
    return pl.pallas_call(
        _rope_kernel,
        grid=(b, s // ts),
        in_specs=[pl.BlockSpec((1, ts, 1), lambda i, j: (i, j, 0)),
                  pl.BlockSpec((1, LANES), lambda i, j: (0, 0))],
        out_specs=[spec, spec],
        out_shape=[jax.ShapeDtypeStruct((b, s, LANES), F32)] * 2,
        compiler_params=_cparams(("arbitrary", "arbitrary")),
        name="rope_tables",
    )(positions.reshape(b, s, 1), inv_lane)


def _t5_bucket(rel):
    nb = REL_BUCKETS // 2
    max_exact = nb // 2
    ret = jnp.where(rel > 0, nb, 0)
    n = jnp.abs(rel)
    nf = jnp.maximum(n, 1).astype(F32)
    large = max_exact + (jnp.log(nf / max_exact) / math.log(REL_MAX_DIST / max_exact) * (nb - max_exact)).astype(jnp.int32)
    large = jnp.minimum(large, nb - 1)
    return ret + jnp.where(n < max_exact, n, large)


def _bias_kernel(bucket_ref, rb_ref, shift_ref, o_ref):
    h = pl.program_id(1)
    bucket = bucket_ref[0]
    acc = jnp.full(bucket.shape, NEG_INF, F32)
    for bkt in range(REL_BUCKETS):
        acc = jnp.where(bucket == bkt, rb_ref[bkt, h], acc)
    o_ref[0, 0] = acc * LOG2E - shift_ref[0]


def _bias_call(rel_bias, shift):
    tabs = []
    for _, dil in DIL_PATTERNS:
        qi, kj = np.arange(DIL_BQ), np.arange(DIL_BK)
        if dil == 1:
            qi = DIL_CLASSES * (qi % (DIL_BQ // DIL_CLASSES)) + qi // (DIL_BQ // DIL_CLASSES)
            kj = DIL_CLASSES * (kj % (DIL_BK // DIL_CLASSES)) + kj // (DIL_BK // DIL_CLASSES)
        qi = jnp.asarray(qi, jnp.int32)[:, None]
        kj = jnp.asarray(kj, jnp.int32)[None, :]
        for variant in range(3):
            rel = kj - variant * DIL_HALF - qi
            tabs.append(jnp.where(jnp.abs(rel) <= DIL_HALF, _t5_bucket(rel * dil), -1))
    bucket = jnp.stack(tabs).astype(jnp.int32)
    n_tab = bucket.shape[0]
    return pl.pallas_call(
        _bias_kernel,
        grid=(n_tab, DIL_HEADS),
        in_specs=[pl.BlockSpec((1, DIL_BQ, DIL_BK), lambda t, h: (t, 0, 0)),
                  pl.BlockSpec(memory_space=pltpu.SMEM), pl.BlockSpec(memory_space=pltpu.SMEM)],
        out_specs=pl.BlockSpec((1, 1, DIL_BQ, DIL_BK), lambda t, h: (t, h, 0, 0)),
        out_shape=jax.ShapeDtypeStruct((n_tab, DIL_HEADS, DIL_BQ, DIL_BK), F32),
        compiler_params=_cparams(("arbitrary", "arbitrary")),
        name="bias_tables",
    )(bucket, rel_bias, jnp.reshape(shift, (1,)).astype(F32))


def _modulated_norm(x, gain, shift, scale):
    r = lax.rsqrt(jnp.mean(x * x, axis=-1, keepdims=True) + EPS)
    return ((x * r) * gain) * (1.0 + scale) + shift


PRE_SUB = 256


def _group_sumsq(t, ones_ref):
    return jnp.dot((t * t).astype(BF16), ones_ref[...], preferred_element_type=F32)


def _pre_kernel(x_ref, mod_ref, g1_ref, cos_ref, sin_ref, w_ref, qan_ref, wqq_ref, kvan_ref, wkv_ref,
                gains_ref, ones_mla_ref, ones_dil_ref,
                q_out, k_out, v_out, qkvd_out):
    tm = x_ref.shape[1]
    hp = MLA_HEADS * HEAD_PAD
    n_dil = DIL_HEADS * DIL_HEAD_DIM
    grp = 2 * LANES
    c0 = MLA_Q_LORA
    c1 = c0 + MLA_KV_LORA
    c2 = c1 + LANES
    half = MLA_ROPE // 2
    inv_qk = 1.0 / MLA_QK
    inv_dh = 1.0 / DIL_HEAD_DIM
    lane = lax.broadcasted_iota(jnp.int32, (PRE_SUB, LANES), 1)
    low_half = lane < MLA_NOPE + half

    for sub in range(tm // PRE_SUB):
        rs = slice(sub * PRE_SUB, (sub + 1) * PRE_SUB)
        x = x_ref[0, rs, :]
        h = _modulated_norm(x, g1_ref[...], mod_ref[0, 0:1, :], mod_ref[0, 1:2, :])
        proj = jnp.dot(h.astype(BF16), w_ref[...], preferred_element_type=F32)

        qc = proj[:, :c0]
        qcn = (qc * lax.rsqrt(jnp.mean(qc * qc, axis=-1, keepdims=True) + EPS)) * qan_ref[...]
        qq = jnp.dot(qcn.astype(BF16), wqq_ref[...], preferred_element_type=F32)
        kvc = proj[:, c0:c1]
        kvn = (kvc * lax.rsqrt(jnp.mean(kvc * kvc, axis=-1, keepdims=True) + EPS)) * kvan_ref[...]
        kv = jnp.dot(kvn.astype(BF16), wkv_ref[...], preferred_element_type=F32)
        k_rope = proj[:, c1:c2]
        k_rope_sw = jnp.where(low_half, pltpu.roll(k_rope, LANES - half, 1), pltpu.roll(k_rope, half, 1))

        cos2 = jnp.concatenate([cos_ref[0, rs, :]] * 2, axis=1)
        sin2 = jnp.concatenate([sin_ref[0, rs, :]] * 2, axis=1)
        aq = gains_ref[0:1, :] * cos2
        bq = gains_ref[1:2, :] * sin2
        ak = gains_ref[2:3, :] * cos2
        bk = gains_ref[3:4, :] * sin2
        kr2 = jnp.concatenate([k_rope] * 2, axis=1)
        krs2 = jnp.concatenate([k_rope_sw] * 2, axis=1)
        for g in range(hp // grp):
            sl = slice(g * grp, (g + 1) * grp)
            qh = qq[:, sl]
            rq = lax.rsqrt(_group_sumsq(qh, ones_mla_ref) * inv_qk + EPS)
            q_out[0, rs, sl] = (rq * (qh * aq + qq[:, hp + g * grp:hp + (g + 1) * grp] * bq)
                                + gains_ref[6:7, :]).astype(BF16)
            kh = kv[:, sl] + kr2
            rk = lax.rsqrt(_group_sumsq(kh, ones_mla_ref) * inv_qk + EPS)
            k_out[0, rs, sl] = (rk * (kh * ak + krs2 * bk) + gains_ref[7:8, :]).astype(BF16)
            v_out[0, rs, sl] = (kv[:, hp + g * grp:hp + (g + 1) * grp] + gains_ref[8:9, :]).astype(BF16)

        for g in range(n_dil // grp):
            qd = proj[:, c2 + g * grp:c2 + (g + 1) * grp]
            qd = ((qd * lax.rsqrt(_group_sumsq(qd, ones_dil_ref) * inv_dh + EPS)) * gains_ref[4:5, :]).astype(BF16)
            kd = proj[:, c2 + n_dil + g * grp:c2 + n_dil + (g + 1) * grp]
            kd = ((kd * lax.rsqrt(_group_sumsq(kd, ones_dil_ref) * inv_dh + EPS)) * gains_ref[5:6, :]).astype(BF16)
            vd = proj[:, c2 + 2 * n_dil + g * grp:c2 + 2 * n_dil + (g + 1) * grp].astype(BF16)
            for j in range(grp // LANES):
                base = (g * (grp // LANES) + j) * DIL_GROUP
                for t, val in enumerate((qd, kd, vd)):
                    qkvd_out[0, rs, base + t * LANES:base + (t + 1) * LANES] = val[:, j * LANES:(j + 1) * LANES]


def _pre_call(x, mod_l, g1, cos_t, sin_t, w_pre, qan, wqq, kvan, wkv, gains, ones_mla, ones_dil, tm):
    b, s, d = x.shape
    hp = MLA_HEADS * HEAD_PAD
    n_dil = DIL_HEADS * DIL_HEAD_DIM
    tok = lambda i, j: (i, j, 0)
    const2 = lambda i, j: (0, 0)
    row = lambda n: _resident((1, n), const2)
    return pl.pallas_call(
        _pre_kernel,
        grid=(b, s // tm),
        in_specs=[
            pl.BlockSpec((1, tm, d), tok),
            pl.BlockSpec((1, 6, d), lambda i, j: (i, 0, 0)),
            row(d),
            pl.BlockSpec((1, tm, LANES), tok),
            pl.BlockSpec((1, tm, LANES), tok),
            _resident((d, PRE_COLS), const2),
            row(MLA_Q_LORA),
            _resident((MLA_Q_LORA, 2 * hp), const2),
            row(MLA_KV_LORA),
            _resident((MLA_KV_LORA, 2 * hp), const2),
            _resident(gains.shape, const2),
            _resident(ones_mla.shape, const2),
            _resident(ones_dil.shape, const2),
        ],
        out_specs=[
            pl.BlockSpec((1, tm, hp), tok), pl.BlockSpec((1, tm, hp), tok), pl.BlockSpec((1, tm, hp), tok),
            pl.BlockSpec((1, tm, 3 * n_dil), tok),
        ],
        out_shape=[jax.ShapeDtypeStruct((b, s, hp), BF16)] * 3 + [jax.ShapeDtypeStruct((b, s, 3 * n_dil), BF16)],
        compiler_params=_cparams(("arbitrary", "arbitrary")),
        name="pre_proj",
    )(x, mod_l, g1, cos_t, sin_t, w_pre, qan, wqq, kvan, wkv, gains, ones_mla, ones_dil)


def _mla_online_kernel(q_ref, k_ref, v_ref, o_ref, *, tk):
    q = q_ref[0]
    tq = q.shape[0]
    n_k = k_ref.shape[1] // tk

    def body(j, carry):
        m, acc = carry
        ks = pl.multiple_of(j * tk, tk)
        k = k_ref[0, pl.ds(ks, tk), :]
        v = v_ref[0, pl.ds(ks, tk), :]
        s = lax.dot_general(q, k, (((1,), (1,)), ((), ())), preferred_element_type=F32)
        m_new = jnp.maximum(m, jnp.max(s, axis=-1, keepdims=True))
        p = jnp.exp2(s - m_new)
        acc = jnp.exp2(m - m_new) * acc + jnp.dot(p.astype(BF16), v, preferred_element_type=F32)
        return m_new, acc

    init = (jnp.full((tq, 1), -jnp.inf, F32), jnp.zeros((tq, HEAD_PAD), F32))
    _, acc = lax.fori_loop(0, n_k, body, init, unroll=True)
    o_ref[0] = (acc / acc[:, MLA_V:MLA_V + 1]).astype(BF16)


def _mla_bounded_kernel(q_ref, k_ref, v_ref, o_ref, *, tk):
    n_k = k_ref.shape[1] // tk
    for qs in range(q_ref.shape[1] // MLA_SUB):
        rows = slice(qs * MLA_SUB, (qs + 1) * MLA_SUB)
        q = q_ref[0, rows, :]
        acc = jnp.zeros((MLA_SUB, HEAD_PAD), F32)
        for j in range(n_k):
            k = k_ref[0, j * tk:(j + 1) * tk, :]
            v = v_ref[0, j * tk:(j + 1) * tk, :]
            s = lax.dot_general(q, k, (((1,), (1,)), ((), ())), preferred_element_type=F32)
            acc = acc + jnp.dot(jnp.exp2(s).astype(BF16), v, preferred_element_type=F32)
        o_ref[0, rows, :] = (acc / acc[:, MLA_V:MLA_V + 1]).astype(BF16)


def _mla_call(q, k, v, tq, tk, bounded):
    b, s, _ = q.shape
    return pl.pallas_call(
        functools.partial(_mla_bounded_kernel if bounded else _mla_online_kernel, tk=tk),
        grid=(b, MLA_HEADS, s // tq),
        in_specs=[
            pl.BlockSpec((1, tq, HEAD_PAD), lambda i, h, j: (i, j, h)),
            pl.BlockSpec((1, s, HEAD_PAD), lambda i, h, j: (i, 0, h)),
            pl.BlockSpec((1, s, HEAD_PAD), lambda i, h, j: (i, 0, h)),
        ],
        out_specs=pl.BlockSpec((1, tq, HEAD_PAD), lambda i, h, j: (i, j, h)),
        out_shape=jax.ShapeDtypeStruct(q.shape, BF16),
        compiler_params=_cparams(("arbitrary", "arbitrary", "arbitrary")),
        name="mla_attn_bounded" if bounded else "mla_attn_online",
    )(q, k, v)


def _dil_kernel(x_ref, bias_ref, o_ref, qf, kf, vf, *states, bounded):
    s_len = o_ref.shape[1]
    cls_len = s_len // DIL_CLASSES
    piece_q = DIL_BQ // DIL_CLASSES
    piece_k = DIL_BK // DIL_CLASSES
    lanes = (slice(0, LANES), slice(LANES, 2 * LANES), slice(2 * LANES, 3 * LANES))
    lane = lax.broadcasted_iota(jnp.int32, (DIL_BQ, LANES), 1)
    first = lane < DIL_HEAD_DIM
    for dst, sl in zip((qf, kf, vf), lanes):
        dst[...] = x_ref[0, :, sl].astype(F32)

    def band_tile(q_t, k_t, v_t, tab):
        zero = jnp.zeros_like(q_t)
        qm = jnp.concatenate([jnp.where(first, q_t, zero), jnp.where(first, zero, q_t)], axis=0)
        s = lax.dot_general(qm, k_t, (((1,), (1,)), ((), ())), preferred_element_type=F32)
        s = s + bias_ref[tab].reshape(2 * DIL_BQ, DIL_BK)
        if not bounded:
            m = jnp.max(s, axis=-1, keepdims=True)
            s = s - m
        p = jnp.exp2(s)
        l = jnp.sum(p, axis=-1, keepdims=True)
        pv = jnp.dot(p.astype(BF16), v_t, preferred_element_type=F32)
        out = (jnp.where(first, l[:DIL_BQ], l[DIL_BQ:]), jnp.where(first, pv[:DIL_BQ], pv[DIL_BQ:]))
        return out if bounded else (jnp.where(first, m[:DIL_BQ], m[DIL_BQ:]),) + out

    def merge(old, new):
        if bounded:
            return tuple(o + n for o, n in zip(old, new))
        m_n = jnp.maximum(old[0], new[0])
        a = jnp.exp2(old[0] - m_n)
        c = jnp.exp2(new[0] - m_n)
        return (m_n, a * old[1] + c * new[1], a * old[2] + c * new[2])

    def variant_of(n, n_blk):
        return 0 if n == 0 else (2 if n == n_blk - 1 else 1)

    p_idx = 1
    n_blk = cls_len // DIL_BQ
    for r in range(DIL_CLASSES):
        for n in range(n_blk):
            i0 = r * cls_len + n * DIL_BQ
            ks = r * cls_len + min(max(n * DIL_BQ - DIL_HALF, 0), cls_len - DIL_BK)
            vals = band_tile(x_ref[0, i0:i0 + DIL_BQ, lanes[0]], x_ref[0, ks:ks + DIL_BK, lanes[1]],
                             x_ref[0, ks:ks + DIL_BK, lanes[2]], 3 * p_idx + variant_of(n, n_blk))
            for st, val in zip(states, vals):
                st[i0:i0 + DIL_BQ, :] = val

    p_idx = 0
    n_blk = s_len // DIL_BQ
    for n in range(n_blk):
        i0 = n * DIL_BQ // DIL_CLASSES
        ks = min(max(n * DIL_BQ - DIL_HALF, 0), s_len - DIL_BK) // DIL_CLASSES
        rows_q = [slice(r * cls_len + i0, r * cls_len + i0 + piece_q) for r in range(DIL_CLASSES)]
        rows_k = [slice(r * cls_len + ks, r * cls_len + ks + piece_k) for r in range(DIL_CLASSES)]
        vals = band_tile(jnp.concatenate([x_ref[0, rq, lanes[0]] for rq in rows_q], axis=0),
                         jnp.concatenate([x_ref[0, rk, lanes[1]] for rk in rows_k], axis=0),
                         jnp.concatenate([x_ref[0, rk, lanes[2]] for rk in rows_k], axis=0),
                         3 * p_idx + variant_of(n, n_blk))
        for r, rq in enumerate(rows_q):
            piece = slice(r * piece_q, (r + 1) * piece_q)
            merged = merge(tuple(st[rq, :] for st in states), tuple(val[piece] for val in vals))
            for st, val in zip(states, merged):
                st[rq, :] = val

    p_idx = 2
    ratio = DIL_PATTERNS[p_idx][1] // DIL_CLASSES
    sub_len = cls_len // ratio
    n_blk = sub_len // DIL_BQ
    for r in range(DIL_CLASSES):
        for c in range(ratio):
            for n in range(n_blk):
                ks = min(max(n * DIL_BQ - DIL_HALF, 0), sub_len - DIL_BK)
                rows_q = pl.ds(r * cls_len + c + ratio * n * DIL_BQ, DIL_BQ, stride=ratio)
                rows_k = pl.ds(r * cls_len + c + ratio * ks, DIL_BK, stride=ratio)
                vals = band_tile(qf[rows_q, :].astype(BF16), kf[rows_k, :].astype(BF16),
                                 vf[rows_k, :].astype(BF16), 3 * p_idx + variant_of(n, n_blk))
                merged = merge(tuple(st[rows_q, :] for st in states), vals)
                for st, val in zip(states, merged):
                    st[rows_q, :] = val

    o_ref[0] = (states[-1][...] / states[-2][...]).astype(BF16)


def _dil_call(qkvd, bias_tabs, bounded):
    b, s, width = qkvd.shape
    n_pair = width // DIL_GROUP
    n_tab = bias_tabs.shape[0]
    n_state = 2 if bounded else 3
    return pl.pallas_call(
        functools.partial(_dil_kernel, bounded=bounded),
        grid=(n_pair, b),
        in_specs=[pl.BlockSpec((1, s, DIL_GROUP), lambda hp, i: (i, 0, hp)),
                  pl.BlockSpec((n_tab, 2, DIL_BQ, DIL_BK), lambda hp, i: (0, hp, 0, 0))],
        out_specs=pl.BlockSpec((1, s, LANES), lambda hp, i: (i, 0, hp)),
        out_shape=jax.ShapeDtypeStruct((b, s, n_pair * LANES), BF16),
        scratch_shapes=[pltpu.VMEM((s, LANES), F32)] * (3 + n_state),
        compiler_params=_cparams(("arbitrary", "arbitrary")),
        name="dil_attn_bounded" if bounded else "dil_attn_online",
    )(qkvd, bias_tabs)


def _post_kernel(x_ref, ya_ref, yb_ref, mod_ref, g1_ref, wg_ref, wa_ref, wb_ref, wo_ref, o_ref):
    x = x_ref[0]
    h = _modulated_norm(x, g1_ref[...], mod_ref[0, 0:1, :], mod_ref[0, 1:2, :])
    gates = jnp.dot(h.astype(BF16), wg_ref[...], preferred_element_type=F32)
    d = x.shape[-1]
    a = jnp.dot(ya_ref[0], wa_ref[...], preferred_element_type=F32)
    bb = jnp.dot(yb_ref[0], wb_ref[...], preferred_element_type=F32)
    merged = jax.nn.sigmoid(gates[:, :d]) * a + jax.nn.sigmoid(gates[:, d:]) * bb
    upd = jnp.dot(merged.astype(BF16), wo_ref[...], preferred_element_type=F32)
    o_ref[0] = x + mod_ref[0, 2:3, :] * upd


def _post_call(x, ya, yb, mod_l, g1, wg, wa, wb, wo, tm):
    b, s, d = x.shape
    tok = lambda i, j: (i, j, 0)
    const2 = lambda i, j: (0, 0)
    return pl.pallas_call(
        _post_kernel,
        grid=(b, s // tm),
        in_specs=[
            pl.BlockSpec((1, tm, d), tok),
            pl.BlockSpec((1, tm, ya.shape[-1]), tok),
            pl.BlockSpec((1, tm, yb.shape[-1]), tok),
            pl.BlockSpec((1, 6, d), lambda i, j: (i, 0, 0)),
            _resident((1, d), const2),
            _resident(wg.shape, const2),
            _resident(wa.shape, const2),
            _resident(wb.shape, const2),
            _resident(wo.shape, const2),
        ],
        out_specs=pl.BlockSpec((1, tm, d), tok),
        out_shape=jax.ShapeDtypeStruct(x.shape, F32),
        compiler_params=_cparams(("arbitrary", "arbitrary")),
        name="attn_out",
    )(x, ya, yb, mod_l, g1, wg, wa, wb, wo)


def _ffn_kernel(x_ref, mod_ref, g2_ref, wg_ref, wu_ref, wd_ref, o_ref):
    x = x_ref[0]
    h = _modulated_norm(x, g2_ref[...], mod_ref[0, 3:4, :], mod_ref[0, 4:5, :]).astype(BF16)
    g = jnp.dot(h, wg_ref[...], preferred_element_type=F32)
    u = jnp.dot(h, wu_ref[...], preferred_element_type=F32)
    act = (g * jax.nn.sigmoid(g)) * u
    upd = jnp.dot(act.astype(BF16), wd_ref[...], preferred_element_type=F32)
    o_ref[0] = x + mod_ref[0, 5:6, :] * upd


def _ffn_call(x, mod_l, g2, wg, wu, wd, tm):
    b, s, d = x.shape
    tok = lambda i, j: (i, j, 0)
    const2 = lambda i, j: (0, 0)
    return pl.pallas_call(
        _ffn_kernel,
        grid=(b, s // tm),
        in_specs=[
            pl.BlockSpec((1, tm, d), tok),
            pl.BlockSpec((1, 6, d), lambda i, j: (i, 0, 0)),
            _resident((1, d), const2),
            _resident(wg.shape, const2),
            _resident(wu.shape, const2),
            _resident(wd.shape, const2),
        ],
        out_specs=pl.BlockSpec((1, tm, d), tok),
        out_shape=jax.ShapeDtypeStruct(x.shape, F32),
        compiler_params=_cparams(("arbitrary", "arbitrary")),
        name="swiglu",
    )(x, mod_l, g2, wg, wu, wd)


def _pad_heads(w, width):
    rows = w.shape[0]
    w = w.reshape(rows, MLA_HEADS, width)
    return jnp.pad(w, ((0, 0), (0, 0), (0, HEAD_PAD - width))).reshape(rows, MLA_HEADS * HEAD_PAD)


def _pad_lanes(g):
    return jnp.pad(g, (0, LANES - g.shape[0]))


def _swap_rope(g):
    half = MLA_ROPE // 2
    return jnp.concatenate([jnp.zeros((MLA_NOPE,), g.dtype), g[MLA_NOPE + half:], g[MLA_NOPE:MLA_NOPE + half]])


def _class_major(t, n_class):
    b, s = t.shape[:2]
    return jnp.swapaxes(t.reshape(b, s // n_class, n_class, *t.shape[2:]), 1, 2).reshape(t.shape)


def _block_ones(block, size=2 * LANES):
    idx = np.arange(size) // block
    return jnp.asarray(idx[:, None] == idx[None, :], BF16)


def kernel(x, c, positions, rel_bias, norm1_g, norm2_g, ada_w, ada_b, w_in, q_a_norm, w_q_b, kv_a_norm, w_kv_b,
           q_norm_a, k_norm_a, q_norm_b, k_norm_b, w_branch_a, w_branch_b, w_out, w_ffn_gate, w_ffn_up,
           w_ffn_down):
    depth = w_in.shape[0]
    b, s, d = x.shape
    x = _class_major(x, DIL_CLASSES)
    positions = _class_major(positions, DIL_CLASSES)

    mod = _ada_call(c, ada_w, ada_b).reshape(depth, b, 6, d)
    cos_t, sin_t = _rope_call(positions)
    dil_qk_bound = DIL_HEAD_DIM * jnp.max(jnp.abs(q_norm_b * (DIL_HEAD_DIM ** -0.5 * LOG2E))) * jnp.max(jnp.abs(k_norm_b))
    dil_shift = dil_qk_bound + jnp.max(rel_bias) * LOG2E
    dil_range = 2.0 * dil_qk_bound + (jnp.max(rel_bias) - jnp.min(rel_bias)) * LOG2E
    bias_tabs = _bias_call(rel_bias, jnp.zeros((), F32))
    bias_tabs_shifted = _bias_call(rel_bias, dil_shift)
    ones_mla = _block_ones(HEAD_PAD)
    ones_dil = _block_ones(DIL_HEAD_DIM)
    half = MLA_ROPE // 2
    lane2 = np.arange(2 * LANES) % LANES
    spare_qk = jnp.asarray(lane2 == MLA_QK, F32)
    spare_v = jnp.asarray(lane2 == MLA_V, F32)

    for l in range(depth):
        wl = w_in[l]
        k_rope_cols = jnp.pad(wl[:, _C_KR:_C_QKVB], ((0, 0), (MLA_NOPE, LANES - MLA_QK)))
        w_pre = jnp.concatenate([wl[:, _C_QC:_C_KR], k_rope_cols, wl[:, _C_QKVB:_C_GATE]], axis=1).astype(BF16)
        w_gates = wl[:, _C_GATE:].astype(BF16)
        wq3 = w_q_b[l].reshape(MLA_Q_LORA, MLA_HEADS, MLA_QK)
        lo, hi = wq3[..., MLA_NOPE:MLA_NOPE + half], wq3[..., MLA_NOPE + half:]
        wq_sw = jnp.concatenate([jnp.zeros_like(wq3[..., :MLA_NOPE]), hi, lo], axis=-1)
        wqq = jnp.concatenate([_pad_heads(w_q_b[l], MLA_QK),
                               _pad_heads(wq_sw.reshape(MLA_Q_LORA, -1), MLA_QK)], axis=1).astype(BF16)
        wkv3 = w_kv_b[l].reshape(MLA_KV_LORA, MLA_HEADS, MLA_NOPE + MLA_V)
        wkv = jnp.concatenate([_pad_heads(wkv3[:, :, :MLA_NOPE].reshape(MLA_KV_LORA, -1), MLA_NOPE),
                               _pad_heads(wkv3[:, :, MLA_NOPE:].reshape(MLA_KV_LORA, -1), MLA_V)], axis=1).astype(BF16)
        gq = q_norm_a[l] * (MLA_QK ** -0.5 * LOG2E)
        gk = k_norm_a[l]
        score_bound = MLA_QK * jnp.max(jnp.abs(gq)) * jnp.max(jnp.abs(gk))
        gains = jnp.stack([
            jnp.tile(_pad_lanes(gq), 2), jnp.tile(_pad_lanes(_swap_rope(gq)), 2),
            jnp.tile(_pad_lanes(gk), 2), jnp.tile(_pad_lanes(_swap_rope(gk)), 2),
            jnp.tile(q_norm_b[l] * (DIL_HEAD_DIM ** -0.5 * LOG2E), 4), jnp.tile(k_norm_b[l], 4),
            -score_bound * spare_qk, spare_qk, spare_v] + [jnp.zeros((2 * LANES,), F32)] * 7)
        wa = jnp.pad(w_branch_a[l].reshape(MLA_HEADS, MLA_V, d), ((0, 0), (0, HEAD_PAD - MLA_V), (0, 0)))
        wa = wa.reshape(MLA_HEADS * HEAD_PAD, d).astype(BF16)
        g1 = norm1_g[l].reshape(1, d)
        g2 = norm2_g[l].reshape(1, d)

        q, k, v, qkvd = _pre_call(x, mod[l], g1, cos_t, sin_t, w_pre, q_a_norm[l].reshape(1, -1), wqq,
                                        kv_a_norm[l].reshape(1, -1), wkv, gains, ones_mla, ones_dil, tm=512)
        ya = lax.cond(score_bound <= MLA_SAFE_BOUND,
                      functools.partial(_mla_call, tq=4096, tk=256, bounded=True),
                      functools.partial(_mla_call, tq=512, tk=1024, bounded=False), q, k, v)
        yb = lax.cond(dil_range <= DIL_SAFE_RANGE,
                      lambda a, shifted, plain: _dil_call(a, shifted, bounded=True),
                      lambda a, shifted, plain: _dil_call(a, plain, bounded=False),
                      qkvd, bias_tabs_shifted, bias_tabs)
        x = _post_call(x, ya, yb, mod[l], g1, w_gates, wa, w_branch_b[l].astype(BF16), w_out[l].astype(BF16), tm=512)
        x = _ffn_call(x, mod[l], g2, w_ffn_gate[l].astype(BF16), w_ffn_up[l].astype(BF16),
                      w_ffn_down[l].astype(BF16), tm=512)
    return _class_major(x, s // DIL_CLASSES)
```

```python
import functools
import math

import jax
import jax.numpy as jnp
import numpy as np
from jax import lax
from jax.experimental import pallas as pl
from jax.experimental.pallas import tpu as pltpu

F32 = jnp.float32
BF16 = jnp.bfloat16

LANES = 128
VMEM_LIMIT_BYTES = 56 * 1024 * 1024

D_MODEL = 1024
MLA_HEADS = 8
MLA_Q_LORA = 256
MLA_KV_LORA = 128
MLA_NOPE = 64
MLA_ROPE = 32
MLA_V = 64
MLA_QK = MLA_NOPE + MLA_ROPE
ROPE_THETA = 10000.0
DIL_HEADS = 8
DIL_HEAD_DIM = 64
DIL_PATTERNS = ((128, 1), (512, 4), (2048, 16))
DIL_HALF = 64
REL_BUCKETS = 32
REL_MAX_DIST = 1024
EPS = 1e-6
NEG_INF = -1e30
LOG2E = math.log2(math.e)
MLA_SAFE_BOUND = 60.0
MLA_SUB = 512

_C_QC = 0
_C_KVC = _C_QC + MLA_Q_LORA
_C_KR = _C_KVC + MLA_KV_LORA
_C_QKVB = _C_KR + MLA_ROPE
_C_GATE = _C_QKVB + 3 * DIL_HEADS * DIL_HEAD_DIM
PRE_COLS = MLA_Q_LORA + MLA_KV_LORA + LANES + 3 * DIL_HEADS * DIL_HEAD_DIM
HEAD_PAD = LANES

DIL_BQ = 2 * DIL_HALF
DIL_BK = 4 * DIL_HALF
DIL_CLASSES = 4
DIL_GROUP = 3 * LANES
DIL_SAFE_RANGE = 120.0


def _cparams(sem):
    return pltpu.CompilerParams(dimension_semantics=sem, vmem_limit_bytes=VMEM_LIMIT_BYTES)


def _resident(shape, index_map):
    return pl.BlockSpec(shape, index_map, pipeline_mode=pl.Buffered(1))


def _ada_kernel(c_ref, w_ref, b_ref, o_ref):
    c = c_ref[...]
    act = (c * jax.nn.sigmoid(c)).astype(BF16)
    o_ref[0] = jnp.dot(act, w_ref[0].astype(BF16), preferred_element_type=F32) + b_ref[0]


def _ada_call(c, ada_w, ada_b):
    depth, d, six_d = ada_w.shape
    b = c.shape[0]
    n_chunk = six_d // d
    return pl.pallas_call(
        _ada_kernel,
        grid=(depth, n_chunk),
        in_specs=[
            pl.BlockSpec((b, d), lambda l, j: (0, 0)),
            pl.BlockSpec((1, d, d), lambda l, j: (l, 0, j)),
            pl.BlockSpec((1, 1, d), lambda l, j: (l, 0, j)),
        ],
        out_specs=pl.BlockSpec((1, b, d), lambda l, j: (l, 0, j)),
        out_shape=jax.ShapeDtypeStruct((depth, b, six_d), F32),
        compiler_params=_cparams(("arbitrary", "arbitrary")),
        name="ada_mod",
    )(c, ada_w, ada_b.reshape(depth, 1, six_d))


def _rope_kernel(pos_ref, inv_ref, c_ref, s_ref):
    ang = pos_ref[0].astype(F32) * inv_ref[...]
    cosv = jnp.cos(ang)
    sinv = jnp.sin(ang)
    lane = lax.broadcasted_iota(jnp.int32, ang.shape, 1)
    half = MLA_ROPE // 2
    c_ref[0] = jnp.where(lane < MLA_NOPE, 1.0, jnp.where(lane < MLA_QK, cosv, 0.0))
    s_ref[0] = jnp.where(lane < MLA_NOPE, 0.0,
                         jnp.where(lane < MLA_NOPE + half, -sinv, jnp.where(lane < MLA_QK, sinv, 0.0)))


def _rope_call(positions):
    b, s = positions.shape
    half = MLA_ROPE // 2
    inv = ROPE_THETA ** (-jnp.arange(half, dtype=F32) / half)
    inv_lane = jnp.zeros((1, LANES), F32)
    inv_lane = inv_lane.at[0, MLA_NOPE:MLA_NOPE + half].set(inv).at[0, MLA_NOPE + half:MLA_QK].set(inv)
    ts = 512
    spec = pl.BlockSpec((1, ts, LANES), lambda i, j: (i, j, 0))
    return pl.pallas_call(
        _rope_kernel,
        grid=(b, s // ts),
        in_specs=[pl.BlockSpec((1, ts, 1), lambda i, j: (i, j, 0)),
                  pl.BlockSpec((1, LANES), lambda i, j: (0, 0))],
        out_specs=[spec, spec],
        out_shape=[jax.ShapeDtypeStruct((b, s, LANES), F32)] * 2,
        compiler_params=_cparams(("arbitrary", "arbitrary")),
        name="rope_tables",
    )(positions.reshape(b, s, 1), inv_lane)


def _t5_bucket(rel):
    nb = REL_BUCKETS // 2
    max_exact = nb // 2
    ret = jnp.where(rel > 0, nb, 0)
    n = jnp.abs(rel)
    nf = jnp.maximum(n, 1).astype(F32)
    large = max_exact + (jnp.log(nf / max_exact) / math.log(REL_MAX_DIST / max_exact) * (nb - max_exact)).astype(jnp.int32)
    large = jnp.minimum(large, nb - 1)
    return ret + jnp.where(n < max_exact, n, large)


def _bias_kernel(bucket_ref, rb_ref, shift_ref, o_ref):
    h = pl.program_id(1)
    bucket = bucket_ref[0]
    acc = jnp.full(bucket.shape, NEG_INF, F32)
    for bkt in range(REL_BUCKETS):
        acc = jnp.where(bucket == bkt, rb_ref[bkt, h], acc)
    o_ref[0, 0] = acc * LOG2E - shift_ref[0]


def _bias_call(rel_bias, shift):
    tabs = []
    for _, dil in DIL_PATTERNS:
        qi, kj = np.arange(DIL_BQ), np.arange(DIL_BK)
        if dil == 1:
            qi = DIL_CLASSES * (qi % (DIL_BQ // DIL_CLASSES)) + qi // (DIL_BQ // DIL_CLASSES)
            kj = DIL_CLASSES * (kj % (DIL_BK // DIL_CLASSES)) + kj // (DIL_BK // DIL_CLASSES)
        qi = jnp.asarray(qi, jnp.int32)[:, None]
        kj = jnp.asarray(kj, jnp.int32)[None, :]
        for variant in range(3):
            rel = kj - variant * DIL_HALF - qi
            tabs.append(jnp.where(jnp.abs(rel) <= DIL_HALF, _t5_bucket(rel * dil), -1))
    bucket = jnp.stack(tabs).astype(jnp.int32)
    n_tab = bucket.shape[0]
    return pl.pallas_call(
        _bias_kernel,
        grid=(n_tab, DIL_HEADS),
        in_specs=[pl.BlockSpec((1, DIL_BQ, DIL_BK), lambda t, h: (t, 0, 0)),
                  pl.BlockSpec(memory_space=pltpu.SMEM), pl.BlockSpec(memory_space=pltpu.SMEM)],
        out_specs=pl.BlockSpec((1, 1, DIL_BQ, DIL_BK), lambda t, h: (t, h, 0, 0)),
        out_shape=jax.ShapeDtypeStruct((n_tab, DIL_HEADS, DIL_BQ, DIL_BK), F32),
        compiler_params=_cparams(("arbitrary", "arbitrary")),
        name="bias_tables",
    )(bucket, rel_bias, jnp.reshape(shift, (1,)).astype(F32))


def _modulated_norm(x, gain, shift, scale):
    r = lax.rsqrt(jnp.mean(x * x, axis=-1, keepdims=True) + EPS)
    return ((x * r) * gain) * (1.0 + scale) + shift


PRE_SUB = 256


def _group_sumsq(t, ones_ref):
    return jnp.dot((t * t).astype(BF16), ones_ref[...], preferred_element_type=F32)


def _pre_kernel(x_ref, mod_ref, g1_ref, cos_ref, sin_ref, w_ref, qan_ref, wqq_ref, kvan_ref, wkv_ref,
                gains_ref, ones_mla_ref, ones_dil_ref,
                q_out, k_out, v_out, qkvd_out):
    tm = x_ref.shape[1]
    hp = MLA_HEADS * HEAD_PAD
    n_dil = DIL_HEADS * DIL_HEAD_DIM
    grp = 2 * LANES
    c0 = MLA_Q_LORA
    c1 = c0 + MLA_KV_LORA
    c2 = c1 + LANES
    half = MLA_ROPE // 2
    inv_qk = 1.0 / MLA_QK
    inv_dh = 1.0 / DIL_HEAD_DIM
    lane = lax.broadcasted_iota(jnp.int32, (PRE_SUB, LANES), 1)
    low_half = lane < MLA_NOPE + half

    for sub in range(tm // PRE_SUB):
        rs = slice(sub * PRE_SUB, (sub + 1) * PRE_SUB)
        x = x_ref[0, rs, :]
        h = _modulated_norm(x, g1_ref[...], mod_ref[0, 0:1, :], mod_ref[0, 1:2, :])
        proj = jnp.dot(h.astype(BF16), w_ref[...], preferred_element_type=F32)

        qc = proj[:, :c0]
        qcn = (qc * lax.rsqrt(jnp.mean(qc * qc, axis=-1, keepdims=True) + EPS)) * qan_ref[...]
        qq = jnp.dot(qcn.astype(BF16), wqq_ref[...], preferred_element_type=F32)
        kvc = proj[:, c0:c1]
        kvn = (kvc * lax.rsqrt(jnp.mean(kvc * kvc, axis=-1, keepdims=True) + EPS)) * kvan_ref[...]
        kv = jnp.dot(kvn.astype(BF16), wkv_ref[...], preferred_element_type=F32)
        k_rope = proj[:, c1:c2]
        k_rope_sw = jnp.where(low_half, pltpu.roll(k_rope, LANES - half, 1), pltpu.roll(k_rope, half, 1))

        cos2 = jnp.concatenate([cos_ref[0, rs, :]] * 2, axis=1)
        sin2 = jnp.concatenate([sin_ref[0, rs, :]] * 2, axis=1)
        aq = gains_ref[0:1, :] * cos2
        bq = gains_ref[1:2, :] * sin2
        ak = gains_ref[2:3, :] * cos2
        bk = gains_ref[3:4, :] * sin2
        kr2 = jnp.concatenate([k_rope] * 2, axis=1)
        krs2 = jnp.concatenate([k_rope_sw] * 2, axis=1)
        for g in range(hp // grp):
            sl = slice(g * grp, (g + 1) * grp)
            qh = qq[:, sl]
            rq = lax.rsqrt(_group_sumsq(qh, ones_mla_ref) * inv_qk + EPS)
            q_out[0, rs, sl] = (rq * (qh * aq + qq[:, hp + g * grp:hp + (g + 1) * grp] * bq)
                                + gains_ref[6:7, :]).astype(BF16)
            kh = kv[:, sl] + kr2
            rk = lax.rsqrt(_group_sumsq(kh, ones_mla_ref) * inv_qk + EPS)
            k_out[0, rs, sl] = (rk * (kh * ak + krs2 * bk) + gains_ref[7:8, :]).astype(BF16)
            v_out[0, rs, sl] = (kv[:, hp + g * grp:hp + (g + 1) * grp] + gains_ref[8:9, :]).astype(BF16)

        for g in range(n_dil // grp):
            qd = proj[:, c2 + g * grp:c2 + (g + 1) * grp]
            qd = ((qd * lax.rsqrt(_group_sumsq(qd, ones_dil_ref) * inv_dh + EPS)) * gains_ref[4:5, :]).astype(BF16)
            kd = proj[:, c2 + n_dil + g * grp:c2 + n_dil + (g + 1) * grp]
            kd = ((kd * lax.rsqrt(_group_sumsq(kd, ones_dil_ref) * inv_dh + EPS)) * gains_ref[5:6, :]).astype(BF16)
            vd = proj[:, c2 + 2 * n_dil + g * grp:c2 + 2 * n_dil + (g + 1) * grp].astype(BF16)
            for j in range(grp // LANES):
                base = (g * (grp // LANES) + j) * DIL_GROUP
                for t, val in enumerate((qd, kd, vd)):
                    qkvd_out[0, rs, base + t * LANES:base + (t + 1) * LANES] = val[:, j * LANES:(j + 1) * LANES]


def _pre_call(x, mod_l, g1, cos_t, sin_t, w_pre, qan, wqq, kvan, wkv, gains, ones_mla, ones_dil, tm):
    b, s, d = x.shape
    hp = MLA_HEADS * HEAD_PAD
    n_dil = DIL_HEADS * DIL_HEAD_DIM
    tok = lambda i, j: (i, j, 0)
    const2 = lambda i, j: (0, 0)
    row = lambda n: _resident((1, n), const2)
    return pl.pallas_call(
        _pre_kernel,
        grid=(b, s // tm),
        in_specs=[
            pl.BlockSpec((1, tm, d), tok),
            pl.BlockSpec((1, 6, d), lambda i, j: (i, 0, 0)),
            row(d),
            pl.BlockSpec((1, tm, LANES), tok),
            pl.BlockSpec((1, tm, LANES), tok),
            _resident((d, PRE_COLS), const2),
            row(MLA_Q_LORA),
            _resident((MLA_Q_LORA, 2 * hp), const2),
            row(MLA_KV_LORA),
            _resident((MLA_KV_LORA, 2 * hp), const2),
            _resident(gains.shape, const2),
            _resident(ones_mla.shape, const2),
            _resident(ones_dil.shape, const2),
        ],
        out_specs=[
            pl.BlockSpec((1, tm, hp), tok), pl.BlockSpec((1, tm, hp), tok), pl.BlockSpec((1, tm, hp), tok),
            pl.BlockSpec((1, tm, 3 * n_dil), tok),
        ],
        out_shape=[jax.ShapeDtypeStruct((b, s, hp), BF16)] * 3 + [jax.ShapeDtypeStruct((b, s, 3 * n_dil), BF16)],
        compiler_params=_cparams(("arbitrary", "arbitrary")),
        name="pre_proj",
    )(x, mod_l, g1, cos_t, sin_t, w_pre, qan, wqq, kvan, wkv, gains, ones_mla, ones_dil)


def _pair_output(outs):
    o0, o1 = outs
    lane = lax.broadcasted_iota(jnp.int32, o0.shape, 1)
    return jnp.where(lane < MLA_V, o0 / o0[:, MLA_V:MLA_V + 1], o1 / o1[:, 0:1]).astype(BF16)


def _mla_online_kernel(q_ref, k_ref, v_ref, o_ref, *, tk):
    tq = q_ref.shape[1]
    n_k = k_ref.shape[1] // tk
    outs = []
    for hh in range(2):
        hl = slice(hh * HEAD_PAD, (hh + 1) * HEAD_PAD)
        q = q_ref[0, :, hl]

        def body(j, carry, q=q, hl=hl):
            m, acc = carry
            ks = pl.multiple_of(j * tk, tk)
            k = k_ref[0, pl.ds(ks, tk), hl]
            v = v_ref[0, pl.ds(ks, tk), hl]
            s = lax.dot_general(q, k, (((1,), (1,)), ((), ())), preferred_element_type=F32)
            m_new = jnp.maximum(m, jnp.max(s, axis=-1, keepdims=True))
            p = jnp.exp2(s - m_new)
            acc = jnp.exp2(m - m_new) * acc + jnp.dot(p.astype(BF16), v, preferred_element_type=F32)
            return m_new, acc

        init = (jnp.full((tq, 1), -jnp.inf, F32), jnp.zeros((tq, HEAD_PAD), F32))
        outs.append(lax.fori_loop(0, n_k, body, init, unroll=True)[1])
    o_ref[0] = _pair_output(outs)


def _mla_bounded_kernel(q_ref, k_ref, v_ref, o_ref, *, tk):
    n_k = k_ref.shape[1] // tk
    for qs in range(q_ref.shape[1] // MLA_SUB):
        rows = slice(qs * MLA_SUB, (qs + 1) * MLA_SUB)
        outs = []
        for hh in range(2):
            hl = slice(hh * HEAD_PAD, (hh + 1) * HEAD_PAD)
            q = q_ref[0, rows, hl]
            acc = jnp.zeros((MLA_SUB, HEAD_PAD), F32)
            for j in range(n_k):
                k = k_ref[0, j * tk:(j + 1) * tk, hl]
                v = v_ref[0, j * tk:(j + 1) * tk, hl]
                s = lax.dot_general(q, k, (((1,), (1,)), ((), ())), preferred_element_type=F32)
                acc = acc + jnp.dot(jnp.exp2(s).astype(BF16), v, preferred_element_type=F32)
            outs.append(acc)
        o_ref[0, rows, :] = _pair_output(outs)


def _mla_call(q, k, v, tq, tk, bounded):
    b, s, width = q.shape
    pair = 2 * HEAD_PAD
    n_pair = width // pair
    return pl.pallas_call(
        functools.partial(_mla_bounded_kernel if bounded else _mla_online_kernel, tk=tk),
        grid=(b, n_pair, s // tq),
        in_specs=[
            pl.BlockSpec((1, tq, pair), lambda i, h, j: (i, j, h)),
            pl.BlockSpec((1, s, pair), lambda i, h, j: (i, 0, h)),
            pl.BlockSpec((1, s, pair), lambda i, h, j: (i, 0, h)),
        ],
        out_specs=pl.BlockSpec((1, tq, 2 * MLA_V), lambda i, h, j: (i, j, h)),
        out_shape=jax.ShapeDtypeStruct((b, s, n_pair * 2 * MLA_V), BF16),
        compiler_params=_cparams(("arbitrary", "arbitrary", "arbitrary")),
        name="mla_attn_bounded" if bounded else "mla_attn_online",
    )(q, k, v)


def _dil_kernel(x_ref, bias_ref, o_ref, qf, kf, vf, *states, bounded):
    s_len = o_ref.shape[1]
    cls_len = s_len // DIL_CLASSES
    piece_q = DIL_BQ // DIL_CLASSES
    piece_k = DIL_BK // DIL_CLASSES
    lanes = (slice(0, LANES), slice(LANES, 2 * LANES), slice(2 * LANES, 3 * LANES))
    lane = lax.broadcasted_iota(jnp.int32, (DIL_BQ, LANES), 1)
    first = lane < DIL_HEAD_DIM
    for dst, sl in zip((qf, kf, vf), lanes):
        dst[...] = x_ref[0, :, sl].astype(F32)

    def band_tile(q_t, k_t, v_t, tab):
        zero = jnp.zeros_like(q_t)
        qm = jnp.concatenate([jnp.where(first, q_t, zero), jnp.where(first, zero, q_t)], axis=0)
        s = lax.dot_general(qm, k_t, (((1,), (1,)), ((), ())), preferred_element_type=F32)
        s = s + bias_ref[tab].reshape(2 * DIL_BQ, DIL_BK)
        if not bounded:
            m = jnp.max(s, axis=-1, keepdims=True)
            s = s - m
        p = jnp.exp2(s)
        l = jnp.sum(p, axis=-1, keepdims=True)
        pv = jnp.dot(p.astype(BF16), v_t, preferred_element_type=F32)
        out = (jnp.where(first, l[:DIL_BQ], l[DIL_BQ:]), jnp.where(first, pv[:DIL_BQ], pv[DIL_BQ:]))
        return out if bounded else (jnp.where(first, m[:DIL_BQ], m[DIL_BQ:]),) + out

    def merge(old, new):
        if bounded:
            return tuple(o + n for o, n in zip(old, new))
        m_n = jnp.maximum(old[0], new[0])
        a = jnp.exp2(old[0] - m_n)
        c = jnp.exp2(new[0] - m_n)
        return (m_n, a * old[1] + c * new[1], a * old[2] + c * new[2])

    def variant_of(n, n_blk):
        return 0 if n == 0 else (2 if n == n_blk - 1 else 1)

    p_idx = 1
    n_blk = cls_len // DIL_BQ
    for r in range(DIL_CLASSES):
        for n in range(n_blk):
            i0 = r * cls_len + n * DIL_BQ
            ks = r * cls_len + min(max(n * DIL_BQ - DIL_HALF, 0), cls_len - DIL_BK)
            vals = band_tile(x_ref[0, i0:i0 + DIL_BQ, lanes[0]], x_ref[0, ks:ks + DIL_BK, lanes[1]],
                             x_ref[0, ks:ks + DIL_BK, lanes[2]], 3 * p_idx + variant_of(n, n_blk))
            for st, val in zip(states, vals):
                st[i0:i0 + DIL_BQ, :] = val

    p_idx = 0
    n_blk = s_len // DIL_BQ
    for n in range(n_blk):
        i0 = n * DIL_BQ // DIL_CLASSES
        ks = min(max(n * DIL_BQ - DIL_HALF, 0), s_len - DIL_BK) // DIL_CLASSES
        rows_q = [slice(r * cls_len + i0, r * cls_len + i0 + piece_q) for r in range(DIL_CLASSES)]
        rows_k = [slice(r * cls_len + ks, r * cls_len + ks + piece_k) for r in range(DIL_CLASSES)]
        vals = band_tile(jnp.concatenate([x_ref[0, rq, lanes[0]] for rq in rows_q], axis=0),
                         jnp.concatenate([x_ref[0, rk, lanes[1]] for rk in rows_k], axis=0),
                         jnp.concatenate([x_ref[0, rk, lanes[2]] for rk in rows_k], axis=0),
                         3 * p_idx + variant_of(n, n_blk))
        for r, rq in enumerate(rows_q):
            piece = slice(r * piece_q, (r + 1) * piece_q)
            merged = merge(tuple(st[rq, :] for st in states), tuple(val[piece] for val in vals))
            for st, val in zip(states, merged):
                st[rq, :] = val

    p_idx = 2
    ratio = DIL_PATTERNS[p_idx][1] // DIL_CLASSES
    sub_len = cls_len // ratio
    n_blk = sub_len // DIL_BQ
    for r in range(DIL_CLASSES):
        for c in range(ratio):
            for n in range(n_blk):
                ks = min(max(n * DIL_BQ - DIL_HALF, 0), sub_len - DIL_BK)
                rows_q = pl.ds(r * cls_len + c + ratio * n * DIL_BQ, DIL_BQ, stride=ratio)
                rows_k = pl.ds(r * cls_len + c + ratio * ks, DIL_BK, stride=ratio)
                vals = band_tile(qf[rows_q, :].astype(BF16), kf[rows_k, :].astype(BF16),
                                 vf[rows_k, :].astype(BF16), 3 * p_idx + variant_of(n, n_blk))
                merged = merge(tuple(st[rows_q, :] for st in states), vals)
                for st, val in zip(states, merged):
                    st[rows_q, :] = val

    o_ref[0] = (states[-1][...] / states[-2][...]).astype(BF16)


def _dil_call(qkvd, bias_tabs, bounded):
    b, s, width = qkvd.shape
    n_pair = width // DIL_GROUP
    n_tab = bias_tabs.shape[0]
    n_state = 2 if bounded else 3
    return pl.pallas_call(
        functools.partial(_dil_kernel, bounded=bounded),
        grid=(n_pair, b),
        in_specs=[pl.BlockSpec((1, s, DIL_GROUP), lambda hp, i: (i, 0, hp)),
                  pl.BlockSpec((n_tab, 2, DIL_BQ, DIL_BK), lambda hp, i: (0, hp, 0, 0))],
        out_specs=pl.BlockSpec((1, s, LANES), lambda hp, i: (i, 0, hp)),
        out_shape=jax.ShapeDtypeStruct((b, s, n_pair * LANES), BF16),
        scratch_shapes=[pltpu.VMEM((s, LANES), F32)] * (3 + n_state),
        compiler_params=_cparams(("arbitrary", "arbitrary")),
        name="dil_attn_bounded" if bounded else "dil_attn_online",
    )(qkvd, bias_tabs)


def _post_kernel(x_ref, ya_ref, yb_ref, mod_ref, g1_ref, wg_ref, wa_ref, wb_ref, wo_ref, o_ref):
    x = x_ref[0]
    h = _modulated_norm(x, g1_ref[...], mod_ref[0, 0:1, :], mod_ref[0, 1:2, :])
    gates = jnp.dot(h.astype(BF16), wg_ref[...], preferred_element_type=F32)
    d = x.shape[-1]
    a = jnp.dot(ya_ref[0], wa_ref[...], preferred_element_type=F32)
    bb = jnp.dot(yb_ref[0], wb_ref[...], preferred_element_type=F32)
    merged = jax.nn.sigmoid(gates[:, :d]) * a + jax.nn.sigmoid(gates[:, d:]) * bb
    upd = jnp.dot(merged.astype(BF16), wo_ref[...], preferred_element_type=F32)
    o_ref[0] = x + mod_ref[0, 2:3, :] * upd


def _post_call(x, ya, yb, mod_l, g1, wg, wa, wb, wo, tm):
    b, s, d = x.shape
    tok = lambda i, j: (i, j, 0)
    const2 = lambda i, j: (0, 0)
    return pl.pallas_call(
        _post_kernel,
        grid=(b, s // tm),
        in_specs=[
            pl.BlockSpec((1, tm, d), tok),
            pl.BlockSpec((1, tm, ya.shape[-1]), tok),
            pl.BlockSpec((1, tm, yb.shape[-1]), tok),
            pl.BlockSpec((1, 6, d), lambda i, j: (i, 0, 0)),
            _resident((1, d), const2),
            _resident(wg.shape, const2),
            _resident(wa.shape, const2),
            _resident(wb.shape, const2),
            _resident(wo.shape, const2),
        ],
        out_specs=pl.BlockSpec((1, tm, d), tok),
        out_shape=jax.ShapeDtypeStruct(x.shape, F32),
        compiler_params=_cparams(("arbitrary", "arbitrary")),
        name="attn_out",
    )(x, ya, yb, mod_l, g1, wg, wa, wb, wo)


def _ffn_kernel(x_ref, mod_ref, g2_ref, wg_ref, wu_ref, wd_ref, o_ref):
    x = x_ref[0]
    h = _modulated_norm(x, g2_ref[...], mod_ref[0, 3:4, :], mod_ref[0, 4:5, :]).astype(BF16)
    g = jnp.dot(h, wg_ref[...], preferred_element_type=F32)
    u = jnp.dot(h, wu_ref[...], preferred_element_type=F32)
    act = (g * jax.nn.sigmoid(g)) * u
    upd = jnp.dot(act.astype(BF16), wd_ref[...], preferred_element_type=F32)
    o_ref[0] = x + mod_ref[0, 5:6, :] * upd


def _ffn_call(x, mod_l, g2, wg, wu, wd, tm):
    b, s, d = x.shape
    tok = lambda i, j: (i, j, 0)
    const2 = lambda i, j: (0, 0)
    return pl.pallas_call(
        _ffn_kernel,
        grid=(b, s // tm),
        in_specs=[
            pl.BlockSpec((1, tm, d), tok),
            pl.BlockSpec((1, 6, d), lambda i, j: (i, 0, 0)),
            _resident((1, d), const2),
            _resident(wg.shape, const2),
            _resident(wu.shape, const2),
            _resident(wd.shape, const2),
        ],
        out_specs=pl.BlockSpec((1, tm, d), tok),
        out_shape=jax.ShapeDtypeStruct(x.shape, F32),
        compiler_params=_cparams(("arbitrary", "arbitrary")),
        name="swiglu",
    )(x, mod_l, g2, wg, wu, wd)


def _pad_heads(w, width):
    rows = w.shape[0]
    w = w.reshape(rows, MLA_HEADS, width)
    return jnp.pad(w, ((0, 0), (0, 0), (0, HEAD_PAD - width))).reshape(rows, MLA_HEADS * HEAD_PAD)


def _pad_heads_alternating(w):
    rows = w.shape[0]
    w = w.reshape(rows, MLA_HEADS // 2, 2, MLA_V)
    zero = jnp.zeros_like(w[:, :, 0])
    even = jnp.concatenate([w[:, :, 0], zero], axis=-1)
    odd = jnp.concatenate([zero, w[:, :, 1]], axis=-1)
    return jnp.stack([even, odd], axis=2).reshape(rows, MLA_HEADS * HEAD_PAD)


def _pad_lanes(g):
    return jnp.pad(g, (0, LANES - g.shape[0]))


def _swap_rope(g):
    half = MLA_ROPE // 2
    return jnp.concatenate([jnp.zeros((MLA_NOPE,), g.dtype), g[MLA_NOPE + half:], g[MLA_NOPE:MLA_NOPE + half]])


def _class_major(t, n_class):
    b, s = t.shape[:2]
    return jnp.swapaxes(t.reshape(b, s // n_class, n_class, *t.shape[2:]), 1, 2).reshape(t.shape)


def _block_ones(block, size=2 * LANES):
    idx = np.arange(size) // block
    return jnp.asarray(idx[:, None] == idx[None, :], BF16)


def kernel(x, c, positions, rel_bias, norm1_g, norm2_g, ada_w, ada_b, w_in, q_a_norm, w_q_b, kv_a_norm, w_kv_b,
           q_norm_a, k_norm_a, q_norm_b, k_norm_b, w_branch_a, w_branch_b, w_out, w_ffn_gate, w_ffn_up,
           w_ffn_down):
    depth = w_in.shape[0]
    b, s, d = x.shape
    x = _class_major(x, DIL_CLASSES)
    positions = _class_major(positions, DIL_CLASSES)

    mod = _ada_call(c, ada_w, ada_b).reshape(depth, b, 6, d)
    cos_t, sin_t = _rope_call(positions)
    dil_qk_bound = DIL_HEAD_DIM * jnp.max(jnp.abs(q_norm_b * (DIL_HEAD_DIM ** -0.5 * LOG2E))) * jnp.max(jnp.abs(k_norm_b))
    dil_shift = dil_qk_bound + jnp.max(rel_bias) * LOG2E
    dil_range = 2.0 * dil_qk_bound + (jnp.max(rel_bias) - jnp.min(rel_bias)) * LOG2E
    bias_tabs = _bias_call(rel_bias, jnp.zeros((), F32))
    bias_tabs_shifted = _bias_call(rel_bias, dil_shift)
    ones_mla = _block_ones(HEAD_PAD)
    ones_dil = _block_ones(DIL_HEAD_DIM)
    half = MLA_ROPE // 2
    lane2 = np.arange(2 * LANES) % LANES
    spare_qk = jnp.asarray(lane2 == MLA_QK, F32)
    spare_v = jnp.asarray((np.arange(2 * LANES) == MLA_V) | (np.arange(2 * LANES) == LANES), F32)

    for l in range(depth):
        wl = w_in[l]
        k_rope_cols = jnp.pad(wl[:, _C_KR:_C_QKVB], ((0, 0), (MLA_NOPE, LANES - MLA_QK)))
        w_pre = jnp.concatenate([wl[:, _C_QC:_C_KR], k_rope_cols, wl[:, _C_QKVB:_C_GATE]], axis=1).astype(BF16)
        w_gates = wl[:, _C_GATE:].astype(BF16)
        wq3 = w_q_b[l].reshape(MLA_Q_LORA, MLA_HEADS, MLA_QK)
        lo, hi = wq3[..., MLA_NOPE:MLA_NOPE + half], wq3[..., MLA_NOPE + half:]
        wq_sw = jnp.concatenate([jnp.zeros_like(wq3[..., :MLA_NOPE]), hi, lo], axis=-1)
        wqq = jnp.concatenate([_pad_heads(w_q_b[l], MLA_QK),
                               _pad_heads(wq_sw.reshape(MLA_Q_LORA, -1), MLA_QK)], axis=1).astype(BF16)
        wkv3 = w_kv_b[l].reshape(MLA_KV_LORA, MLA_HEADS, MLA_NOPE + MLA_V)
        wkv = jnp.concatenate([_pad_heads(wkv3[:, :, :MLA_NOPE].reshape(MLA_KV_LORA, -1), MLA_NOPE),
                               _pad_heads_alternating(wkv3[:, :, MLA_NOPE:])], axis=1).astype(BF16)
        gq = q_norm_a[l] * (MLA_QK ** -0.5 * LOG2E)
        gk = k_norm_a[l]
        score_bound = MLA_QK * jnp.max(jnp.abs(gq)) * jnp.max(jnp.abs(gk))
        gains = jnp.stack([
            jnp.tile(_pad_lanes(gq), 2), jnp.tile(_pad_lanes(_swap_rope(gq)), 2),
            jnp.tile(_pad_lanes(gk), 2), jnp.tile(_pad_lanes(_swap_rope(gk)), 2),
            jnp.tile(q_norm_b[l] * (DIL_HEAD_DIM ** -0.5 * LOG2E), 4), jnp.tile(k_norm_b[l], 4),
            -score_bound * spare_qk, spare_qk, spare_v] + [jnp.zeros((2 * LANES,), F32)] * 7)
        g1 = norm1_g[l].reshape(1, d)
        g2 = norm2_g[l].reshape(1, d)

        q, k, v, qkvd = _pre_call(x, mod[l], g1, cos_t, sin_t, w_pre, q_a_norm[l].reshape(1, -1), wqq,
                                        kv_a_norm[l].reshape(1, -1), wkv, gains, ones_mla, ones_dil, tm=512)
        ya = lax.cond(score_bound <= MLA_SAFE_BOUND,
                      functools.partial(_mla_call, tq=2048, tk=256, bounded=True),
                      functools.partial(_mla_call, tq=512, tk=1024, bounded=False), q, k, v)
        yb = lax.cond(dil_range <= DIL_SAFE_RANGE,
                      lambda a, shifted, plain: _dil_call(a, shifted, bounded=True),
                      lambda a, shifted, plain: _dil_call(a, plain, bounded=False),
                      qkvd, bias_tabs_shifted, bias_tabs)
        x = _post_call(x, ya, yb, mod[l], g1, w_gates, w_branch_a[l].astype(BF16), w_branch_b[l].astype(BF16),
                       w_out[l].astype(BF16), tm=512)
        x = _ffn_call(x, mod[l], g2, w_ffn_gate[l].astype(BF16), w_ffn_up[l].astype(BF16),
                      w_ffn_down[l].astype(BF16), tm=512)
    return _class_major(x, s // DIL_CLASSES)
```

```python
import functools
import math

import jax
import jax.numpy as jnp
import numpy as np
from jax import lax
from jax.experimental import pallas as pl
from jax.experimental.pallas import tpu as pltpu

F32 = jnp.float32
BF16 = jnp.bfloat16

LANES = 128
VMEM_LIMIT_BYTES = 56 * 1024 * 1024

D_MODEL = 1024
MLA_HEADS = 8
MLA_Q_LORA = 256
MLA_KV_LORA = 128
MLA_NOPE = 64
MLA_ROPE = 32
MLA_V = 64
MLA_QK = MLA_NOPE + MLA_ROPE
ROPE_THETA = 10000.0
DIL_HEADS = 8
DIL_HEAD_DIM = 64
DIL_PATTERNS = ((128, 1), (512, 4), (2048, 16))
DIL_HALF = 64
REL_BUCKETS = 32
REL_MAX_DIST = 1024
EPS = 1e-6
NEG_INF = -1e30
LOG2E = math.log2(math.e)
MLA_SAFE_BOUND = 60.0
MLA_SUB = 512

_C_QC = 0
_C_KVC = _C_QC + MLA_Q_LORA
_C_KR = _C_KVC + MLA_KV_LORA
_C_QKVB = _C_KR + MLA_ROPE
_C_GATE = _C_QKVB + 3 * DIL_HEADS * DIL_HEAD_DIM
PRE_COLS = MLA_Q_LORA + MLA_KV_LORA + LANES + 3 * DIL_HEADS * DIL_HEAD_DIM
HEAD_PAD = LANES

DIL_BQ = 2 * DIL_HALF
DIL_BK = 4 * DIL_HALF
DIL_CLASSES = 4
DIL_GROUP = 3 * LANES
DIL_SAFE_RANGE = 120.0


def _cparams(sem):
    return pltpu.CompilerParams(dimension_semantics=sem, vmem_limit_bytes=VMEM_LIMIT_BYTES)


def _resident(shape, index_map):
    return pl.BlockSpec(shape, index_map, pipeline_mode=pl.Buffered(1))


def _ada_kernel(c_ref, w_ref, b_ref, o_ref):
    c = c_ref[...]
    act = (c * jax.nn.sigmoid(c)).astype(BF16)
    o_ref[0] = jnp.dot(act, w_ref[0].astype(BF16), preferred_element_type=F32) + b_ref[0]


def _ada_call(c, ada_w, ada_b):
    depth, d, six_d = ada_w.shape
    b = c.shape[0]
    n_chunk = six_d // d
    return pl.pallas_call(
        _ada_kernel,
        grid=(depth, n_chunk),
        in_specs=[
            pl.BlockSpec((b, d), lambda l, j: (0, 0)),
            pl.BlockSpec((1, d, d), lambda l, j: (l, 0, j)),
            pl.BlockSpec((1, 1, d), lambda l, j: (l, 0, j)),
        ],
        out_specs=pl.BlockSpec((1, b, d), lambda l, j: (l, 0, j)),
        out_shape=jax.ShapeDtypeStruct((depth, b, six_d), F32),
        compiler_params=_cparams(("arbitrary", "arbitrary")),
        name="ada_mod",
    )(c, ada_w, ada_b.reshape(depth, 1, six_d))


def _rope_kernel(pos_ref, inv_ref, c_ref, s_ref):
    ang = pos_ref[0].astype(F32) * inv_ref[...]
    cosv = jnp.cos(ang)
    sinv = jnp.sin(ang)
    lane = lax.broadcasted_iota(jnp.int32, ang.shape, 1)
    half = MLA_ROPE // 2
    c_ref[0] = jnp.where(lane < MLA_NOPE, 1.0, jnp.where(lane < MLA_QK, cosv, 0.0))
    s_ref[0] = jnp.where(lane < MLA_NOPE, 0.0,
                         jnp.where(lane < MLA_NOPE + half, -sinv, jnp.where(lane < MLA_QK, sinv, 0.0)))


def _rope_call(positions):
    b, s = positions.shape
    half = MLA_ROPE // 2
    inv = ROPE_THETA ** (-jnp.arange(half, dtype=F32) / half)
    inv_lane = jnp.zeros((1, LANES), F32)
    inv_lane = inv_lane.at[0, MLA_NOPE:MLA_NOPE + half].set(inv).at[0, MLA_NOPE + half:MLA_QK].set(inv)
    ts = 512
    spec = pl.BlockSpec((1, ts, LANES), lambda i, j: (i, j, 0))
    return pl.pallas_call(
        _rope_kernel,
        grid=(b, s // ts),
        in_specs=[pl.BlockSpec((1, ts, 1), lambda i, j: (i, j, 0)),
                  pl.BlockSpec((1, LANES), lambda i, j: (0, 0))],
        out_specs=[spec, spec],
        out_shape=[jax.ShapeDtypeStruct((b, s, LANES), F32)] * 2,
        compiler_params=_cparams(("arbitrary", "arbitrary")),
        name="rope_tables",
    )(positions.reshape(b, s, 1), inv_lane)


def _t5_bucket(rel):
    nb = REL_BUCKETS // 2
    max_exact = nb // 2
    ret = jnp.where(rel > 0, nb, 0)
    n = jnp.abs(rel)
    nf = jnp.maximum(n, 1).astype(F32)
    large = max_exact + (jnp.log(nf / max_exact) / math.log(REL_MAX_DIST / max_exact) * (nb - max_exact)).astype(jnp.int32)
    large = jnp.minimum(large, nb - 1)
    return ret + jnp.where(n < max_exact, n, large)


def _bias_kernel(bucket_ref, rb_ref, shift_ref, o_ref):
    h = pl.program_id(1)
    bucket = bucket_ref[0]
    acc = jnp.full(bucket.shape, NEG_INF, F32)
    for bkt in range(REL_BUCKETS):
        acc = jnp.where(bucket == bkt, rb_ref[bkt, h], acc)
    o_ref[0, 0] = acc * LOG2E - shift_ref[0]


def _bias_call(rel_bias, shift):
    tabs = []
    for _, dil in DIL_PATTERNS:
        qi, kj = np.arange(DIL_BQ), np.arange(DIL_BK)
        if dil == 1:
            qi = DIL_CLASSES * (qi % (DIL_BQ // DIL_CLASSES)) + qi // (DIL_BQ // DIL_CLASSES)
            kj = DIL_CLASSES * (kj % (DIL_BK // DIL_CLASSES)) + kj // (DIL_BK // DIL_CLASSES)
        qi = jnp.asarray(qi, jnp.int32)[:, None]
        kj = jnp.asarray(kj, jnp.int32)[None, :]
        for variant in range(3):
            rel = kj - variant * DIL_HALF - qi
            tabs.append(jnp.where(jnp.abs(rel) <= DIL_HALF, _t5_bucket(rel * dil), -1))
    bucket = jnp.stack(tabs).astype(jnp.int32)
    n_tab = bucket.shape[0]
    return pl.pallas_call(
        _bias_kernel,
        grid=(n_tab, DIL_HEADS),
        in_specs=[pl.BlockSpec((1, DIL_BQ, DIL_BK), lambda t, h: (t, 0, 0)),
                  pl.BlockSpec(memory_space=pltpu.SMEM), pl.BlockSpec(memory_space=pltpu.SMEM)],
        out_specs=pl.BlockSpec((1, 1, DIL_BQ, DIL_BK), lambda t, h: (t, h, 0, 0)),
        out_shape=jax.ShapeDtypeStruct((n_tab, DIL_HEADS, DIL_BQ, DIL_BK), F32),
        compiler_params=_cparams(("arbitrary", "arbitrary")),
        name="bias_tables",
    )(bucket, rel_bias, jnp.reshape(shift, (1,)).astype(F32))


def _modulated_norm(x, gain, shift, scale):
    r = lax.rsqrt(jnp.mean(x * x, axis=-1, keepdims=True) + EPS)
    return ((x * r) * gain) * (1.0 + scale) + shift


PRE_SUB = 256


def _group_sumsq(t, ones_ref):
    return jnp.dot((t * t).astype(BF16), ones_ref[...], preferred_element_type=F32)


def _class_slabs_load(slab_ref, tile, cls):
    rows = tile // DIL_CLASSES
    return jnp.concatenate([slab_ref[c, pl.ds(cls, rows, stride=DIL_CLASSES), :] for c in range(slab_ref.shape[0])],
                           axis=1)


def _pre_kernel(x_ref, mod_ref, g1_ref, cos_ref, sin_ref, w_ref, qan_ref, wqq_ref, kvan_ref, wkv_ref,
                gains_ref, ones_mla_ref, ones_dil_ref, *rest, natural_in):
    if natural_in:
        q_out, k_out, v_out, qkvd_out, xp_out, slab_ref = rest
        tm = x_ref.shape[1]
        for c in range(slab_ref.shape[0]):
            slab_ref[c] = x_ref[0, :, c * LANES:(c + 1) * LANES]
    else:
        q_out, k_out, v_out, qkvd_out = rest
        tm = x_ref.shape[1] * x_ref.shape[2]
    cls_rows = tm // DIL_CLASSES
    cls_per_sub = PRE_SUB // cls_rows

    def class_rows(ref, sub):
        return jnp.concatenate([ref[0, sub * cls_per_sub + j] for j in range(cls_per_sub)], axis=0)

    def store(ref, sub, sl, val):
        for j in range(cls_per_sub):
            ref[0, sub * cls_per_sub + j, :, sl] = val[j * cls_rows:(j + 1) * cls_rows]

    hp = MLA_HEADS * HEAD_PAD
    n_dil = DIL_HEADS * DIL_HEAD_DIM
    grp = 2 * LANES
    c0 = MLA_Q_LORA
    c1 = c0 + MLA_KV_LORA
    c2 = c1 + LANES
    half = MLA_ROPE // 2
    inv_qk = 1.0 / MLA_QK
    inv_dh = 1.0 / DIL_HEAD_DIM
    lane = lax.broadcasted_iota(jnp.int32, (PRE_SUB, LANES), 1)
    low_half = lane < MLA_NOPE + half

    for sub in range(tm // PRE_SUB):
        if natural_in:
            pieces = [_class_slabs_load(slab_ref, tm, sub * cls_per_sub + j) for j in range(cls_per_sub)]
            for j, piece in enumerate(pieces):
                xp_out[0, sub * cls_per_sub + j] = piece
            x = jnp.concatenate(pieces, axis=0)
        else:
            x = class_rows(x_ref, sub)
        h = _modulated_norm(x, g1_ref[...], mod_ref[0, 0:1, :], mod_ref[0, 1:2, :])
        proj = jnp.dot(h.astype(BF16), w_ref[...], preferred_element_type=F32)

        qc = proj[:, :c0]
        qcn = (qc * lax.rsqrt(jnp.mean(qc * qc, axis=-1, keepdims=True) + EPS)) * qan_ref[...]
        qq = jnp.dot(qcn.astype(BF16), wqq_ref[...], preferred_element_type=F32)
        kvc = proj[:, c0:c1]
        kvn = (kvc * lax.rsqrt(jnp.mean(kvc * kvc, axis=-1, keepdims=True) + EPS)) * kvan_ref[...]
        kv = jnp.dot(kvn.astype(BF16), wkv_ref[...], preferred_element_type=F32)
        k_rope = proj[:, c1:c2]
        k_rope_sw = jnp.where(low_half, pltpu.roll(k_rope, LANES - half, 1), pltpu.roll(k_rope, half, 1))

        cos2 = jnp.concatenate([class_rows(cos_ref, sub)] * 2, axis=1)
        sin2 = jnp.concatenate([class_rows(sin_ref, sub)] * 2, axis=1)
        aq = gains_ref[0:1, :] * cos2
        bq = gains_ref[1:2, :] * sin2
        ak = gains_ref[2:3, :] * cos2
        bk = gains_ref[3:4, :] * sin2
        kr2 = jnp.concatenate([k_rope] * 2, axis=1)
        krs2 = jnp.concatenate([k_rope_sw] * 2, axis=1)
        for g in range(hp // grp):
            sl = slice(g * grp, (g + 1) * grp)
            qh = qq[:, sl]
            rq = lax.rsqrt(_group_sumsq(qh, ones_mla_ref) * inv_qk + EPS)
            store(q_out, sub, sl, (rq * (qh * aq + qq[:, hp + g * grp:hp + (g + 1) * grp] * bq)
                                   + gains_ref[6:7, :]).astype(BF16))
            kh = kv[:, sl] + kr2
            rk = lax.rsqrt(_group_sumsq(kh, ones_mla_ref) * inv_qk + EPS)
            store(k_out, sub, sl, (rk * (kh * ak + krs2 * bk) + gains_ref[7:8, :]).astype(BF16))
            store(v_out, sub, sl, (kv[:, hp + g * grp:hp + (g + 1) * grp] + gains_ref[8:9, :]).astype(BF16))

        for g in range(n_dil // grp):
            qd = proj[:, c2 + g * grp:c2 + (g + 1) * grp]
            qd = ((qd * lax.rsqrt(_group_sumsq(qd, ones_dil_ref) * inv_dh + EPS)) * gains_ref[4:5, :]).astype(BF16)
            kd = proj[:, c2 + n_dil + g * grp:c2 + n_dil + (g + 1) * grp]
            kd = ((kd * lax.rsqrt(_group_sumsq(kd, ones_dil_ref) * inv_dh + EPS)) * gains_ref[5:6, :]).astype(BF16)
            vd = proj[:, c2 + 2 * n_dil + g * grp:c2 + 2 * n_dil + (g + 1) * grp].astype(BF16)
            for j in range(grp // LANES):
                base = (g * (grp // LANES) + j) * DIL_GROUP
                for t, val in enumerate((qd, kd, vd)):
                    store(qkvd_out, sub, slice(base + t * LANES, base + (t + 1) * LANES), val[:, j * LANES:(j + 1) * LANES])


def _pre_call(x, mod_l, g1, cos_t, sin_t, w_pre, qan, wqq, kvan, wkv, gains, ones_mla, ones_dil, tm, natural_in):
    b, s, d = x.shape
    hp = MLA_HEADS * HEAD_PAD
    n_dil = DIL_HEADS * DIL_HEAD_DIM
    cls_len = s // DIL_CLASSES
    cls_rows = tm // DIL_CLASSES
    cls_view = lambda t: t.reshape(b, DIL_CLASSES, cls_len, t.shape[-1])
    cls_spec = lambda cols: pl.BlockSpec((1, DIL_CLASSES, cls_rows, cols), lambda i, j: (i, 0, j, 0))
    const2 = lambda i, j: (0, 0)
    row = lambda n: _resident((1, n), const2)
    out_cols = [hp, hp, hp, 3 * n_dil] + ([d] if natural_in else [])
    out_dtypes = [BF16] * 4 + ([F32] if natural_in else [])
    outs = pl.pallas_call(
        functools.partial(_pre_kernel, natural_in=natural_in),
        grid=(b, s // tm),
        in_specs=[
            pl.BlockSpec((1, tm, d), lambda i, j: (i, j, 0)) if natural_in else cls_spec(d),
            pl.BlockSpec((1, 6, d), lambda i, j: (i, 0, 0)),
            row(d),
            cls_spec(LANES),
            cls_spec(LANES),
            _resident((d, PRE_COLS), const2),
            row(MLA_Q_LORA),
            _resident((MLA_Q_LORA, 2 * hp), const2),
            row(MLA_KV_LORA),
            _resident((MLA_KV_LORA, 2 * hp), const2),
            _resident(gains.shape, const2),
            _resident(ones_mla.shape, const2),
            _resident(ones_dil.shape, const2),
        ],
        out_specs=[cls_spec(c) for c in out_cols],
        out_shape=[jax.ShapeDtypeStruct((b, DIL_CLASSES, cls_len, c), dt) for c, dt in zip(out_cols, out_dtypes)],
        scratch_shapes=[pltpu.VMEM((d // LANES, tm, LANES), F32)] if natural_in else [],
        compiler_params=_cparams(("arbitrary", "arbitrary")),
        name="pre_proj_first" if natural_in else "pre_proj",
    )(x if natural_in else cls_view(x), mod_l, g1, cls_view(cos_t), cls_view(sin_t), w_pre, qan, wqq, kvan, wkv,
      gains, ones_mla, ones_dil)
    return [o.reshape(b, s, o.shape[-1]) for o in outs]


def _pair_output(outs):
    o0, o1 = outs
    lane = lax.broadcasted_iota(jnp.int32, o0.shape, 1)
    return jnp.where(lane < MLA_V, o0 / o0[:, MLA_V:MLA_V + 1], o1 / o1[:, 0:1]).astype(BF16)


def _mla_online_kernel(q_ref, k_ref, v_ref, o_ref, *, tk):
    tq = q_ref.shape[1]
    n_k = k_ref.shape[1] // tk
    outs = []
    for hh in range(2):
        hl = slice(hh * HEAD_PAD, (hh + 1) * HEAD_PAD)
        q = q_ref[0, :, hl]

        def body(j, carry, q=q, hl=hl):
            m, acc = carry
            ks = pl.multiple_of(j * tk, tk)
            k = k_ref[0, pl.ds(ks, tk), hl]
            v = v_ref[0, pl.ds(ks, tk), hl]
            s = lax.dot_general(q, k, (((1,), (1,)), ((), ())), preferred_element_type=F32)
            m_new = jnp.maximum(m, jnp.max(s, axis=-1, keepdims=True))
            p = jnp.exp2(s - m_new)
            acc = jnp.exp2(m - m_new) * acc + jnp.dot(p.astype(BF16), v, preferred_element_type=F32)
            return m_new, acc

        init = (jnp.full((tq, 1), -jnp.inf, F32), jnp.zeros((tq, HEAD_PAD), F32))
        outs.append(lax.fori_loop(0, n_k, body, init, unroll=True)[1])
    o_ref[0] = _pair_output(outs)


def _mla_bounded_kernel(q_ref, k_ref, v_ref, o_ref, *, tk):
    n_k = k_ref.shape[1] // tk
    for qs in range(q_ref.shape[1] // MLA_SUB):
        rows = slice(qs * MLA_SUB, (qs + 1) * MLA_SUB)
        outs = []
        for hh in range(2):
            hl = slice(hh * HEAD_PAD, (hh + 1) * HEAD_PAD)
            q = q_ref[0, rows, hl]
            acc = jnp.zeros((MLA_SUB, HEAD_PAD), F32)
            for j in range(n_k):
                k = k_ref[0, j * tk:(j + 1) * tk, hl]
                v = v_ref[0, j * tk:(j + 1) * tk, hl]
                s = lax.dot_general(q, k, (((1,), (1,)), ((), ())), preferred_element_type=F32)
                acc = acc + jnp.dot(jnp.exp2(s).astype(BF16), v, preferred_element_type=F32)
            outs.append(acc)
        o_ref[0, rows, :] = _pair_output(outs)


def _mla_call(q, k, v, tq, tk, bounded):
    b, s, width = q.shape
    pair = 2 * HEAD_PAD
    n_pair = width // pair
    return pl.pallas_call(
        functools.partial(_mla_bounded_kernel if bounded else _mla_online_kernel, tk=tk),
        grid=(b, n_pair, s // tq),
        in_specs=[
            pl.BlockSpec((1, tq, pair), lambda i, h, j: (i, j, h)),
            pl.BlockSpec((1, s, pair), lambda i, h, j: (i, 0, h)),
            pl.BlockSpec((1, s, pair), lambda i, h, j: (i, 0, h)),
        ],
        out_specs=pl.BlockSpec((1, tq, 2 * MLA_V), lambda i, h, j: (i, j, h)),
        out_shape=jax.ShapeDtypeStruct((b, s, n_pair * 2 * MLA_V), BF16),
        compiler_params=_cparams(("arbitrary", "arbitrary", "arbitrary")),
        name="mla_attn_bounded" if bounded else "mla_attn_online",
    )(q, k, v)


def _dil_kernel(x_ref, bias_ref, o_ref, qf, kf, vf, *states, bounded):
    s_len = o_ref.shape[1]
    cls_len = s_len // DIL_CLASSES
    piece_q = DIL_BQ // DIL_CLASSES
    piece_k = DIL_BK // DIL_CLASSES
    lanes = (slice(0, LANES), slice(LANES, 2 * LANES), slice(2 * LANES, 3 * LANES))
    lane = lax.broadcasted_iota(jnp.int32, (DIL_BQ, LANES), 1)
    first = lane < DIL_HEAD_DIM
    for dst, sl in zip((qf, kf, vf), lanes):
        dst[...] = x_ref[0, :, sl].astype(F32)

    def band_tile(q_t, k_t, v_t, tab):
        zero = jnp.zeros_like(q_t)
        qm = jnp.concatenate([jnp.where(first, q_t, zero), jnp.where(first, zero, q_t)], axis=0)
        s = lax.dot_general(qm, k_t, (((1,), (1,)), ((), ())), preferred_element_type=F32)
        s = s + bias_ref[tab].reshape(2 * DIL_BQ, DIL_BK)
        if not bounded:
            m = jnp.max(s, axis=-1, keepdims=True)
            s = s - m
        p = jnp.exp2(s)
        l = jnp.sum(p, axis=-1, keepdims=True)
        pv = jnp.dot(p.astype(BF16), v_t, preferred_element_type=F32)
        out = (jnp.where(first, l[:DIL_BQ], l[DIL_BQ:]), jnp.where(first, pv[:DIL_BQ], pv[DIL_BQ:]))
        return out if bounded else (jnp.where(first, m[:DIL_BQ], m[DIL_BQ:]),) + out

    def merge(old, new):
        if bounded:
            return tuple(o + n for o, n in zip(old, new))
        m_n = jnp.maximum(old[0], new[0])
        a = jnp.exp2(old[0] - m_n)
        c = jnp.exp2(new[0] - m_n)
        return (m_n, a * old[1] + c * new[1], a * old[2] + c * new[2])

    def variant_of(n, n_blk):
        return 0 if n == 0 else (2 if n == n_blk - 1 else 1)

    p_idx = 1
    n_blk = cls_len // DIL_BQ
    for r in range(DIL_CLASSES):
        for n in range(n_blk):
            i0 = r * cls_len + n * DIL_BQ
            ks = r * cls_len + min(max(n * DIL_BQ - DIL_HALF, 0), cls_len - DIL_BK)
            vals = band_tile(x_ref[0, i0:i0 + DIL_BQ, lanes[0]], x_ref[0, ks:ks + DIL_BK, lanes[1]],
                             x_ref[0, ks:ks + DIL_BK, lanes[2]], 3 * p_idx + variant_of(n, n_blk))
            for st, val in zip(states, vals):
                st[i0:i0 + DIL_BQ, :] = val

    p_idx = 0
    n_blk = s_len // DIL_BQ
    for n in range(n_blk):
        i0 = n * DIL_BQ // DIL_CLASSES
        ks = min(max(n * DIL_BQ - DIL_HALF, 0), s_len - DIL_BK) // DIL_CLASSES
        rows_q = [slice(r * cls_len + i0, r * cls_len + i0 + piece_q) for r in range(DIL_CLASSES)]
        rows_k = [slice(r * cls_len + ks, r * cls_len + ks + piece_k) for r in range(DIL_CLASSES)]
        vals = band_tile(jnp.concatenate([x_ref[0, rq, lanes[0]] for rq in rows_q], axis=0),
                         jnp.concatenate([x_ref[0, rk, lanes[1]] for rk in rows_k], axis=0),
                         jnp.concatenate([x_ref[0, rk, lanes[2]] for rk in rows_k], axis=0),
                         3 * p_idx + variant_of(n, n_blk))
        for r, rq in enumerate(rows_q):
            piece = slice(r * piece_q, (r + 1) * piece_q)
            merged = merge(tuple(st[rq, :] for st in states), tuple(val[piece] for val in vals))
            for st, val in zip(states, merged):
                st[rq, :] = val

    p_idx = 2
    ratio = DIL_PATTERNS[p_idx][1] // DIL_CLASSES
    sub_len = cls_len // ratio
    n_blk = sub_len // DIL_BQ
    for r in range(DIL_CLASSES):
        for c in range(ratio):
            for n in range(n_blk):
                ks = min(max(n * DIL_BQ - DIL_HALF, 0), sub_len - DIL_BK)
                rows_q = pl.ds(r * cls_len + c + ratio * n * DIL_BQ, DIL_BQ, stride=ratio)
                rows_k = pl.ds(r * cls_len + c + ratio * ks, DIL_BK, stride=ratio)
                vals = band_tile(qf[rows_q, :].astype(BF16), kf[rows_k, :].astype(BF16),
                                 vf[rows_k, :].astype(BF16), 3 * p_idx + variant_of(n, n_blk))
                merged = merge(tuple(st[rows_q, :] for st in states), vals)
                for st, val in zip(states, merged):
                    st[rows_q, :] = val

    o_ref[0] = (states[-1][...] / states[-2][...]).astype(BF16)


def _dil_call(qkvd, bias_tabs, bounded):
    b, s, width = qkvd.shape
    n_pair = width // DIL_GROUP
    n_tab = bias_tabs.shape[0]
    n_state = 2 if bounded else 3
    return pl.pallas_call(
        functools.partial(_dil_kernel, bounded=bounded),
        grid=(n_pair, b),
        in_specs=[pl.BlockSpec((1, s, DIL_GROUP), lambda hp, i: (i, 0, hp)),
                  pl.BlockSpec((n_tab, 2, DIL_BQ, DIL_BK), lambda hp, i: (0, hp, 0, 0))],
        out_specs=pl.BlockSpec((1, s, LANES), lambda hp, i: (i, 0, hp)),
        out_shape=jax.ShapeDtypeStruct((b, s, n_pair * LANES), BF16),
        scratch_shapes=[pltpu.VMEM((s, LANES), F32)] * (3 + n_state),
        compiler_params=_cparams(("arbitrary", "arbitrary")),
        name="dil_attn_bounded" if bounded else "dil_attn_online",
    )(qkvd, bias_tabs)


def _post_kernel(x_ref, ya_ref, yb_ref, mod_ref, g1_ref, wg_ref, wa_ref, wb_ref, wo_ref, o_ref):
    x = x_ref[0]
    h = _modulated_norm(x, g1_ref[...], mod_ref[0, 0:1, :], mod_ref[0, 1:2, :])
    gates = jnp.dot(h.astype(BF16), wg_ref[...], preferred_element_type=F32)
    d = x.shape[-1]
    a = jnp.dot(ya_ref[0], wa_ref[...], preferred_element_type=F32)
    bb = jnp.dot(yb_ref[0], wb_ref[...], preferred_element_type=F32)
    merged = jax.nn.sigmoid(gates[:, :d]) * a + jax.nn.sigmoid(gates[:, d:]) * bb
    upd = jnp.dot(merged.astype(BF16), wo_ref[...], preferred_element_type=F32)
    o_ref[0] = x + mod_ref[0, 2:3, :] * upd


def _post_call(x, ya, yb, mod_l, g1, wg, wa, wb, wo, tm):
    b, s, d = x.shape
    tok = lambda i, j: (i, j, 0)
    const2 = lambda i, j: (0, 0)
    return pl.pallas_call(
        _post_kernel,
        grid=(b, s // tm),
        in_specs=[
            pl.BlockSpec((1, tm, d), tok),
            pl.BlockSpec((1, tm, ya.shape[-1]), tok),
            pl.BlockSpec((1, tm, yb.shape[-1]), tok),
            pl.BlockSpec((1, 6, d), lambda i, j: (i, 0, 0)),
            _resident((1, d), const2),
            _resident(wg.shape, const2),
            _resident(wa.shape, const2),
            _resident(wb.shape, const2),
            _resident(wo.shape, const2),
        ],
        out_specs=pl.BlockSpec((1, tm, d), tok),
        out_shape=jax.ShapeDtypeStruct(x.shape, F32),
        compiler_params=_cparams(("arbitrary", "arbitrary")),
        name="attn_out",
    )(x, ya, yb, mod_l, g1, wg, wa, wb, wo)


def _ffn_kernel(x_ref, mod_ref, g2_ref, wg_ref, wu_ref, wd_ref, o_ref, *scratch, natural_out):
    x = jnp.concatenate([x_ref[0, c] for c in range(DIL_CLASSES)], axis=0) if natural_out else x_ref[0]
    h = _modulated_norm(x, g2_ref[...], mod_ref[0, 3:4, :], mod_ref[0, 4:5, :]).astype(BF16)
    g = jnp.dot(h, wg_ref[...], preferred_element_type=F32)
    u = jnp.dot(h, wu_ref[...], preferred_element_type=F32)
    act = (g * jax.nn.sigmoid(g)) * u
    upd = jnp.dot(act.astype(BF16), wd_ref[...], preferred_element_type=F32)
    res = x + mod_ref[0, 5:6, :] * upd
    if not natural_out:
        o_ref[0] = res
        return
    slab_ref, = scratch
    cls_rows = res.shape[0] // DIL_CLASSES
    for cls in range(DIL_CLASSES):
        for c in range(slab_ref.shape[0]):
            slab_ref[c, pl.ds(cls, cls_rows, stride=DIL_CLASSES), :] = res[cls * cls_rows:(cls + 1) * cls_rows,
                                                                           c * LANES:(c + 1) * LANES]
    o_ref[0] = jnp.concatenate([slab_ref[c] for c in range(slab_ref.shape[0])], axis=1)


def _ffn_call(x, mod_l, g2, wg, wu, wd, tm, natural_out):
    b, s, d = x.shape
    tok = lambda i, j: (i, j, 0)
    const2 = lambda i, j: (0, 0)
    cls_rows = tm // DIL_CLASSES
    x_in = x.reshape(b, DIL_CLASSES, s // DIL_CLASSES, d) if natural_out else x
    return pl.pallas_call(
        functools.partial(_ffn_kernel, natural_out=natural_out),
        grid=(b, s // tm),
        in_specs=[
            pl.BlockSpec((1, DIL_CLASSES, cls_rows, d), lambda i, j: (i, 0, j, 0)) if natural_out
            else pl.BlockSpec((1, tm, d), tok),
            pl.BlockSpec((1, 6, d), lambda i, j: (i, 0, 0)),
            _resident((1, d), const2),
            _resident(wg.shape, const2),
            _resident(wu.shape, const2),
            _resident(wd.shape, const2),
        ],
        out_specs=pl.BlockSpec((1, tm, d), tok),
        out_shape=jax.ShapeDtypeStruct(x.shape, F32),
        scratch_shapes=[pltpu.VMEM((d // LANES, tm, LANES), F32)] if natural_out else [],
        compiler_params=_cparams(("arbitrary", "arbitrary")),
        name="swiglu_last" if natural_out else "swiglu",
    )(x_in, mod_l, g2, wg, wu, wd)


def _pad_heads(w, width):
    rows = w.shape[0]
    w = w.reshape(rows, MLA_HEADS, width)
    return jnp.pad(w, ((0, 0), (0, 0), (0, HEAD_PAD - width))).reshape(rows, MLA_HEADS * HEAD_PAD)


def _pad_heads_alternating(w):
    rows = w.shape[0]
    w = w.reshape(rows, MLA_HEADS // 2, 2, MLA_V)
    zero = jnp.zeros_like(w[:, :, 0])
    even = jnp.concatenate([w[:, :, 0], zero], axis=-1)
    odd = jnp.concatenate([zero, w[:, :, 1]], axis=-1)
    return jnp.stack([even, odd], axis=2).reshape(rows, MLA_HEADS * HEAD_PAD)


def _pad_lanes(g):
    return jnp.pad(g, (0, LANES - g.shape[0]))


def _swap_rope(g):
    half = MLA_ROPE // 2
    return jnp.concatenate([jnp.zeros((MLA_NOPE,), g.dtype), g[MLA_NOPE + half:], g[MLA_NOPE:MLA_NOPE + half]])


def _class_major(t, n_class):
    b, s = t.shape[:2]
    return jnp.swapaxes(t.reshape(b, s // n_class, n_class, *t.shape[2:]), 1, 2).reshape(t.shape)


def _block_ones(block, size=2 * LANES):
    idx = np.arange(size) // block
    return jnp.asarray(idx[:, None] == idx[None, :], BF16)


def kernel(x, c, positions, rel_bias, norm1_g, norm2_g, ada_w, ada_b, w_in, q_a_norm, w_q_b, kv_a_norm, w_kv_b,
           q_norm_a, k_norm_a, q_norm_b, k_norm_b, w_branch_a, w_branch_b, w_out, w_ffn_gate, w_ffn_up,
           w_ffn_down):
    depth = w_in.shape[0]
    b, s, d = x.shape
    positions = _class_major(positions, DIL_CLASSES)

    mod = _ada_call(c, ada_w, ada_b).reshape(depth, b, 6, d)
    cos_t, sin_t = _rope_call(positions)
    dil_qk_bound = DIL_HEAD_DIM * jnp.max(jnp.abs(q_norm_b * (DIL_HEAD_DIM ** -0.5 * LOG2E))) * jnp.max(jnp.abs(k_norm_b))
    dil_shift = dil_qk_bound + jnp.max(rel_bias) * LOG2E
    dil_range = 2.0 * dil_qk_bound + (jnp.max(rel_bias) - jnp.min(rel_bias)) * LOG2E
    bias_tabs = _bias_call(rel_bias, jnp.zeros((), F32))
    bias_tabs_shifted = _bias_call(rel_bias, dil_shift)
    ones_mla = _block_ones(HEAD_PAD)
    ones_dil = _block_ones(DIL_HEAD_DIM)
    half = MLA_ROPE // 2
    lane2 = np.arange(2 * LANES) % LANES
    spare_qk = jnp.asarray(lane2 == MLA_QK, F32)
    spare_v = jnp.asarray((np.arange(2 * LANES) == MLA_V) | (np.arange(2 * LANES) == LANES), F32)

    for l in range(depth):
        wl = w_in[l]
        k_rope_cols = jnp.pad(wl[:, _C_KR:_C_QKVB], ((0, 0), (MLA_NOPE, LANES - MLA_QK)))
        w_pre = jnp.concatenate([wl[:, _C_QC:_C_KR], k_rope_cols, wl[:, _C_QKVB:_C_GATE]], axis=1).astype(BF16)
        w_gates = wl[:, _C_GATE:].astype(BF16)
        wq3 = w_q_b[l].reshape(MLA_Q_LORA, MLA_HEADS, MLA_QK)
        lo, hi = wq3[..., MLA_NOPE:MLA_NOPE + half], wq3[..., MLA_NOPE + half:]
        wq_sw = jnp.concatenate([jnp.zeros_like(wq3[..., :MLA_NOPE]), hi, lo], axis=-1)
        wqq = jnp.concatenate([_pad_heads(w_q_b[l], MLA_QK),
                               _pad_heads(wq_sw.reshape(MLA_Q_LORA, -1), MLA_QK)], axis=1).astype(BF16)
        wkv3 = w_kv_b[l].reshape(MLA_KV_LORA, MLA_HEADS, MLA_NOPE + MLA_V)
        wkv = jnp.concatenate([_pad_heads(wkv3[:, :, :MLA_NOPE].reshape(MLA_KV_LORA, -1), MLA_NOPE),
                               _pad_heads_alternating(wkv3[:, :, MLA_NOPE:])], axis=1).astype(BF16)
        gq = q_norm_a[l] * (MLA_QK ** -0.5 * LOG2E)
        gk = k_norm_a[l]
        score_bound = MLA_QK * jnp.max(jnp.abs(gq)) * jnp.max(jnp.abs(gk))
        gains = jnp.stack([
            jnp.tile(_pad_lanes(gq), 2), jnp.tile(_pad_lanes(_swap_rope(gq)), 2),
            jnp.tile(_pad_lanes(gk), 2), jnp.tile(_pad_lanes(_swap_rope(gk)), 2),
            jnp.tile(q_norm_b[l] * (DIL_HEAD_DIM ** -0.5 * LOG2E), 4), jnp.tile(k_norm_b[l], 4),
            -score_bound * spare_qk, spare_qk, spare_v] + [jnp.zeros((2 * LANES,), F32)] * 7)
        g1 = norm1_g[l].reshape(1, d)
        g2 = norm2_g[l].reshape(1, d)

        q, k, v, qkvd, *x_cls = _pre_call(x, mod[l], g1, cos_t, sin_t, w_pre, q_a_norm[l].reshape(1, -1), wqq,
                                          kv_a_norm[l].reshape(1, -1), wkv, gains, ones_mla, ones_dil, tm=512,
                                          natural_in=(l == 0))
        if x_cls:
            x, = x_cls
        ya = lax.cond(score_bound <= MLA_SAFE_BOUND,
                      functools.partial(_mla_call, tq=2048, tk=256, bounded=True),
                      functools.partial(_mla_call, tq=512, tk=1024, bounded=False), q, k, v)
        yb = lax.cond(dil_range <= DIL_SAFE_RANGE,
                      lambda a, shifted, plain: _dil_call(a, shifted, bounded=True),
                      lambda a, shifted, plain: _dil_call(a, plain, bounded=False),
                      qkvd, bias_tabs_shifted, bias_tabs)
        x = _post_call(x, ya, yb, mod[l], g1, w_gates, w_branch_a[l].astype(BF16), w_branch_b[l].astype(BF16),
                       w_out[l].astype(BF16), tm=512)
        x = _ffn_call(x, mod[l], g2, w_ffn_gate[l].astype(BF16), w_ffn_up[l].astype(BF16),
                      w_ffn_down[l].astype(BF16), tm=512, natural_out=(l == depth - 1))
    return x
```

```python
import functools
import math

import jax
import jax.numpy as jnp
import numpy as np
from jax import lax
from jax.experimental import pallas as pl
from jax.experimental.pallas import tpu as pltpu

F32 = jnp.float32
BF16 = jnp.bfloat16

LANES = 128
V7X_VMEM_BYTES = 64 * 1024 * 1024
VMEM_LIMIT_BYTES = V7X_VMEM_BYTES - 8 * 1024 * 1024

MLA_HEADS = 8
MLA_Q_LORA = 256
MLA_KV_LORA = 128
MLA_NOPE = 64
MLA_ROPE = 32
MLA_V = 64
MLA_QK = MLA_NOPE + MLA_ROPE
ROPE_THETA = 10000.0
DIL_HEADS = 8
DIL_HEAD_DIM = 64
DIL_PATTERNS = ((128, 1), (512, 4), (2048, 16))
DIL_HALF = 64
REL_BUCKETS = 32
REL_MAX_DIST = 1024
EPS = 1e-6
NEG_INF = -1e30
LOG2E = math.log2(math.e)
MLA_SAFE_BOUND = 60.0
MLA_SUB = 512

PRE_TM = 512
POST_TM = 1024
FFN_TM = 1024
MLA_BOUNDED_TILES = (2048, 256)
MLA_ONLINE_TILES = (512, 1024)

_C_QC = 0
_C_KVC = _C_QC + MLA_Q_LORA
_C_KR = _C_KVC + MLA_KV_LORA
_C_QKVB = _C_KR + MLA_ROPE
_C_GATE = _C_QKVB + 3 * DIL_HEADS * DIL_HEAD_DIM
PRE_COLS = MLA_Q_LORA + MLA_KV_LORA + LANES + 3 * DIL_HEADS * DIL_HEAD_DIM
HEAD_PAD = LANES

DIL_BQ = 2 * DIL_HALF
DIL_BK = 4 * DIL_HALF
DIL_CLASSES = 4
DIL_GROUP = 3 * LANES
DIL_SAFE_RANGE = 120.0


def _cparams(sem):
    return pltpu.CompilerParams(dimension_semantics=sem, vmem_limit_bytes=VMEM_LIMIT_BYTES)


def _resident(shape, index_map):
    return pl.BlockSpec(shape, index_map, pipeline_mode=pl.Buffered(1))


def _ada_kernel(c_ref, w_ref, b_ref, o_ref):
    c = c_ref[...]
    act = (c * jax.nn.sigmoid(c)).astype(BF16)
    o_ref[0] = jnp.dot(act, w_ref[0].astype(BF16), preferred_element_type=F32) + b_ref[0]


def _ada_call(c, ada_w, ada_b):
    depth, d, six_d = ada_w.shape
    b = c.shape[0]
    n_chunk = six_d // d
    return pl.pallas_call(
        _ada_kernel,
        grid=(depth, n_chunk),
        in_specs=[
            pl.BlockSpec((b, d), lambda l, j: (0, 0)),
            pl.BlockSpec((1, d, d), lambda l, j: (l, 0, j)),
            pl.BlockSpec((1, 1, d), lambda l, j: (l, 0, j)),
        ],
        out_specs=pl.BlockSpec((1, b, d), lambda l, j: (l, 0, j)),
        out_shape=jax.ShapeDtypeStruct((depth, b, six_d), F32),
        compiler_params=_cparams(("arbitrary", "arbitrary")),
        name="ada_mod",
    )(c, ada_w, ada_b.reshape(depth, 1, six_d))


def _rope_kernel(pos_ref, inv_ref, c_ref, s_ref):
    ang = pos_ref[0].astype(F32) * inv_ref[...]
    cosv = jnp.cos(ang)
    sinv = jnp.sin(ang)
    lane = lax.broadcasted_iota(jnp.int32, ang.shape, 1)
    half = MLA_ROPE // 2
    c_ref[0] = jnp.where(lane < MLA_NOPE, 1.0, jnp.where(lane < MLA_QK, cosv, 0.0))
    s_ref[0] = jnp.where(lane < MLA_NOPE, 0.0,
                         jnp.where(lane < MLA_NOPE + half, -sinv, jnp.where(lane < MLA_QK, sinv, 0.0)))


def _rope_call(positions):
    b, s = positions.shape
    half = MLA_ROPE // 2
    inv = ROPE_THETA ** (-jnp.arange(half, dtype=F32) / half)
    inv_lane = jnp.zeros((1, LANES), F32)
    inv_lane = inv_lane.at[0, MLA_NOPE:MLA_NOPE + half].set(inv).at[0, MLA_NOPE + half:MLA_QK].set(inv)
    ts = 512
    spec = pl.BlockSpec((1, ts, LANES), lambda i, j: (i, j, 0))
    return pl.pallas_call(
        _rope_kernel,
        grid=(b, s // ts),
        in_specs=[pl.BlockSpec((1, ts, 1), lambda i, j: (i, j, 0)),
                  pl.BlockSpec((1, LANES), lambda i, j: (0, 0))],
        out_specs=[spec, spec],
        out_shape=[jax.ShapeDtypeStruct((b, s, LANES), F32)] * 2,
        compiler_params=_cparams(("arbitrary", "arbitrary")),
        name="rope_tables",
    )(positions.reshape(b, s, 1), inv_lane)


def _t5_bucket(rel):
    nb = REL_BUCKETS // 2
    max_exact = nb // 2
    ret = jnp.where(rel > 0, nb, 0)
    n = jnp.abs(rel)
    nf = jnp.maximum(n, 1).astype(F32)
    large = max_exact + (jnp.log(nf / max_exact) / math.log(REL_MAX_DIST / max_exact) * (nb - max_exact)).astype(jnp.int32)
    large = jnp.minimum(large, nb - 1)
    return ret + jnp.where(n < max_exact, n, large)


def _bias_kernel(bucket_ref, rb_ref, shift_ref, o_ref):
    h = pl.program_id(1)
    bucket = bucket_ref[0]
    acc = jnp.full(bucket.shape, NEG_INF, F32)
    for bkt in range(REL_BUCKETS):
        acc = jnp.where(bucket == bkt, rb_ref[bkt, h], acc)
    o_ref[0, 0] = acc * LOG2E - shift_ref[0]


def _bias_call(rel_bias, shift):
    tabs = []
    for _, dil in DIL_PATTERNS:
        qi, kj = np.arange(DIL_BQ), np.arange(DIL_BK)
        if dil == 1:
            qi = DIL_CLASSES * (qi % (DIL_BQ // DIL_CLASSES)) + qi // (DIL_BQ // DIL_CLASSES)
            kj = DIL_CLASSES * (kj % (DIL_BK // DIL_CLASSES)) + kj // (DIL_BK // DIL_CLASSES)
        qi = jnp.asarray(qi, jnp.int32)[:, None]
        kj = jnp.asarray(kj, jnp.int32)[None, :]
        for variant in range(3):
            rel = kj - variant * DIL_HALF - qi
            tabs.append(jnp.where(jnp.abs(rel) <= DIL_HALF, _t5_bucket(rel * dil), -1))
    bucket = jnp.stack(tabs).astype(jnp.int32)
    n_tab = bucket.shape[0]
    return pl.pallas_call(
        _bias_kernel,
        grid=(n_tab, DIL_HEADS),
        in_specs=[pl.BlockSpec((1, DIL_BQ, DIL_BK), lambda t, h: (t, 0, 0)),
                  pl.BlockSpec(memory_space=pltpu.SMEM), pl.BlockSpec(memory_space=pltpu.SMEM)],
        out_specs=pl.BlockSpec((1, 1, DIL_BQ, DIL_BK), lambda t, h: (t, h, 0, 0)),
        out_shape=jax.ShapeDtypeStruct((n_tab, DIL_HEADS, DIL_BQ, DIL_BK), F32),
        compiler_params=_cparams(("arbitrary", "arbitrary")),
        name="bias_tables",
    )(bucket, rel_bias, jnp.reshape(shift, (1,)).astype(F32))


def _modulated_norm(x, gain, shift, scale):
    r = lax.rsqrt(jnp.mean(x * x, axis=-1, keepdims=True) + EPS)
    return ((x * r) * gain) * (1.0 + scale) + shift


PRE_SUB = 256


def _group_sumsq(t, ones_ref):
    return jnp.dot((t * t).astype(BF16), ones_ref[...], preferred_element_type=F32)


def _class_slabs_load(slab_ref, tile, cls):
    rows = tile // DIL_CLASSES
    return jnp.concatenate([slab_ref[c, pl.ds(cls, rows, stride=DIL_CLASSES), :] for c in range(slab_ref.shape[0])],
                           axis=1)


def _pre_kernel(x_ref, mod_ref, g1_ref, cos_ref, sin_ref, w_ref, qan_ref, wqq_ref, kvan_ref, wkv_ref,
                gains_ref, ones_mla_ref, ones_dil_ref, *rest, natural_in):
    if natural_in:
        q_out, k_out, v_out, qkvd_out, xp_out, slab_ref = rest
        tm = x_ref.shape[1]
        for c in range(slab_ref.shape[0]):
            slab_ref[c] = x_ref[0, :, c * LANES:(c + 1) * LANES]
    else:
        q_out, k_out, v_out, qkvd_out = rest
        tm = x_ref.shape[1] * x_ref.shape[2]
    cls_rows = tm // DIL_CLASSES
    cls_per_sub = PRE_SUB // cls_rows

    def class_rows(ref, sub):
        return jnp.concatenate([ref[0, sub * cls_per_sub + j] for j in range(cls_per_sub)], axis=0)

    def store(ref, sub, sl, val):
        for j in range(cls_per_sub):
            ref[0, sub * cls_per_sub + j, :, sl] = val[j * cls_rows:(j + 1) * cls_rows]

    hp = MLA_HEADS * HEAD_PAD
    n_dil = DIL_HEADS * DIL_HEAD_DIM
    grp = 2 * LANES
    c0 = MLA_Q_LORA
    c1 = c0 + MLA_KV_LORA
    c2 = c1 + LANES
    half = MLA_ROPE // 2
    inv_qk = 1.0 / MLA_QK
    inv_dh = 1.0 / DIL_HEAD_DIM
    lane = lax.broadcasted_iota(jnp.int32, (PRE_SUB, LANES), 1)
    low_half = lane < MLA_NOPE + half

    for sub in range(tm // PRE_SUB):
        if natural_in:
            pieces = [_class_slabs_load(slab_ref, tm, sub * cls_per_sub + j) for j in range(cls_per_sub)]
            for j, piece in enumerate(pieces):
                xp_out[0, sub * cls_per_sub + j] = piece
            x = jnp.concatenate(pieces, axis=0)
        else:
            x = class_rows(x_ref, sub)
        h = _modulated_norm(x, g1_ref[...], mod_ref[0, 0:1, :], mod_ref[0, 1:2, :])
        proj = jnp.dot(h.astype(BF16), w_ref[...], preferred_element_type=F32)

        qc = proj[:, :c0]
        qcn = (qc * lax.rsqrt(jnp.mean(qc * qc, axis=-1, keepdims=True) + EPS)) * qan_ref[...]
        qq = jnp.dot(qcn.astype(BF16), wqq_ref[...], preferred_element_type=F32)
        kvc = proj[:, c0:c1]
        kvn = (kvc * lax.rsqrt(jnp.mean(kvc * kvc, axis=-1, keepdims=True) + EPS)) * kvan_ref[...]
        kv = jnp.dot(kvn.astype(BF16), wkv_ref[...], preferred_element_type=F32)
        k_rope = proj[:, c1:c2]
        k_rope_sw = jnp.where(low_half, pltpu.roll(k_rope, LANES - half, 1), pltpu.roll(k_rope, half, 1))

        cos2 = jnp.concatenate([class_rows(cos_ref, sub)] * 2, axis=1)
        sin2 = jnp.concatenate([class_rows(sin_ref, sub)] * 2, axis=1)
        aq = gains_ref[0:1, :] * cos2
        bq = gains_ref[1:2, :] * sin2
        ak = gains_ref[2:3, :] * cos2
        bk = gains_ref[3:4, :] * sin2
        kr2 = jnp.concatenate([k_rope] * 2, axis=1)
        krs2 = jnp.concatenate([k_rope_sw] * 2, axis=1)
        for g in range(hp // grp):
            sl = slice(g * grp, (g + 1) * grp)
            qh = qq[:, sl]
            rq = lax.rsqrt(_group_sumsq(qh, ones_mla_ref) * inv_qk + EPS)
            store(q_out, sub, sl, (rq * (qh * aq + qq[:, hp + g * grp:hp + (g + 1) * grp] * bq)
                                   + gains_ref[6:7, :]).astype(BF16))
            kh = kv[:, sl] + kr2
            rk = lax.rsqrt(_group_sumsq(kh, ones_mla_ref) * inv_qk + EPS)
            store(k_out, sub, sl, (rk * (kh * ak + krs2 * bk) + gains_ref[7:8, :]).astype(BF16))
            store(v_out, sub, sl, (kv[:, hp + g * grp:hp + (g + 1) * grp] + gains_ref[8:9, :]).astype(BF16))

        for g in range(n_dil // grp):
            qd = proj[:, c2 + g * grp:c2 + (g + 1) * grp]
            qd = ((qd * lax.rsqrt(_group_sumsq(qd, ones_dil_ref) * inv_dh + EPS)) * gains_ref[4:5, :]).astype(BF16)
            kd = proj[:, c2 + n_dil + g * grp:c2 + n_dil + (g + 1) * grp]
            kd = ((kd * lax.rsqrt(_group_sumsq(kd, ones_dil_ref) * inv_dh + EPS)) * gains_ref[5:6, :]).astype(BF16)
            vd = proj[:, c2 + 2 * n_dil + g * grp:c2 + 2 * n_dil + (g + 1) * grp].astype(BF16)
            for j in range(grp // LANES):
                base = (g * (grp // LANES) + j) * DIL_GROUP
                for t, val in enumerate((qd, kd, vd)):
                    store(qkvd_out, sub, slice(base + t * LANES, base + (t + 1) * LANES), val[:, j * LANES:(j + 1) * LANES])


def _pre_call(x, mod_l, g1, cos_t, sin_t, w_pre, qan, wqq, kvan, wkv, gains, ones_mla, ones_dil, tm, natural_in):
    b, s, d = x.shape
    hp = MLA_HEADS * HEAD_PAD
    n_dil = DIL_HEADS * DIL_HEAD_DIM
    cls_len = s // DIL_CLASSES
    cls_rows = tm // DIL_CLASSES
    cls_view = lambda t: t.reshape(b, DIL_CLASSES, cls_len, t.shape[-1])
    cls_spec = lambda cols: pl.BlockSpec((1, DIL_CLASSES, cls_rows, cols), lambda i, j: (i, 0, j, 0))
    const2 = lambda i, j: (0, 0)
    row = lambda n: _resident((1, n), const2)
    out_cols = [hp, hp, hp, 3 * n_dil] + ([d] if natural_in else [])
    out_dtypes = [BF16] * 4 + ([F32] if natural_in else [])
    outs = pl.pallas_call(
        functools.partial(_pre_kernel, natural_in=natural_in),
        grid=(b, s // tm),
        in_specs=[
            pl.BlockSpec((1, tm, d), lambda i, j: (i, j, 0)) if natural_in else cls_spec(d),
            pl.BlockSpec((1, 6, d), lambda i, j: (i, 0, 0)),
            row(d),
            cls_spec(LANES),
            cls_spec(LANES),
            _resident((d, PRE_COLS), const2),
            row(MLA_Q_LORA),
            _resident((MLA_Q_LORA, 2 * hp), const2),
            row(MLA_KV_LORA),
            _resident((MLA_KV_LORA, 2 * hp), const2),
            _resident(gains.shape, const2),
            _resident(ones_mla.shape, const2),
            _resident(ones_dil.shape, const2),
        ],
        out_specs=[cls_spec(c) for c in out_cols],
        out_shape=[jax.ShapeDtypeStruct((b, DIL_CLASSES, cls_len, c), dt) for c, dt in zip(out_cols, out_dtypes)],
        scratch_shapes=[pltpu.VMEM((d // LANES, tm, LANES), F32)] if natural_in else [],
        compiler_params=_cparams(("arbitrary", "arbitrary")),
        name="pre_proj_first" if natural_in else "pre_proj",
    )(x if natural_in else cls_view(x), mod_l, g1, cls_view(cos_t), cls_view(sin_t), w_pre, qan, wqq, kvan, wkv,
      gains, ones_mla, ones_dil)
    return [o.reshape(b, s, o.shape[-1]) for o in outs]


def _pair_output(outs):
    o0, o1 = outs
    lane = lax.broadcasted_iota(jnp.int32, o0.shape, 1)
    return jnp.where(lane < MLA_V, o0 / o0[:, MLA_V:MLA_V + 1], o1 / o1[:, 0:1]).astype(BF16)


def _mla_online_kernel(q_ref, k_ref, v_ref, o_ref, *, tk):
    tq = q_ref.shape[1]
    n_k = k_ref.shape[1] // tk
    outs = []
    for hh in range(2):
        hl = slice(hh * HEAD_PAD, (hh + 1) * HEAD_PAD)
        q = q_ref[0, :, hl]

        def body(j, carry, q=q, hl=hl):
            m, acc = carry
            ks = pl.multiple_of(j * tk, tk)
            k = k_ref[0, pl.ds(ks, tk), hl]
            v = v_ref[0, pl.ds(ks, tk), hl]
            s = lax.dot_general(q, k, (((1,), (1,)), ((), ())), preferred_element_type=F32)
            m_new = jnp.maximum(m, jnp.max(s, axis=-1, keepdims=True))
            p = jnp.exp2(s - m_new)
            acc = jnp.exp2(m - m_new) * acc + jnp.dot(p.astype(BF16), v, preferred_element_type=F32)
            return m_new, acc

        init = (jnp.full((tq, 1), -jnp.inf, F32), jnp.zeros((tq, HEAD_PAD), F32))
        outs.append(lax.fori_loop(0, n_k, body, init, unroll=True)[1])
    o_ref[0] = _pair_output(outs)


def _mla_bounded_kernel(q_ref, k_ref, v_ref, o_ref, *, tk):
    n_k = k_ref.shape[1] // tk
    for qs in range(q_ref.shape[1] // MLA_SUB):
        rows = slice(qs * MLA_SUB, (qs + 1) * MLA_SUB)
        outs = []
        for hh in range(2):
            hl = slice(hh * HEAD_PAD, (hh + 1) * HEAD_PAD)
            q = q_ref[0, rows, hl]
            acc = jnp.zeros((MLA_SUB, HEAD_PAD), F32)
            for j in range(n_k):
                k = k_ref[0, j * tk:(j + 1) * tk, hl]
                v = v_ref[0, j * tk:(j + 1) * tk, hl]
                s = lax.dot_general(q, k, (((1,), (1,)), ((), ())), preferred_element_type=F32)
                acc = acc + jnp.dot(jnp.exp2(s).astype(BF16), v, preferred_element_type=F32)
            outs.append(acc)
        o_ref[0, rows, :] = _pair_output(outs)


def _mla_call(q, k, v, tiles, bounded):
    tq, tk = tiles
    b, s, width = q.shape
    pair = 2 * HEAD_PAD
    n_pair = width // pair
    return pl.pallas_call(
        functools.partial(_mla_bounded_kernel if bounded else _mla_online_kernel, tk=tk),
        grid=(b, n_pair, s // tq),
        in_specs=[
            pl.BlockSpec((1, tq, pair), lambda i, h, j: (i, j, h)),
            pl.BlockSpec((1, s, pair), lambda i, h, j: (i, 0, h)),
            pl.BlockSpec((1, s, pair), lambda i, h, j: (i, 0, h)),
        ],
        out_specs=pl.BlockSpec((1, tq, 2 * MLA_V), lambda i, h, j: (i, j, h)),
        out_shape=jax.ShapeDtypeStruct((b, s, n_pair * 2 * MLA_V), BF16),
        compiler_params=_cparams(("arbitrary", "arbitrary", "arbitrary")),
        name="mla_attn_bounded" if bounded else "mla_attn_online",
    )(q, k, v)


def _dil_kernel(x_ref, bias_ref, o_ref, qf, kf, vf, *states, bounded):
    s_len = o_ref.shape[1]
    cls_len = s_len // DIL_CLASSES
    piece_q = DIL_BQ // DIL_CLASSES
    piece_k = DIL_BK // DIL_CLASSES
    lanes = (slice(0, LANES), slice(LANES, 2 * LANES), slice(2 * LANES, 3 * LANES))
    lane = lax.broadcasted_iota(jnp.int32, (DIL_BQ, LANES), 1)
    first = lane < DIL_HEAD_DIM
    for dst, sl in zip((qf, kf, vf), lanes):
        dst[...] = x_ref[0, :, sl].astype(F32)

    def band_tile(q_t, k_t, v_t, tab):
        zero = jnp.zeros_like(q_t)
        qm = jnp.concatenate([jnp.where(first, q_t, zero), jnp.where(first, zero, q_t)], axis=0)
        s = lax.dot_general(qm, k_t, (((1,), (1,)), ((), ())), preferred_element_type=F32)
        s = s + bias_ref[tab].reshape(2 * DIL_BQ, DIL_BK)
        if not bounded:
            m = jnp.max(s, axis=-1, keepdims=True)
            s = s - m
        p = jnp.exp2(s)
        l = jnp.sum(p, axis=-1, keepdims=True)
        pv = jnp.dot(p.astype(BF16), v_t, preferred_element_type=F32)
        out = (jnp.where(first, l[:DIL_BQ], l[DIL_BQ:]), jnp.where(first, pv[:DIL_BQ], pv[DIL_BQ:]))
        return out if bounded else (jnp.where(first, m[:DIL_BQ], m[DIL_BQ:]),) + out

    def merge(old, new):
        if bounded:
            return tuple(o + n for o, n in zip(old, new))
        m_n = jnp.maximum(old[0], new[0])
        a = jnp.exp2(old[0] - m_n)
        c = jnp.exp2(new[0] - m_n)
        return (m_n, a * old[1] + c * new[1], a * old[2] + c * new[2])

    def variant_of(n, n_blk):
        return 0 if n == 0 else (2 if n == n_blk - 1 else 1)

    p_idx = 1
    n_blk = cls_len // DIL_BQ
    for r in range(DIL_CLASSES):
        for n in range(n_blk):
            i0 = r * cls_len + n * DIL_BQ
            ks = r * cls_len + min(max(n * DIL_BQ - DIL_HALF, 0), cls_len - DIL_BK)
            vals = band_tile(x_ref[0, i0:i0 + DIL_BQ, lanes[0]], x_ref[0, ks:ks + DIL_BK, lanes[1]],
                             x_ref[0, ks:ks + DIL_BK, lanes[2]], 3 * p_idx + variant_of(n, n_blk))
            for st, val in zip(states, vals):
                st[i0:i0 + DIL_BQ, :] = val

    p_idx = 0
    n_blk = s_len // DIL_BQ
    for n in range(n_blk):
        i0 = n * DIL_BQ // DIL_CLASSES
        ks = min(max(n * DIL_BQ - DIL_HALF, 0), s_len - DIL_BK) // DIL_CLASSES
        rows_q = [slice(r * cls_len + i0, r * cls_len + i0 + piece_q) for r in range(DIL_CLASSES)]
        rows_k = [slice(r * cls_len + ks, r * cls_len + ks + piece_k) for r in range(DIL_CLASSES)]
        vals = band_tile(jnp.concatenate([x_ref[0, rq, lanes[0]] for rq in rows_q], axis=0),
                         jnp.concatenate([x_ref[0, rk, lanes[1]] for rk in rows_k], axis=0),
                         jnp.concatenate([x_ref[0, rk, lanes[2]] for rk in rows_k], axis=0),
                         3 * p_idx + variant_of(n, n_blk))
        for r, rq in enumerate(rows_q):
            piece = slice(r * piece_q, (r + 1) * piece_q)
            merged = merge(tuple(st[rq, :] for st in states), tuple(val[piece] for val in vals))
            for st, val in zip(states, merged):
                st[rq, :] = val

    p_idx = 2
    ratio = DIL_PATTERNS[p_idx][1] // DIL_CLASSES
    sub_len = cls_len // ratio
    n_blk = sub_len // DIL_BQ
    for r in range(DIL_CLASSES):
        for c in range(ratio):
            for n in range(n_blk):
                ks = min(max(n * DIL_BQ - DIL_HALF, 0), sub_len - DIL_BK)
                rows_q = pl.ds(r * cls_len + c + ratio * n * DIL_BQ, DIL_BQ, stride=ratio)
                rows_k = pl.ds(r * cls_len + c + ratio * ks, DIL_BK, stride=ratio)
                vals = band_tile(qf[rows_q, :].astype(BF16), kf[rows_k, :].astype(BF16),
                                 vf[rows_k, :].astype(BF16), 3 * p_idx + variant_of(n, n_blk))
                merged = merge(tuple(st[rows_q, :] for st in states), vals)
                for st, val in zip(states, merged):
                    st[rows_q, :] = val

    o_ref[0] = (states[-1][...] / states[-2][...]).astype(BF16)


def _dil_call(qkvd, bias_tabs, bounded):
    b, s, width = qkvd.shape
    n_pair = width // DIL_GROUP
    n_tab = bias_tabs.shape[0]
    n_state = 2 if bounded else 3
    return pl.pallas_call(
        functools.partial(_dil_kernel, bounded=bounded),
        grid=(n_pair, b),
        in_specs=[pl.BlockSpec((1, s, DIL_GROUP), lambda hp, i: (i, 0, hp)),
                  pl.BlockSpec((n_tab, 2, DIL_BQ, DIL_BK), lambda hp, i: (0, hp, 0, 0))],
        out_specs=pl.BlockSpec((1, s, LANES), lambda hp, i: (i, 0, hp)),
        out_shape=jax.ShapeDtypeStruct((b, s, n_pair * LANES), BF16),
        scratch_shapes=[pltpu.VMEM((s, LANES), F32)] * (3 + n_state),
        compiler_params=_cparams(("arbitrary", "arbitrary")),
        name="dil_attn_bounded" if bounded else "dil_attn_online",
    )(qkvd, bias_tabs)


def _post_kernel(x_ref, ya_ref, yb_ref, mod_ref, g1_ref, wg_ref, wa_ref, wb_ref, wo_ref, o_ref):
    x = x_ref[0]
    h = _modulated_norm(x, g1_ref[...], mod_ref[0, 0:1, :], mod_ref[0, 1:2, :])
    gates = jnp.dot(h.astype(BF16), wg_ref[...], preferred_element_type=F32)
    d = x.shape[-1]
    a = jnp.dot(ya_ref[0], wa_ref[...], preferred_element_type=F32)
    bb = jnp.dot(yb_ref[0], wb_ref[...], preferred_element_type=F32)
    merged = jax.nn.sigmoid(gates[:, :d]) * a + jax.nn.sigmoid(gates[:, d:]) * bb
    upd = jnp.dot(merged.astype(BF16), wo_ref[...], preferred_element_type=F32)
    o_ref[0] = x + mod_ref[0, 2:3, :] * upd


def _post_call(x, ya, yb, mod_l, g1, wg, wa, wb, wo, tm):
    b, s, d = x.shape
    tok = lambda i, j: (i, j, 0)
    const2 = lambda i, j: (0, 0)
    return pl.pallas_call(
        _post_kernel,
        grid=(b, s // tm),
        in_specs=[
            pl.BlockSpec((1, tm, d), tok),
            pl.BlockSpec((1, tm, ya.shape[-1]), tok),
            pl.BlockSpec((1, tm, yb.shape[-1]), tok),
            pl.BlockSpec((1, 6, d), lambda i, j: (i, 0, 0)),
            _resident((1, d), const2),
            _resident(wg.shape, const2),
            _resident(wa.shape, const2),
            _resident(wb.shape, const2),
            _resident(wo.shape, const2),
        ],
        out_specs=pl.BlockSpec((1, tm, d), tok),
        out_shape=jax.ShapeDtypeStruct(x.shape, F32),
        compiler_params=_cparams(("arbitrary", "arbitrary")),
        name="attn_out",
    )(x, ya, yb, mod_l, g1, wg, wa, wb, wo)


def _ffn_kernel(x_ref, mod_ref, g2_ref, wg_ref, wu_ref, wd_ref, o_ref, *scratch, natural_out):
    x = jnp.concatenate([x_ref[0, c] for c in range(DIL_CLASSES)], axis=0) if natural_out else x_ref[0]
    h = _modulated_norm(x, g2_ref[...], mod_ref[0, 3:4, :], mod_ref[0, 4:5, :]).astype(BF16)
    g = jnp.dot(h, wg_ref[...], preferred_element_type=F32)
    u = jnp.dot(h, wu_ref[...], preferred_element_type=F32)
    act = (g * jax.nn.sigmoid(g)) * u
    upd = jnp.dot(act.astype(BF16), wd_ref[...], preferred_element_type=F32)
    res = x + mod_ref[0, 5:6, :] * upd
    if not natural_out:
        o_ref[0] = res
        return
    slab_ref, = scratch
    cls_rows = res.shape[0] // DIL_CLASSES
    for cls in range(DIL_CLASSES):
        for c in range(slab_ref.shape[0]):
            slab_ref[c, pl.ds(cls, cls_rows, stride=DIL_CLASSES), :] = res[cls * cls_rows:(cls + 1) * cls_rows,
                                                                           c * LANES:(c + 1) * LANES]
    o_ref[0] = jnp.concatenate([slab_ref[c] for c in range(slab_ref.shape[0])], axis=1)


def _ffn_call(x, mod_l, g2, wg, wu, wd, tm, natural_out):
    b, s, d = x.shape
    tok = lambda i, j: (i, j, 0)
    const2 = lambda i, j: (0, 0)
    cls_rows = tm // DIL_CLASSES
    x_in = x.reshape(b, DIL_CLASSES, s // DIL_CLASSES, d) if natural_out else x
    return pl.pallas_call(
        functools.partial(_ffn_kernel, natural_out=natural_out),
        grid=(b, s // tm),
        in_specs=[
            pl.BlockSpec((1, DIL_CLASSES, cls_rows, d), lambda i, j: (i, 0, j, 0)) if natural_out
            else pl.BlockSpec((1, tm, d), tok),
            pl.BlockSpec((1, 6, d), lambda i, j: (i, 0, 0)),
            _resident((1, d), const2),
            _resident(wg.shape, const2),
            _resident(wu.shape, const2),
            _resident(wd.shape, const2),
        ],
        out_specs=pl.BlockSpec((1, tm, d), tok),
        out_shape=jax.ShapeDtypeStruct(x.shape, F32),
        scratch_shapes=[pltpu.VMEM((d // LANES, tm, LANES), F32)] if natural_out else [],
        compiler_params=_cparams(("arbitrary", "arbitrary")),
        name="swiglu_last" if natural_out else "swiglu",
    )(x_in, mod_l, g2, wg, wu, wd)


def _pad_heads(w, width):
    rows = w.shape[0]
    w = w.reshape(rows, MLA_HEADS, width)
    return jnp.pad(w, ((0, 0), (0, 0), (0, HEAD_PAD - width))).reshape(rows, MLA_HEADS * HEAD_PAD)


def _pad_heads_alternating(w):
    rows = w.shape[0]
    w = w.reshape(rows, MLA_HEADS // 2, 2, MLA_V)
    zero = jnp.zeros_like(w[:, :, 0])
    even = jnp.concatenate([w[:, :, 0], zero], axis=-1)
    odd = jnp.concatenate([zero, w[:, :, 1]], axis=-1)
    return jnp.stack([even, odd], axis=2).reshape(rows, MLA_HEADS * HEAD_PAD)


def _pad_lanes(g):
    return jnp.pad(g, (0, LANES - g.shape[0]))


def _swap_rope(g):
    half = MLA_ROPE // 2
    return jnp.concatenate([jnp.zeros((MLA_NOPE,), g.dtype), g[MLA_NOPE + half:], g[MLA_NOPE:MLA_NOPE + half]])


def _class_major(t, n_class):
    b, s = t.shape[:2]
    return jnp.swapaxes(t.reshape(b, s // n_class, n_class, *t.shape[2:]), 1, 2).reshape(t.shape)


def _block_ones(block, size=2 * LANES):
    idx = np.arange(size) // block
    return jnp.asarray(idx[:, None] == idx[None, :], BF16)


def kernel(x, c, positions, rel_bias, norm1_g, norm2_g, ada_w, ada_b, w_in, q_a_norm, w_q_b, kv_a_norm, w_kv_b,
           q_norm_a, k_norm_a, q_norm_b, k_norm_b, w_branch_a, w_branch_b, w_out, w_ffn_gate, w_ffn_up,
           w_ffn_down):
    depth = w_in.shape[0]
    b, s, d = x.shape
    positions = _class_major(positions, DIL_CLASSES)

    mod = _ada_call(c, ada_w, ada_b).reshape(depth, b, 6, d)
    cos_t, sin_t = _rope_call(positions)
    dil_qk_bound = DIL_HEAD_DIM * jnp.max(jnp.abs(q_norm_b * (DIL_HEAD_DIM ** -0.5 * LOG2E))) * jnp.max(jnp.abs(k_norm_b))
    dil_shift = dil_qk_bound + jnp.max(rel_bias) * LOG2E
    dil_range = 2.0 * dil_qk_bound + (jnp.max(rel_bias) - jnp.min(rel_bias)) * LOG2E
    bias_tabs = _bias_call(rel_bias, jnp.zeros((), F32))
    bias_tabs_shifted = _bias_call(rel_bias, dil_shift)
    ones_mla = _block_ones(HEAD_PAD)
    ones_dil = _block_ones(DIL_HEAD_DIM)
    half = MLA_ROPE // 2
    lane2 = np.arange(2 * LANES) % LANES
    spare_qk = jnp.asarray(lane2 == MLA_QK, F32)
    spare_v = jnp.asarray((np.arange(2 * LANES) == MLA_V) | (np.arange(2 * LANES) == LANES), F32)

    for l in range(depth):
        wl = w_in[l]
        k_rope_cols = jnp.pad(wl[:, _C_KR:_C_QKVB], ((0, 0), (MLA_NOPE, LANES - MLA_QK)))
        w_pre = jnp.concatenate([wl[:, _C_QC:_C_KR], k_rope_cols, wl[:, _C_QKVB:_C_GATE]], axis=1).astype(BF16)
        w_gates = wl[:, _C_GATE:].astype(BF16)
        wq3 = w_q_b[l].reshape(MLA_Q_LORA, MLA_HEADS, MLA_QK)
        lo, hi = wq3[..., MLA_NOPE:MLA_NOPE + half], wq3[..., MLA_NOPE + half:]
        wq_sw = jnp.concatenate([jnp.zeros_like(wq3[..., :MLA_NOPE]), hi, lo], axis=-1)
        wqq = jnp.concatenate([_pad_heads(w_q_b[l], MLA_QK),
                               _pad_heads(wq_sw.reshape(MLA_Q_LORA, -1), MLA_QK)], axis=1).astype(BF16)
        wkv3 = w_kv_b[l].reshape(MLA_KV_LORA, MLA_HEADS, MLA_NOPE + MLA_V)
        wkv = jnp.concatenate([_pad_heads(wkv3[:, :, :MLA_NOPE].reshape(MLA_KV_LORA, -1), MLA_NOPE),
                               _pad_heads_alternating(wkv3[:, :, MLA_NOPE:])], axis=1).astype(BF16)
        gq = q_norm_a[l] * (MLA_QK ** -0.5 * LOG2E)
        gk = k_norm_a[l]
        score_bound = MLA_QK * jnp.max(jnp.abs(gq)) * jnp.max(jnp.abs(gk))
        gains = jnp.stack([
            jnp.tile(_pad_lanes(gq), 2), jnp.tile(_pad_lanes(_swap_rope(gq)), 2),
            jnp.tile(_pad_lanes(gk), 2), jnp.tile(_pad_lanes(_swap_rope(gk)), 2),
            jnp.tile(q_norm_b[l] * (DIL_HEAD_DIM ** -0.5 * LOG2E), 4), jnp.tile(k_norm_b[l], 4),
            -score_bound * spare_qk, spare_qk, spare_v] + [jnp.zeros((2 * LANES,), F32)] * 7)
        g1 = norm1_g[l].reshape(1, d)
        g2 = norm2_g[l].reshape(1, d)

        q, k, v, qkvd, *x_cls = _pre_call(x, mod[l], g1, cos_t, sin_t, w_pre, q_a_norm[l].reshape(1, -1), wqq,
                                          kv_a_norm[l].reshape(1, -1), wkv, gains, ones_mla, ones_dil, tm=PRE_TM,
                                          natural_in=(l == 0))
        if x_cls:
            x, = x_cls
        ya = lax.cond(score_bound <= MLA_SAFE_BOUND,
                      functools.partial(_mla_call, tiles=MLA_BOUNDED_TILES, bounded=True),
                      functools.partial(_mla_call, tiles=MLA_ONLINE_TILES, bounded=False), q, k, v)
        yb = lax.cond(dil_range <= DIL_SAFE_RANGE,
                      lambda a, shifted, plain: _dil_call(a, shifted, bounded=True),
                      lambda a, shifted, plain: _dil_call(a, plain, bounded=False),
                      qkvd, bias_tabs_shifted, bias_tabs)
        x = _post_call(x, ya, yb, mod[l], g1, w_gates, w_branch_a[l].astype(BF16), w_branch_b[l].astype(BF16),
                       w_out[l].astype(BF16), tm=POST_TM)
        x = _ffn_call(x, mod[l], g2, w_ffn_gate[l].astype(BF16), w_ffn_up[l].astype(BF16),
                      w_ffn_down[l].astype(BF16), tm=FFN_TM, natural_out=(l == depth - 1))
    return x
```

```python
import functools
import math

import jax
import jax.numpy as jnp
import numpy as np
from jax import lax
from jax.experimental import pallas as pl
from jax.experimental.pallas import tpu as pltpu

F32 = jnp.float32
BF16 = jnp.bfloat16

LANES = 128
V7X_VMEM_BYTES = 64 * 1024 * 1024
VMEM_LIMIT_BYTES = V7X_VMEM_BYTES - 8 * 1024 * 1024

MLA_HEADS = 8
MLA_Q_LORA = 256
MLA_KV_LORA = 128
MLA_NOPE = 64
MLA_ROPE = 32
MLA_V = 64
MLA_QK = MLA_NOPE + MLA_ROPE
ROPE_THETA = 10000.0
DIL_HEADS = 8
DIL_HEAD_DIM = 64
DIL_PATTERNS = ((128, 1), (512, 4), (2048, 16))
DIL_HALF = 64
REL_BUCKETS = 32
REL_MAX_DIST = 1024
EPS = 1e-6
NEG_INF = -1e30
LOG2E = math.log2(math.e)
MLA_SAFE_BOUND = 60.0
MLA_SUB = 512

PRE_TM = 512
POST_TM = 1024
FFN_TM = 1024
MLA_BOUNDED_TILES = (4096, 256)
MLA_ONLINE_TILES = (512, 1024)

_C_QC = 0
_C_KVC = _C_QC + MLA_Q_LORA
_C_KR = _C_KVC + MLA_KV_LORA
_C_QKVB = _C_KR + MLA_ROPE
_C_GATE = _C_QKVB + 3 * DIL_HEADS * DIL_HEAD_DIM
PRE_COLS = MLA_Q_LORA + MLA_KV_LORA + LANES + 3 * DIL_HEADS * DIL_HEAD_DIM
HEAD_PAD = LANES

DIL_BQ = 2 * DIL_HALF
DIL_BK = 4 * DIL_HALF
DIL_CLASSES = 4
DIL_GROUP = 3 * LANES
DIL_SAFE_RANGE = 120.0


def _cparams(sem):
    return pltpu.CompilerParams(dimension_semantics=sem, vmem_limit_bytes=VMEM_LIMIT_BYTES)


def _resident(shape, index_map):
    return pl.BlockSpec(shape, index_map, pipeline_mode=pl.Buffered(1))


def _ada_kernel(c_ref, w_ref, b_ref, o_ref):
    c = c_ref[...]
    act = (c * jax.nn.sigmoid(c)).astype(BF16)
    o_ref[0] = jnp.dot(act, w_ref[0].astype(BF16), preferred_element_type=F32) + b_ref[0]


def _ada_call(c, ada_w, ada_b):
    depth, d, six_d = ada_w.shape
    b = c.shape[0]
    cols = 2 * d
    return pl.pallas_call(
        _ada_kernel,
        grid=(depth, six_d // cols),
        in_specs=[
            pl.BlockSpec((b, d), lambda l, j: (0, 0)),
            pl.BlockSpec((1, d, cols), lambda l, j: (l, 0, j)),
            pl.BlockSpec((1, 1, cols), lambda l, j: (l, 0, j)),
        ],
        out_specs=pl.BlockSpec((1, b, cols), lambda l, j: (l, 0, j)),
        out_shape=jax.ShapeDtypeStruct((depth, b, six_d), F32),
        compiler_params=_cparams(("arbitrary", "arbitrary")),
        name="ada_mod",
    )(c, ada_w, ada_b.reshape(depth, 1, six_d))


def _rope_kernel(pos_ref, inv_ref, c_ref, s_ref):
    ang = pos_ref[0].astype(F32) * inv_ref[...]
    cosv = jnp.cos(ang)
    sinv = jnp.sin(ang)
    lane = lax.broadcasted_iota(jnp.int32, ang.shape, 1)
    half = MLA_ROPE // 2
    c_ref[0] = jnp.where(lane < MLA_NOPE, 1.0, jnp.where(lane < MLA_QK, cosv, 0.0))
    s_ref[0] = jnp.where(lane < MLA_NOPE, 0.0,
                         jnp.where(lane < MLA_NOPE + half, -sinv, jnp.where(lane < MLA_QK, sinv, 0.0)))


def _rope_call(positions):
    b, s = positions.shape
    half = MLA_ROPE // 2
    inv = ROPE_THETA ** (-jnp.arange(half, dtype=F32) / half)
    inv_lane = jnp.zeros((1, LANES), F32)
    inv_lane = inv_lane.at[0, MLA_NOPE:MLA_NOPE + half].set(inv).at[0, MLA_NOPE + half:MLA_QK].set(inv)
    ts = 512
    spec = pl.BlockSpec((1, ts, LANES), lambda i, j: (i, j, 0))
    return pl.pallas_call(
        _rope_kernel,
        grid=(b, s // ts),
        in_specs=[pl.BlockSpec((1, ts, 1), lambda i, j: (i, j, 0)),
                  pl.BlockSpec((1, LANES), lambda i, j: (0, 0))],
        out_specs=[spec, spec],
        out_shape=[jax.ShapeDtypeStruct((b, s, LANES), F32)] * 2,
        compiler_params=_cparams(("arbitrary", "arbitrary")),
        name="rope_tables",
    )(positions.reshape(b, s, 1), inv_lane)


def _t5_bucket(rel):
    nb = REL_BUCKETS // 2
    max_exact = nb // 2
    ret = jnp.where(rel > 0, nb, 0)
    n = jnp.abs(rel)
    nf = jnp.maximum(n, 1).astype(F32)
    large = max_exact + (jnp.log(nf / max_exact) / math.log(REL_MAX_DIST / max_exact) * (nb - max_exact)).astype(jnp.int32)
    large = jnp.minimum(large, nb - 1)
    return ret + jnp.where(n < max_exact, n, large)


def _bias_kernel(bucket_ref, rb_ref, shift_ref, o_ref):
    h = pl.program_id(1)
    bucket = bucket_ref[0]
    acc = jnp.full(bucket.shape, NEG_INF, F32)
    for bkt in range(REL_BUCKETS):
        acc = jnp.where(bucket == bkt, rb_ref[bkt, h], acc)
    o_ref[0, 0] = acc * LOG2E - shift_ref[0]


def _bias_call(rel_bias, shift):
    tabs = []
    for _, dil in DIL_PATTERNS:
        qi, kj = np.arange(DIL_BQ), np.arange(DIL_BK)
        if dil == 1:
            qi = DIL_CLASSES * (qi % (DIL_BQ // DIL_CLASSES)) + qi // (DIL_BQ // DIL_CLASSES)
            kj = DIL_CLASSES * (kj % (DIL_BK // DIL_CLASSES)) + kj // (DIL_BK // DIL_CLASSES)
        qi = jnp.asarray(qi, jnp.int32)[:, None]
        kj = jnp.asarray(kj, jnp.int32)[None, :]
        for variant in range(3):
            rel = kj - variant * DIL_HALF - qi
            tabs.append(jnp.where(jnp.abs(rel) <= DIL_HALF, _t5_bucket(rel * dil), -1))
    bucket = jnp.stack(tabs).astype(jnp.int32)
    n_tab = bucket.shape[0]
    return pl.pallas_call(
        _bias_kernel,
        grid=(n_tab, DIL_HEADS),
        in_specs=[pl.BlockSpec((1, DIL_BQ, DIL_BK), lambda t, h: (t, 0, 0)),
                  pl.BlockSpec(memory_space=pltpu.SMEM), pl.BlockSpec(memory_space=pltpu.SMEM)],
        out_specs=pl.BlockSpec((1, 1, DIL_BQ, DIL_BK), lambda t, h: (t, h, 0, 0)),
        out_shape=jax.ShapeDtypeStruct((n_tab, DIL_HEADS, DIL_BQ, DIL_BK), F32),
        compiler_params=_cparams(("arbitrary", "arbitrary")),
        name="bias_tables",
    )(bucket, rel_bias, jnp.reshape(shift, (1,)).astype(F32))


def _modulated_norm(x, gain, shift, scale):
    r = lax.rsqrt(jnp.mean(x * x, axis=-1, keepdims=True) + EPS)
    return ((x * r) * gain) * (1.0 + scale) + shift


PRE_SUB = 256


def _group_sumsq(t, ones_ref):
    return jnp.dot((t * t).astype(BF16), ones_ref[...], preferred_element_type=F32)


def _class_slabs_load(slab_ref, tile, cls):
    rows = tile // DIL_CLASSES
    return jnp.concatenate([slab_ref[c, pl.ds(cls, rows, stride=DIL_CLASSES), :] for c in range(slab_ref.shape[0])],
                           axis=1)


def _pre_kernel(x_ref, mod_ref, g1_ref, cos_ref, sin_ref, w_ref, qan_ref, wqq_ref, kvan_ref, wkv_ref,
                gains_ref, ones_mla_ref, ones_dil_ref, *rest, natural_in):
    if natural_in:
        q_out, k_out, v_out, qkvd_out, xp_out, slab_ref = rest
        tm = x_ref.shape[1]
        for c in range(slab_ref.shape[0]):
            slab_ref[c] = x_ref[0, :, c * LANES:(c + 1) * LANES]
    else:
        q_out, k_out, v_out, qkvd_out = rest
        tm = x_ref.shape[1] * x_ref.shape[2]
    cls_rows = tm // DIL_CLASSES
    cls_per_sub = PRE_SUB // cls_rows

    def class_rows(ref, sub):
        return jnp.concatenate([ref[0, sub * cls_per_sub + j] for j in range(cls_per_sub)], axis=0)

    def store(ref, sub, sl, val):
        for j in range(cls_per_sub):
            ref[0, sub * cls_per_sub + j, :, sl] = val[j * cls_rows:(j + 1) * cls_rows]

    hp = MLA_HEADS * HEAD_PAD
    n_dil = DIL_HEADS * DIL_HEAD_DIM
    grp = 2 * LANES
    c0 = MLA_Q_LORA
    c1 = c0 + MLA_KV_LORA
    c2 = c1 + LANES
    half = MLA_ROPE // 2
    inv_qk = 1.0 / MLA_QK
    inv_dh = 1.0 / DIL_HEAD_DIM
    lane = lax.broadcasted_iota(jnp.int32, (PRE_SUB, LANES), 1)
    low_half = lane < MLA_NOPE + half

    for sub in range(tm // PRE_SUB):
        if natural_in:
            pieces = [_class_slabs_load(slab_ref, tm, sub * cls_per_sub + j) for j in range(cls_per_sub)]
            for j, piece in enumerate(pieces):
                xp_out[0, sub * cls_per_sub + j] = piece
            x = jnp.concatenate(pieces, axis=0)
        else:
            x = class_rows(x_ref, sub)
        h = _modulated_norm(x, g1_ref[...], mod_ref[0, 0:1, :], mod_ref[0, 1:2, :])
        proj = jnp.dot(h.astype(BF16), w_ref[...], preferred_element_type=F32)

        qc = proj[:, :c0]
        qcn = (qc * lax.rsqrt(jnp.mean(qc * qc, axis=-1, keepdims=True) + EPS)) * qan_ref[...]
        qq = jnp.dot(qcn.astype(BF16), wqq_ref[...], preferred_element_type=F32)
        kvc = proj[:, c0:c1]
        kvn = (kvc * lax.rsqrt(jnp.mean(kvc * kvc, axis=-1, keepdims=True) + EPS)) * kvan_ref[...]
        kv = jnp.dot(kvn.astype(BF16), wkv_ref[...], preferred_element_type=F32)
        k_rope = proj[:, c1:c2]
        k_rope_sw = jnp.where(low_half, pltpu.roll(k_rope, LANES - half, 1), pltpu.roll(k_rope, half, 1))

        cos2 = jnp.concatenate([class_rows(cos_ref, sub)] * 2, axis=1)
        sin2 = jnp.concatenate([class_rows(sin_ref, sub)] * 2, axis=1)
        aq = gains_ref[0:1, :] * cos2
        bq = gains_ref[1:2, :] * sin2
        ak = gains_ref[2:3, :] * cos2
        bk = gains_ref[3:4, :] * sin2
        kr2 = jnp.concatenate([k_rope] * 2, axis=1)
        krs2 = jnp.concatenate([k_rope_sw] * 2, axis=1)
        for g in range(hp // grp):
            sl = slice(g * grp, (g + 1) * grp)
            qh = qq[:, sl]
            rq = lax.rsqrt(_group_sumsq(qh, ones_mla_ref) * inv_qk + EPS)
            store(q_out, sub, sl, (rq * (qh * aq + qq[:, hp + g * grp:hp + (g + 1) * grp] * bq)
                                   + gains_ref[6:7, :]).astype(BF16))
            kh = kv[:, sl] + kr2
            rk = lax.rsqrt(_group_sumsq(kh, ones_mla_ref) * inv_qk + EPS)
            store(k_out, sub, sl, (rk * (kh * ak + krs2 * bk) + gains_ref[7:8, :]).astype(BF16))
            store(v_out, sub, sl, (kv[:, hp + g * grp:hp + (g + 1) * grp] + gains_ref[8:9, :]).astype(BF16))

        for g in range(n_dil // grp):
            qd = proj[:, c2 + g * grp:c2 + (g + 1) * grp]
            qd = ((qd * lax.rsqrt(_group_sumsq(qd, ones_dil_ref) * inv_dh + EPS)) * gains_ref[4:5, :]).astype(BF16)
            kd = proj[:, c2 + n_dil + g * grp:c2 + n_dil + (g + 1) * grp]
            kd = ((kd * lax.rsqrt(_group_sumsq(kd, ones_dil_ref) * inv_dh + EPS)) * gains_ref[5:6, :]).astype(BF16)
            vd = proj[:, c2 + 2 * n_dil + g * grp:c2 + 2 * n_dil + (g + 1) * grp].astype(BF16)
            for j in range(grp // LANES):
                base = (g * (grp // LANES) + j) * DIL_GROUP
                for t, val in enumerate((qd, kd, vd)):
                    store(qkvd_out, sub, slice(base + t * LANES, base + (t + 1) * LANES), val[:, j * LANES:(j + 1) * LANES])


def _pre_call(x, mod_l, g1, cos_t, sin_t, w_pre, qan, wqq, kvan, wkv, gains, ones_mla, ones_dil, tm, natural_in):
    b, s, d = x.shape
    hp = MLA_HEADS * HEAD_PAD
    n_dil = DIL_HEADS * DIL_HEAD_DIM
    cls_len = s // DIL_CLASSES
    cls_rows = tm // DIL_CLASSES
    cls_view = lambda t: t.reshape(b, DIL_CLASSES, cls_len, t.shape[-1])
    cls_spec = lambda cols: pl.BlockSpec((1, DIL_CLASSES, cls_rows, cols), lambda i, j: (i, 0, j, 0))
    const2 = lambda i, j: (0, 0)
    row = lambda n: _resident((1, n), const2)
    out_cols = [hp, hp, hp, 3 * n_dil] + ([d] if natural_in else [])
    out_dtypes = [BF16] * 4 + ([F32] if natural_in else [])
    outs = pl.pallas_call(
        functools.partial(_pre_kernel, natural_in=natural_in),
        grid=(b, s // tm),
        in_specs=[
            pl.BlockSpec((1, tm, d), lambda i, j: (i, j, 0)) if natural_in else cls_spec(d),
            pl.BlockSpec((1, 6, d), lambda i, j: (i, 0, 0)),
            row(d),
            cls_spec(LANES),
            cls_spec(LANES),
            _resident((d, PRE_COLS), const2),
            row(MLA_Q_LORA),
            _resident((MLA_Q_LORA, 2 * hp), const2),
            row(MLA_KV_LORA),
            _resident((MLA_KV_LORA, 2 * hp), const2),
            _resident(gains.shape, const2),
            _resident(ones_mla.shape, const2),
            _resident(ones_dil.shape, const2),
        ],
        out_specs=[cls_spec(c) for c in out_cols],
        out_shape=[jax.ShapeDtypeStruct((b, DIL_CLASSES, cls_len, c), dt) for c, dt in zip(out_cols, out_dtypes)],
        scratch_shapes=[pltpu.VMEM((d // LANES, tm, LANES), F32)] if natural_in else [],
        compiler_params=_cparams(("arbitrary", "arbitrary")),
        name="pre_proj_first" if natural_in else "pre_proj",
    )(x if natural_in else cls_view(x), mod_l, g1, cls_view(cos_t), cls_view(sin_t), w_pre, qan, wqq, kvan, wkv,
      gains, ones_mla, ones_dil)
    return [o.reshape(b, s, o.shape[-1]) for o in outs]


def _pair_output(outs):
    o0, o1 = outs
    lane = lax.broadcasted_iota(jnp.int32, o0.shape, 1)
    return jnp.where(lane < MLA_V, o0 / o0[:, MLA_V:MLA_V + 1], o1 / o1[:, 0:1]).astype(BF16)


def _mla_online_kernel(q_ref, k_ref, v_ref, o_ref, *, tk):
    tq = q_ref.shape[1]
    n_k = k_ref.shape[1] // tk
    outs = []
    for hh in range(2):
        hl = slice(hh * HEAD_PAD, (hh + 1) * HEAD_PAD)
        q = q_ref[0, :, hl]

        def body(j, carry, q=q, hl=hl):
            m, acc = carry
            ks = pl.multiple_of(j * tk, tk)
            k = k_ref[0, pl.ds(ks, tk), hl]
            v = v_ref[0, pl.ds(ks, tk), hl]
            s = lax.dot_general(q, k, (((1,), (1,)), ((), ())), preferred_element_type=F32)
            m_new = jnp.maximum(m, jnp.max(s, axis=-1, keepdims=True))
            p = jnp.exp2(s - m_new)
            acc = jnp.exp2(m - m_new) * acc + jnp.dot(p.astype(BF16), v, preferred_element_type=F32)
            return m_new, acc

        init = (jnp.full((tq, 1), -jnp.inf, F32), jnp.zeros((tq, HEAD_PAD), F32))
        outs.append(lax.fori_loop(0, n_k, body, init, unroll=True)[1])
    o_ref[0] = _pair_output(outs)


def _mla_bounded_kernel(q_ref, k_ref, v_ref, o_ref, *, tk):
    n_k = k_ref.shape[1] // tk
    for qs in range(q_ref.shape[1] // MLA_SUB):
        rows = slice(qs * MLA_SUB, (qs + 1) * MLA_SUB)
        outs = []
        for hh in range(2):
            hl = slice(hh * HEAD_PAD, (hh + 1) * HEAD_PAD)
            q = q_ref[0, rows, hl]
            acc = jnp.zeros((MLA_SUB, HEAD_PAD), F32)
            for j in range(n_k):
                k = k_ref[0, j * tk:(j + 1) * tk, hl]
                v = v_ref[0, j * tk:(j + 1) * tk, hl]
                s = lax.dot_general(q, k, (((1,), (1,)), ((), ())), preferred_element_type=F32)
                acc = acc + jnp.dot(jnp.exp2(s).astype(BF16), v, preferred_element_type=F32)
            outs.append(acc)
        o_ref[0, rows, :] = _pair_output(outs)


def _mla_call(q, k, v, tiles, bounded):
    tq, tk = tiles
    b, s, width = q.shape
    pair = 2 * HEAD_PAD
    n_pair = width // pair
    return pl.pallas_call(
        functools.partial(_mla_bounded_kernel if bounded else _mla_online_kernel, tk=tk),
        grid=(b, n_pair, s // tq),
        in_specs=[
            pl.BlockSpec((1, tq, pair), lambda i, h, j: (i, j, h)),
            pl.BlockSpec((1, s, pair), lambda i, h, j: (i, 0, h)),
            pl.BlockSpec((1, s, pair), lambda i, h, j: (i, 0, h)),
        ],
        out_specs=pl.BlockSpec((1, tq, 2 * MLA_V), lambda i, h, j: (i, j, h)),
        out_shape=jax.ShapeDtypeStruct((b, s, n_pair * 2 * MLA_V), BF16),
        compiler_params=_cparams(("arbitrary", "arbitrary", "arbitrary")),
        name="mla_attn_bounded" if bounded else "mla_attn_online",
    )(q, k, v)


def _dil_kernel(x_ref, bias_ref, o_ref, qf, kf, vf, *states, bounded):
    s_len = o_ref.shape[1]
    cls_len = s_len // DIL_CLASSES
    piece_q = DIL_BQ // DIL_CLASSES
    piece_k = DIL_BK // DIL_CLASSES
    lanes = (slice(0, LANES), slice(LANES, 2 * LANES), slice(2 * LANES, 3 * LANES))
    lane = lax.broadcasted_iota(jnp.int32, (DIL_BQ, LANES), 1)
    first = lane < DIL_HEAD_DIM
    for dst, sl in zip((qf, kf, vf), lanes):
        dst[...] = x_ref[0, :, sl].astype(F32)

    def band_tile(q_t, k_t, v_t, tab):
        zero = jnp.zeros_like(q_t)
        qm = jnp.concatenate([jnp.where(first, q_t, zero), jnp.where(first, zero, q_t)], axis=0)
        s = lax.dot_general(qm, k_t, (((1,), (1,)), ((), ())), preferred_element_type=F32)
        s = s + bias_ref[tab].reshape(2 * DIL_BQ, DIL_BK)
        if not bounded:
            m = jnp.max(s, axis=-1, keepdims=True)
            s = s - m
        p = jnp.exp2(s)
        l = jnp.sum(p, axis=-1, keepdims=True)
        pv = jnp.dot(p.astype(BF16), v_t, preferred_element_type=F32)
        out = (jnp.where(first, l[:DIL_BQ], l[DIL_BQ:]), jnp.where(first, pv[:DIL_BQ], pv[DIL_BQ:]))
        return out if bounded else (jnp.where(first, m[:DIL_BQ], m[DIL_BQ:]),) + out

    def merge(old, new):
        if bounded:
            return tuple(o + n for o, n in zip(old, new))
        m_n = jnp.maximum(old[0], new[0])
        a = jnp.exp2(old[0] - m_n)
        c = jnp.exp2(new[0] - m_n)
        return (m_n, a * old[1] + c * new[1], a * old[2] + c * new[2])

    def variant_of(n, n_blk):
        return 0 if n == 0 else (2 if n == n_blk - 1 else 1)

    p_idx = 1
    n_blk = cls_len // DIL_BQ
    for r in range(DIL_CLASSES):
        for n in range(n_blk):
            i0 = r * cls_len + n * DIL_BQ
            ks = r * cls_len + min(max(n * DIL_BQ - DIL_HALF, 0), cls_len - DIL_BK)
            vals = band_tile(x_ref[0, i0:i0 + DIL_BQ, lanes[0]], x_ref[0, ks:ks + DIL_BK, lanes[1]],
                             x_ref[0, ks:ks + DIL_BK, lanes[2]], 3 * p_idx + variant_of(n, n_blk))
            for st, val in zip(states, vals):
                st[i0:i0 + DIL_BQ, :] = val

    p_idx = 0
    n_blk = s_len // DIL_BQ
    for n in range(n_blk):
        i0 = n * DIL_BQ // DIL_CLASSES
        ks = min(max(n * DIL_BQ - DIL_HALF, 0), s_len - DIL_BK) // DIL_CLASSES
        rows_q = [slice(r * cls_len + i0, r * cls_len + i0 + piece_q) for r in range(DIL_CLASSES)]
        rows_k = [slice(r * cls_len + ks, r * cls_len + ks + piece_k) for r in range(DIL_CLASSES)]
        vals = band_tile(jnp.concatenate([x_ref[0, rq, lanes[0]] for rq in rows_q], axis=0),
                         jnp.concatenate([x_ref[0, rk, lanes[1]] for rk in rows_k], axis=0),
                         jnp.concatenate([x_ref[0, rk, lanes[2]] for rk in rows_k], axis=0),
                         3 * p_idx + variant_of(n, n_blk))
        for r, rq in enumerate(rows_q):
            piece = slice(r * piece_q, (r + 1) * piece_q)
            merged = merge(tuple(st[rq, :] for st in states), tuple(val[piece] for val in vals))
            for st, val in zip(states, merged):
                st[rq, :] = val

    p_idx = 2
    ratio = DIL_PATTERNS[p_idx][1] // DIL_CLASSES
    sub_len = cls_len // ratio
    n_blk = sub_len // DIL_BQ
    for r in range(DIL_CLASSES):
        for c in range(ratio):
            for n in range(n_blk):
                ks = min(max(n * DIL_BQ - DIL_HALF, 0), sub_len - DIL_BK)
                rows_q = pl.ds(r * cls_len + c + ratio * n * DIL_BQ, DIL_BQ, stride=ratio)
                rows_k = pl.ds(r * cls_len + c + ratio * ks, DIL_BK, stride=ratio)
                vals = band_tile(qf[rows_q, :].astype(BF16), kf[rows_k, :].astype(BF16),
                                 vf[rows_k, :].astype(BF16), 3 * p_idx + variant_of(n, n_blk))
                merged = merge(tuple(st[rows_q, :] for st in states), vals)
                for st, val in zip(states, merged):
                    st[rows_q, :] = val

    o_ref[0] = (states[-1][...] / states[-2][...]).astype(BF16)


def _dil_call(qkvd, bias_tabs, bounded):
    b, s, width = qkvd.shape
    n_pair = width // DIL_GROUP
    n_tab = bias_tabs.shape[0]
    n_state = 2 if bounded else 3
    return pl.pallas_call(
        functools.partial(_dil_kernel, bounded=bounded),
        grid=(n_pair, b),
        in_specs=[pl.BlockSpec((1, s, DIL_GROUP), lambda hp, i: (i, 0, hp)),
                  pl.BlockSpec((n_tab, 2, DIL_BQ, DIL_BK), lambda hp, i: (0, hp, 0, 0))],
        out_specs=pl.BlockSpec((1, s, LANES), lambda hp, i: (i, 0, hp)),
        out_shape=jax.ShapeDtypeStruct((b, s, n_pair * LANES), BF16),
        scratch_shapes=[pltpu.VMEM((s, LANES), F32)] * (3 + n_state),
        compiler_params=_cparams(("arbitrary", "arbitrary")),
        name="dil_attn_bounded" if bounded else "dil_attn_online",
    )(qkvd, bias_tabs)


def _post_kernel(x_ref, ya_ref, yb_ref, mod_ref, g1_ref, wg_ref, wa_ref, wb_ref, wo_ref, o_ref):
    x = x_ref[0]
    h = _modulated_norm(x, g1_ref[...], mod_ref[0, 0:1, :], mod_ref[0, 1:2, :])
    gates = jnp.dot(h.astype(BF16), wg_ref[...], preferred_element_type=F32)
    d = x.shape[-1]
    a = jnp.dot(ya_ref[0], wa_ref[...], preferred_element_type=F32)
    bb = jnp.dot(yb_ref[0], wb_ref[...], preferred_element_type=F32)
    merged = jax.nn.sigmoid(gates[:, :d]) * a + jax.nn.sigmoid(gates[:, d:]) * bb
    upd = jnp.dot(merged.astype(BF16), wo_ref[...], preferred_element_type=F32)
    o_ref[0] = x + mod_ref[0, 2:3, :] * upd


def _post_call(x, ya, yb, mod_l, g1, wg, wa, wb, wo, tm):
    b, s, d = x.shape
    tok = lambda i, j: (i, j, 0)
    const2 = lambda i, j: (0, 0)
    return pl.pallas_call(
        _post_kernel,
        grid=(b, s // tm),
        in_specs=[
            pl.BlockSpec((1, tm, d), tok),
            pl.BlockSpec((1, tm, ya.shape[-1]), tok),
            pl.BlockSpec((1, tm, yb.shape[-1]), tok),
            pl.BlockSpec((1, 6, d), lambda i, j: (i, 0, 0)),
            _resident((1, d), const2),
            _resident(wg.shape, const2),
            _resident(wa.shape, const2),
            _resident(wb.shape, const2),
            _resident(wo.shape, const2),
        ],
        out_specs=pl.BlockSpec((1, tm, d), tok),
        out_shape=jax.ShapeDtypeStruct(x.shape, F32),
        compiler_params=_cparams(("arbitrary", "arbitrary")),
        name="attn_out",
    )(x, ya, yb, mod_l, g1, wg, wa, wb, wo)


def _ffn_kernel(x_ref, mod_ref, g2_ref, wg_ref, wu_ref, wd_ref, o_ref, *scratch, natural_out):
    x = jnp.concatenate([x_ref[0, c] for c in range(DIL_CLASSES)], axis=0) if natural_out else x_ref[0]
    h = _modulated_norm(x, g2_ref[...], mod_ref[0, 3:4, :], mod_ref[0, 4:5, :]).astype(BF16)
    g = jnp.dot(h, wg_ref[...], preferred_element_type=F32)
    u = jnp.dot(h, wu_ref[...], preferred_element_type=F32)
    act = (g * jax.nn.sigmoid(g)) * u
    upd = jnp.dot(act.astype(BF16), wd_ref[...], preferred_element_type=F32)
    res = x + mod_ref[0, 5:6, :] * upd
    if not natural_out:
        o_ref[0] = res
        return
    slab_ref, = scratch
    cls_rows = res.shape[0] // DIL_CLASSES
    for cls in range(DIL_CLASSES):
        for c in range(slab_ref.shape[0]):
            slab_ref[c, pl.ds(cls, cls_rows, stride=DIL_CLASSES), :] = res[cls * cls_rows:(cls + 1) * cls_rows,
                                                                           c * LANES:(c + 1) * LANES]
    o_ref[0] = jnp.concatenate([slab_ref[c] for c in range(slab_ref.shape[0])], axis=1)


def _ffn_call(x, mod_l, g2, wg, wu, wd, tm, natural_out):
    b, s, d = x.shape
    tok = lambda i, j: (i, j, 0)
    const2 = lambda i, j: (0, 0)
    cls_rows = tm // DIL_CLASSES
    x_in = x.reshape(b, DIL_CLASSES, s // DIL_CLASSES, d) if natural_out else x
    return pl.pallas_call(
        functools.partial(_ffn_kernel, natural_out=natural_out),
        grid=(b, s // tm),
        in_specs=[
            pl.BlockSpec((1, DIL_CLASSES, cls_rows, d), lambda i, j: (i, 0, j, 0)) if natural_out
            else pl.BlockSpec((1, tm, d), tok),
            pl.BlockSpec((1, 6, d), lambda i, j: (i, 0, 0)),
            _resident((1, d), const2),
            _resident(wg.shape, const2),
            _resident(wu.shape, const2),
            _resident(wd.shape, const2),
        ],
        out_specs=pl.BlockSpec((1, tm, d), tok),
        out_shape=jax.ShapeDtypeStruct(x.shape, F32),
        scratch_shapes=[pltpu.VMEM((d // LANES, tm, LANES), F32)] if natural_out else [],
        compiler_params=_cparams(("arbitrary", "arbitrary")),
        name="swiglu_last" if natural_out else "swiglu",
    )(x_in, mod_l, g2, wg, wu, wd)


def _pad_heads(w, width):
    rows = w.shape[0]
    w = w.reshape(rows, MLA_HEADS, width)
    return jnp.pad(w, ((0, 0), (0, 0), (0, HEAD_PAD - width))).reshape(rows, MLA_HEADS * HEAD_PAD)


def _pad_heads_alternating(w):
    rows = w.shape[0]
    w = w.reshape(rows, MLA_HEADS // 2, 2, MLA_V)
    zero = jnp.zeros_like(w[:, :, 0])
    even = jnp.concatenate([w[:, :, 0], zero], axis=-1)
    odd = jnp.concatenate([zero, w[:, :, 1]], axis=-1)
    return jnp.stack([even, odd], axis=2).reshape(rows, MLA_HEADS * HEAD_PAD)


def _pad_lanes(g):
    return jnp.pad(g, (0, LANES - g.shape[0]))


def _swap_rope(g):
    half = MLA_ROPE // 2
    return jnp.concatenate([jnp.zeros((MLA_NOPE,), g.dtype), g[MLA_NOPE + half:], g[MLA_NOPE:MLA_NOPE + half]])


def _class_major(t, n_class):
    b, s = t.shape[:2]
    return jnp.swapaxes(t.reshape(b, s // n_class, n_class, *t.shape[2:]), 1, 2).reshape(t.shape)


def _block_ones(block, size=2 * LANES):
    idx = np.arange(size) // block
    return jnp.asarray(idx[:, None] == idx[None, :], BF16)


def kernel(x, c, positions, rel_bias, norm1_g, norm2_g, ada_w, ada_b, w_in, q_a_norm, w_q_b, kv_a_norm, w_kv_b,
           q_norm_a, k_norm_a, q_norm_b, k_norm_b, w_branch_a, w_branch_b, w_out, w_ffn_gate, w_ffn_up,
           w_ffn_down):
    depth = w_in.shape[0]
    b, s, d = x.shape
    positions = _class_major(positions, DIL_CLASSES)

    mod = _ada_call(c, ada_w, ada_b).reshape(depth, b, 6, d)
    cos_t, sin_t = _rope_call(positions)
    dil_qk_bound = DIL_HEAD_DIM * jnp.max(jnp.abs(q_norm_b * (DIL_HEAD_DIM ** -0.5 * LOG2E))) * jnp.max(jnp.abs(k_norm_b))
    dil_shift = dil_qk_bound + jnp.max(rel_bias) * LOG2E
    dil_range = 2.0 * dil_qk_bound + (jnp.max(rel_bias) - jnp.min(rel_bias)) * LOG2E
    bias_tabs = _bias_call(rel_bias, jnp.zeros((), F32))
    bias_tabs_shifted = _bias_call(rel_bias, dil_shift)
    ones_mla = _block_ones(HEAD_PAD)
    ones_dil = _block_ones(DIL_HEAD_DIM)
    half = MLA_ROPE // 2
    lane2 = np.arange(2 * LANES) % LANES
    spare_qk = jnp.asarray(lane2 == MLA_QK, F32)
    spare_v = jnp.asarray((np.arange(2 * LANES) == MLA_V) | (np.arange(2 * LANES) == LANES), F32)

    for l in range(depth):
        wl = w_in[l]
        k_rope_cols = jnp.pad(wl[:, _C_KR:_C_QKVB], ((0, 0), (MLA_NOPE, LANES - MLA_QK)))
        w_pre = jnp.concatenate([wl[:, _C_QC:_C_KR], k_rope_cols, wl[:, _C_QKVB:_C_GATE]], axis=1).astype(BF16)
        w_gates = wl[:, _C_GATE:].astype(BF16)
        wq3 = w_q_b[l].reshape(MLA_Q_LORA, MLA_HEADS, MLA_QK)
        lo, hi = wq3[..., MLA_NOPE:MLA_NOPE + half], wq3[..., MLA_NOPE + half:]
        wq_sw = jnp.concatenate([jnp.zeros_like(wq3[..., :MLA_NOPE]), hi, lo], axis=-1)
        wqq = jnp.concatenate([_pad_heads(w_q_b[l], MLA_QK),
                               _pad_heads(wq_sw.reshape(MLA_Q_LORA, -1), MLA_QK)], axis=1).astype(BF16)
        wkv3 = w_kv_b[l].reshape(MLA_KV_LORA, MLA_HEADS, MLA_NOPE + MLA_V)
        wkv = jnp.concatenate([_pad_heads(wkv3[:, :, :MLA_NOPE].reshape(MLA_KV_LORA, -1), MLA_NOPE),
                               _pad_heads_alternating(wkv3[:, :, MLA_NOPE:])], axis=1).astype(BF16)
        gq = q_norm_a[l] * (MLA_QK ** -0.5 * LOG2E)
        gk = k_norm_a[l]
        score_bound = MLA_QK * jnp.max(jnp.abs(gq)) * jnp.max(jnp.abs(gk))
        gains = jnp.stack([
            jnp.tile(_pad_lanes(gq), 2), jnp.tile(_pad_lanes(_swap_rope(gq)), 2),
            jnp.tile(_pad_lanes(gk), 2), jnp.tile(_pad_lanes(_swap_rope(gk)), 2),
            jnp.tile(q_norm_b[l] * (DIL_HEAD_DIM ** -0.5 * LOG2E), 4), jnp.tile(k_norm_b[l], 4),
            -score_bound * spare_qk, spare_qk, spare_v] + [jnp.zeros((2 * LANES,), F32)] * 7)
        g1 = norm1_g[l].reshape(1, d)
        g2 = norm2_g[l].reshape(1, d)

        q, k, v, qkvd, *x_cls = _pre_call(x, mod[l], g1, cos_t, sin_t, w_pre, q_a_norm[l].reshape(1, -1), wqq,
                                          kv_a_norm[l].reshape(1, -1), wkv, gains, ones_mla, ones_dil, tm=PRE_TM,
                                          natural_in=(l == 0))
        if x_cls:
            x, = x_cls
        ya = lax.cond(score_bound <= MLA_SAFE_BOUND,
                      functools.partial(_mla_call, tiles=MLA_BOUNDED_TILES, bounded=True),
                      functools.partial(_mla_call, tiles=MLA_ONLINE_TILES, bounded=False), q, k, v)
        yb = lax.cond(dil_range <= DIL_SAFE_RANGE,
                      lambda a, shifted, plain: _dil_call(a, shifted, bounded=True),
                      lambda a, shifted, plain: _dil_call(a, plain, bounded=False),
                      qkvd, bias_tabs_shifted, bias_tabs)
        x = _post_call(x, ya, yb, mod[l], g1, w_gates, w_branch_a[l].astype(BF16), w_branch_b[l].astype(BF16),
                       w_out[l].astype(BF16), tm=POST_TM)
        x = _ffn_call(x, mod[l], g2, w_ffn_gate[l].astype(BF16), w_ffn_up[l].astype(BF16),
                      w_ffn_down[l].astype(BF16), tm=FFN_TM, natural_out=(l == depth - 1))
    return x
```

```python
import functools
import math

import jax
import jax.numpy as jnp
import numpy as np
from jax import lax
from jax.experimental import pallas as pl
from jax.experimental.pallas import tpu as pltpu

F32 = jnp.float32
BF16 = jnp.bfloat16

LANES = 128
V7X_VMEM_BYTES = 64 * 1024 * 1024
VMEM_LIMIT_BYTES = V7X_VMEM_BYTES - 8 * 1024 * 1024

MLA_HEADS = 8
MLA_Q_LORA = 256
MLA_KV_LORA = 128
MLA_NOPE = 64
MLA_ROPE = 32
MLA_V = 64
MLA_QK = MLA_NOPE + MLA_ROPE
ROPE_THETA = 10000.0
DIL_HEADS = 8
DIL_HEAD_DIM = 64
DIL_PATTERNS = ((128, 1), (512, 4), (2048, 16))
DIL_HALF = 64
REL_BUCKETS = 32
REL_MAX_DIST = 1024
EPS = 1e-6
NEG_INF = -1e30
LOG2E = math.log2(math.e)
MLA_SAFE_BOUND = 60.0
MLA_SUB = 512

PRE_TM = 512
POST_TM = 1024
FFN_TM = 1024
MLA_BOUNDED_TILES = (4096, 256)
MLA_ONLINE_TILES = (512, 1024)

_C_QC = 0
_C_KVC = _C_QC + MLA_Q_LORA
_C_KR = _C_KVC + MLA_KV_LORA
_C_QKVB = _C_KR + MLA_ROPE
_C_GATE = _C_QKVB + 3 * DIL_HEADS * DIL_HEAD_DIM
PRE_COLS = MLA_Q_LORA + MLA_KV_LORA + LANES + 3 * DIL_HEADS * DIL_HEAD_DIM
HEAD_PAD = LANES

DIL_BQ = 2 * DIL_HALF
DIL_BK = 4 * DIL_HALF
DIL_CLASSES = 4
DIL_GROUP = 3 * LANES
DIL_SAFE_RANGE = 120.0


def _cparams(sem):
    return pltpu.CompilerParams(dimension_semantics=sem, vmem_limit_bytes=VMEM_LIMIT_BYTES)


def _resident(shape, index_map):
    return pl.BlockSpec(shape, index_map, pipeline_mode=pl.Buffered(1))


def _ada_kernel(c_ref, w_ref, b_ref, o_ref):
    c = c_ref[...]
    act = (c * jax.nn.sigmoid(c)).astype(BF16)
    o_ref[0] = jnp.dot(act, w_ref[0].astype(BF16), preferred_element_type=F32) + b_ref[0]


def _ada_call(c, ada_w, ada_b):
    depth, d, six_d = ada_w.shape
    b = c.shape[0]
    cols = 2 * d
    return pl.pallas_call(
        _ada_kernel,
        grid=(depth, six_d // cols),
        in_specs=[
            pl.BlockSpec((b, d), lambda l, j: (0, 0)),
            pl.BlockSpec((1, d, cols), lambda l, j: (l, 0, j)),
            pl.BlockSpec((1, 1, cols), lambda l, j: (l, 0, j)),
        ],
        out_specs=pl.BlockSpec((1, b, cols), lambda l, j: (l, 0, j)),
        out_shape=jax.ShapeDtypeStruct((depth, b, six_d), F32),
        compiler_params=_cparams(("arbitrary", "arbitrary")),
        name="ada_mod",
    )(c, ada_w, ada_b.reshape(depth, 1, six_d))


ROPE_PACK = LANES // MLA_ROPE


def _rope_kernel(pos_ref, inv_ref, c_ref, s_ref):
    ts = pos_ref.shape[1]
    rows = ts // ROPE_PACK
    half = MLA_ROPE // 2
    lane = lax.broadcasted_iota(jnp.int32, (rows, LANES), 1)
    pos = pos_ref[0].astype(F32)
    ang = jnp.zeros((rows, LANES), F32)
    for g in range(ROPE_PACK):
        ang = jnp.where((lane >= g * MLA_ROPE) & (lane < (g + 1) * MLA_ROPE),
                        pos[g * rows:(g + 1) * rows] * inv_ref[...], ang)
    cosv = jnp.cos(ang)
    sinv = jnp.sin(ang)
    for g in range(ROPE_PACK):
        shift = (MLA_NOPE - g * MLA_ROPE) % LANES
        cg = pltpu.roll(cosv, shift, 1) if shift else cosv
        sg = pltpu.roll(sinv, shift, 1) if shift else sinv
        c_ref[0, g * rows:(g + 1) * rows, :] = jnp.where(lane < MLA_NOPE, 1.0, jnp.where(lane < MLA_QK, cg, 0.0))
        s_ref[0, g * rows:(g + 1) * rows, :] = jnp.where(
            lane < MLA_NOPE, 0.0, jnp.where(lane < MLA_NOPE + half, -sg, jnp.where(lane < MLA_QK, sg, 0.0)))


def _rope_call(positions):
    b, s = positions.shape
    half = MLA_ROPE // 2
    inv = ROPE_THETA ** (-jnp.arange(half, dtype=F32) / half)
    inv_lane = jnp.tile(inv, 2 * ROPE_PACK).reshape(1, LANES)
    ts = 512
    spec = pl.BlockSpec((1, ts, LANES), lambda i, j: (i, j, 0))
    return pl.pallas_call(
        _rope_kernel,
        grid=(b, s // ts),
        in_specs=[pl.BlockSpec((1, ts, 1), lambda i, j: (i, j, 0)),
                  pl.BlockSpec((1, LANES), lambda i, j: (0, 0))],
        out_specs=[spec, spec],
        out_shape=[jax.ShapeDtypeStruct((b, s, LANES), F32)] * 2,
        compiler_params=_cparams(("arbitrary", "arbitrary")),
        name="rope_tables",
    )(positions.reshape(b, s, 1), inv_lane)


def _t5_bucket(rel):
    nb = REL_BUCKETS // 2
    max_exact = nb // 2
    ret = jnp.where(rel > 0, nb, 0)
    n = jnp.abs(rel)
    nf = jnp.maximum(n, 1).astype(F32)
    large = max_exact + (jnp.log(nf / max_exact) / math.log(REL_MAX_DIST / max_exact) * (nb - max_exact)).astype(jnp.int32)
    large = jnp.minimum(large, nb - 1)
    return ret + jnp.where(n < max_exact, n, large)


def _bias_kernel(bucket_ref, rb_ref, shift_ref, o_ref):
    h = pl.program_id(1)
    bucket = bucket_ref[0]
    acc = jnp.full(bucket.shape, NEG_INF, F32)
    for bkt in range(REL_BUCKETS):
        acc = jnp.where(bucket == bkt, rb_ref[bkt, h], acc)
    o_ref[0, 0] = acc * LOG2E - shift_ref[0]


def _bias_call(rel_bias, shift):
    tabs = []
    for _, dil in DIL_PATTERNS:
        qi, kj = np.arange(DIL_BQ), np.arange(DIL_BK)
        if dil == 1:
            qi = DIL_CLASSES * (qi % (DIL_BQ // DIL_CLASSES)) + qi // (DIL_BQ // DIL_CLASSES)
            kj = DIL_CLASSES * (kj % (DIL_BK // DIL_CLASSES)) + kj // (DIL_BK // DIL_CLASSES)
        qi = jnp.asarray(qi, jnp.int32)[:, None]
        kj = jnp.asarray(kj, jnp.int32)[None, :]
        for variant in range(3):
            rel = kj - variant * DIL_HALF - qi
            tabs.append(jnp.where(jnp.abs(rel) <= DIL_HALF, _t5_bucket(rel * dil), -1))
    bucket = jnp.stack(tabs).astype(jnp.int32)
    n_tab = bucket.shape[0]
    return pl.pallas_call(
        _bias_kernel,
        grid=(n_tab, DIL_HEADS),
        in_specs=[pl.BlockSpec((1, DIL_BQ, DIL_BK), lambda t, h: (t, 0, 0)),
                  pl.BlockSpec(memory_space=pltpu.SMEM), pl.BlockSpec(memory_space=pltpu.SMEM)],
        out_specs=pl.BlockSpec((1, 1, DIL_BQ, DIL_BK), lambda t, h: (t, h, 0, 0)),
        out_shape=jax.ShapeDtypeStruct((n_tab, DIL_HEADS, DIL_BQ, DIL_BK), F32),
        compiler_params=_cparams(("arbitrary", "arbitrary")),
        name="bias_tables",
    )(bucket, rel_bias, jnp.reshape(shift, (1,)).astype(F32))


def _modulated_norm(x, gain, shift, scale):
    r = lax.rsqrt(jnp.mean(x * x, axis=-1, keepdims=True) + EPS)
    return ((x * r) * gain) * (1.0 + scale) + shift


PRE_SUB = 256


def _group_sumsq(t, ones_ref):
    return jnp.dot((t * t).astype(BF16), ones_ref[...], preferred_element_type=F32)


def _class_slabs_load(slab_ref, tile, cls):
    rows = tile // DIL_CLASSES
    return jnp.concatenate([slab_ref[c, pl.ds(cls, rows, stride=DIL_CLASSES), :] for c in range(slab_ref.shape[0])],
                           axis=1)


def _pre_kernel(x_ref, mod_ref, g1_ref, cos_ref, sin_ref, w_ref, qan_ref, wqq_ref, kvan_ref, wkv_ref,
                gains_ref, ones_mla_ref, ones_dil_ref, *rest, natural_in):
    if natural_in:
        q_out, k_out, v_out, qkvd_out, xp_out, slab_ref = rest
        tm = x_ref.shape[1]
        for c in range(slab_ref.shape[0]):
            slab_ref[c] = x_ref[0, :, c * LANES:(c + 1) * LANES]
    else:
        q_out, k_out, v_out, qkvd_out = rest
        tm = x_ref.shape[1] * x_ref.shape[2]
    cls_rows = tm // DIL_CLASSES
    cls_per_sub = PRE_SUB // cls_rows

    def class_rows(ref, sub):
        return jnp.concatenate([ref[0, sub * cls_per_sub + j] for j in range(cls_per_sub)], axis=0)

    def store(ref, sub, sl, val):
        for j in range(cls_per_sub):
            ref[0, sub * cls_per_sub + j, :, sl] = val[j * cls_rows:(j + 1) * cls_rows]

    hp = MLA_HEADS * HEAD_PAD
    n_dil = DIL_HEADS * DIL_HEAD_DIM
    grp = 2 * LANES
    c0 = MLA_Q_LORA
    c1 = c0 + MLA_KV_LORA
    c2 = c1 + LANES
    half = MLA_ROPE // 2
    inv_qk = 1.0 / MLA_QK
    inv_dh = 1.0 / DIL_HEAD_DIM
    lane = lax.broadcasted_iota(jnp.int32, (PRE_SUB, LANES), 1)
    low_half = lane < MLA_NOPE + half

    for sub in range(tm // PRE_SUB):
        if natural_in:
            pieces = [_class_slabs_load(slab_ref, tm, sub * cls_per_sub + j) for j in range(cls_per_sub)]
            for j, piece in enumerate(pieces):
                xp_out[0, sub * cls_per_sub + j] = piece
            x = jnp.concatenate(pieces, axis=0)
        else:
            x = class_rows(x_ref, sub)
        h = _modulated_norm(x, g1_ref[...], mod_ref[0, 0:1, :], mod_ref[0, 1:2, :])
        proj = jnp.dot(h.astype(BF16), w_ref[...], preferred_element_type=F32)

        qc = proj[:, :c0]
        qcn = (qc * lax.rsqrt(jnp.mean(qc * qc, axis=-1, keepdims=True) + EPS)) * qan_ref[...]
        qq = jnp.dot(qcn.astype(BF16), wqq_ref[...], preferred_element_type=F32)
        kvc = proj[:, c0:c1]
        kvn = (kvc * lax.rsqrt(jnp.mean(kvc * kvc, axis=-1, keepdims=True) + EPS)) * kvan_ref[...]
        kv = jnp.dot(kvn.astype(BF16), wkv_ref[...], preferred_element_type=F32)
        k_rope = proj[:, c1:c2]
        k_rope_sw = jnp.where(low_half, pltpu.roll(k_rope, LANES - half, 1), pltpu.roll(k_rope, half, 1))

        cos2 = jnp.concatenate([class_rows(cos_ref, sub)] * 2, axis=1)
        sin2 = jnp.concatenate([class_rows(sin_ref, sub)] * 2, axis=1)
        aq = gains_ref[0:1, :] * cos2
        bq = gains_ref[1:2, :] * sin2
        ak = gains_ref[2:3, :] * cos2
        bk = gains_ref[3:4, :] * sin2
        kr2 = jnp.concatenate([k_rope] * 2, axis=1)
        krs2 = jnp.concatenate([k_rope_sw] * 2, axis=1)
        for g in range(hp // grp):
            sl = slice(g * grp, (g + 1) * grp)
            qh = qq[:, sl]
            rq = lax.rsqrt(_group_sumsq(qh, ones_mla_ref) * inv_qk + EPS)
            store(q_out, sub, sl, (rq * (qh * aq + qq[:, hp + g * grp:hp + (g + 1) * grp] * bq)
                                   + gains_ref[6:7, :]).astype(BF16))
            kh = kv[:, sl] + kr2
            rk = lax.rsqrt(_group_sumsq(kh, ones_mla_ref) * inv_qk + EPS)
            store(k_out, sub, sl, (rk * (kh * ak + krs2 * bk) + gains_ref[7:8, :]).astype(BF16))
            store(v_out, sub, sl, (kv[:, hp + g * grp:hp + (g + 1) * grp] + gains_ref[8:9, :]).astype(BF16))

        for g in range(n_dil // grp):
            qd = proj[:, c2 + g * grp:c2 + (g + 1) * grp]
            qd = ((qd * lax.rsqrt(_group_sumsq(qd, ones_dil_ref) * inv_dh + EPS)) * gains_ref[4:5, :]).astype(BF16)
            kd = proj[:, c2 + n_dil + g * grp:c2 + n_dil + (g + 1) * grp]
            kd = ((kd * lax.rsqrt(_group_sumsq(kd, ones_dil_ref) * inv_dh + EPS)) * gains_ref[5:6, :]).astype(BF16)
            vd = proj[:, c2 + 2 * n_dil + g * grp:c2 + 2 * n_dil + (g + 1) * grp].astype(BF16)
            for j in range(grp // LANES):
                base = (g * (grp // LANES) + j) * DIL_GROUP
                for t, val in enumerate((qd, kd, vd)):
                    store(qkvd_out, sub, slice(base + t * LANES, base + (t + 1) * LANES), val[:, j * LANES:(j + 1) * LANES])


def _pre_call(x, mod_l, g1, cos_t, sin_t, w_pre, qan, wqq, kvan, wkv, gains, ones_mla, ones_dil, tm, natural_in):
    b, s, d = x.shape
    hp = MLA_HEADS * HEAD_PAD
    n_dil = DIL_HEADS * DIL_HEAD_DIM
    cls_len = s // DIL_CLASSES
    cls_rows = tm // DIL_CLASSES
    cls_view = lambda t: t.reshape(b, DIL_CLASSES, cls_len, t.shape[-1])
    cls_spec = lambda cols: pl.BlockSpec((1, DIL_CLASSES, cls_rows, cols), lambda i, j: (i, 0, j, 0))
    const2 = lambda i, j: (0, 0)
    row = lambda n: _resident((1, n), const2)
    out_cols = [hp, hp, hp, 3 * n_dil] + ([d] if natural_in else [])
    out_dtypes = [BF16] * 4 + ([F32] if natural_in else [])
    outs = pl.pallas_call(
        functools.partial(_pre_kernel, natural_in=natural_in),
        grid=(b, s // tm),
        in_specs=[
            pl.BlockSpec((1, tm, d), lambda i, j: (i, j, 0)) if natural_in else cls_spec(d),
            pl.BlockSpec((1, 6, d), lambda i, j: (i, 0, 0)),
            row(d),
            cls_spec(LANES),
            cls_spec(LANES),
            _resident((d, PRE_COLS), const2),
            row(MLA_Q_LORA),
            _resident((MLA_Q_LORA, 2 * hp), const2),
            row(MLA_KV_LORA),
            _resident((MLA_KV_LORA, 2 * hp), const2),
            _resident(gains.shape, const2),
            _resident(ones_mla.shape, const2),
            _resident(ones_dil.shape, const2),
        ],
        out_specs=[cls_spec(c) for c in out_cols],
        out_shape=[jax.ShapeDtypeStruct((b, DIL_CLASSES, cls_len, c), dt) for c, dt in zip(out_cols, out_dtypes)],
        scratch_shapes=[pltpu.VMEM((d // LANES, tm, LANES), F32)] if natural_in else [],
        compiler_params=_cparams(("arbitrary", "arbitrary")),
        name="pre_proj_first" if natural_in else "pre_proj",
    )(x if natural_in else cls_view(x), mod_l, g1, cls_view(cos_t), cls_view(sin_t), w_pre, qan, wqq, kvan, wkv,
      gains, ones_mla, ones_dil)
    return [o.reshape(b, s, o.shape[-1]) for o in outs]


def _pair_output(outs):
    o0, o1 = outs
    lane = lax.broadcasted_iota(jnp.int32, o0.shape, 1)
    return jnp.where(lane < MLA_V, o0 / o0[:, MLA_V:MLA_V + 1], o1 / o1[:, 0:1]).astype(BF16)


def _mla_online_kernel(q_ref, k_ref, v_ref, o_ref, *, tk):
    tq = q_ref.shape[1]
    n_k = k_ref.shape[1] // tk
    outs = []
    for hh in range(2):
        hl = slice(hh * HEAD_PAD, (hh + 1) * HEAD_PAD)
        q = q_ref[0, :, hl]

        def body(j, carry, q=q, hl=hl):
            m, acc = carry
            ks = pl.multiple_of(j * tk, tk)
            k = k_ref[0, pl.ds(ks, tk), hl]
            v = v_ref[0, pl.ds(ks, tk), hl]
            s = lax.dot_general(q, k, (((1,), (1,)), ((), ())), preferred_element_type=F32)
            m_new = jnp.maximum(m, jnp.max(s, axis=-1, keepdims=True))
            p = jnp.exp2(s - m_new)
            acc = jnp.exp2(m - m_new) * acc + jnp.dot(p.astype(BF16), v, preferred_element_type=F32)
            return m_new, acc

        init = (jnp.full((tq, 1), -jnp.inf, F32), jnp.zeros((tq, HEAD_PAD), F32))
        outs.append(lax.fori_loop(0, n_k, body, init, unroll=True)[1])
    o_ref[0] = _pair_output(outs)


def _mla_bounded_kernel(q_ref, k_ref, v_ref, o_ref, *, tk):
    n_k = k_ref.shape[1] // tk
    for qs in range(q_ref.shape[1] // MLA_SUB):
        rows = slice(qs * MLA_SUB, (qs + 1) * MLA_SUB)
        outs = []
        for hh in range(2):
            hl = slice(hh * HEAD_PAD, (hh + 1) * HEAD_PAD)
            q = q_ref[0, rows, hl]
            acc = jnp.zeros((MLA_SUB, HEAD_PAD), F32)
            for j in range(n_k):
                k = k_ref[0, j * tk:(j + 1) * tk, hl]
                v = v_ref[0, j * tk:(j + 1) * tk, hl]
                s = lax.dot_general(q, k, (((1,), (1,)), ((), ())), preferred_element_type=F32)
                acc = acc + jnp.dot(jnp.exp2(s).astype(BF16), v, preferred_element_type=F32)
            outs.append(acc)
        o_ref[0, rows, :] = _pair_output(outs)


def _mla_call(q, k, v, tiles, bounded):
    tq, tk = tiles
    b, s, width = q.shape
    pair = 2 * HEAD_PAD
    n_pair = width // pair
    return pl.pallas_call(
        functools.partial(_mla_bounded_kernel if bounded else _mla_online_kernel, tk=tk),
        grid=(b, n_pair, s // tq),
        in_specs=[
            pl.BlockSpec((1, tq, pair), lambda i, h, j: (i, j, h)),
            pl.BlockSpec((1, s, pair), lambda i, h, j: (i, 0, h)),
            pl.BlockSpec((1, s, pair), lambda i, h, j: (i, 0, h)),
        ],
        out_specs=pl.BlockSpec((1, tq, 2 * MLA_V), lambda i, h, j: (i, j, h)),
        out_shape=jax.ShapeDtypeStruct((b, s, n_pair * 2 * MLA_V), BF16),
        compiler_params=_cparams(("arbitrary", "arbitrary", "arbitrary")),
        name="mla_attn_bounded" if bounded else "mla_attn_online",
    )(q, k, v)


def _dil_kernel(x_ref, bias_ref, o_ref, qf, kf, vf, *states, bounded):
    s_len = o_ref.shape[1]
    cls_len = s_len // DIL_CLASSES
    piece_q = DIL_BQ // DIL_CLASSES
    piece_k = DIL_BK // DIL_CLASSES
    lanes = (slice(0, LANES), slice(LANES, 2 * LANES), slice(2 * LANES, 3 * LANES))
    lane = lax.broadcasted_iota(jnp.int32, (DIL_BQ, LANES), 1)
    first = lane < DIL_HEAD_DIM
    for dst, sl in zip((qf, kf, vf), lanes):
        dst[...] = x_ref[0, :, sl].astype(F32)

    def band_tile(q_t, k_t, v_t, tab):
        zero = jnp.zeros_like(q_t)
        qm = jnp.concatenate([jnp.where(first, q_t, zero), jnp.where(first, zero, q_t)], axis=0)
        s = lax.dot_general(qm, k_t, (((1,), (1,)), ((), ())), preferred_element_type=F32)
        s = s + bias_ref[tab].reshape(2 * DIL_BQ, DIL_BK)
        if not bounded:
            m = jnp.max(s, axis=-1, keepdims=True)
            s = s - m
        p = jnp.exp2(s)
        l = jnp.sum(p, axis=-1, keepdims=True)
        pv = jnp.dot(p.astype(BF16), v_t, preferred_element_type=F32)
        out = (jnp.where(first, l[:DIL_BQ], l[DIL_BQ:]), jnp.where(first, pv[:DIL_BQ], pv[DIL_BQ:]))
        return out if bounded else (jnp.where(first, m[:DIL_BQ], m[DIL_BQ:]),) + out

    def merge(old, new):
        if bounded:
            return tuple(o + n for o, n in zip(old, new))
        m_n = jnp.maximum(old[0], new[0])
        a = jnp.exp2(old[0] - m_n)
        c = jnp.exp2(new[0] - m_n)
        return (m_n, a * old[1] + c * new[1], a * old[2] + c * new[2])

    def variant_of(n, n_blk):
        return 0 if n == 0 else (2 if n == n_blk - 1 else 1)

    p_idx = 1
    n_blk = cls_len // DIL_BQ
    for r in range(DIL_CLASSES):
        for n in range(n_blk):
            i0 = r * cls_len + n * DIL_BQ
            ks = r * cls_len + min(max(n * DIL_BQ - DIL_HALF, 0), cls_len - DIL_BK)
            vals = band_tile(x_ref[0, i0:i0 + DIL_BQ, lanes[0]], x_ref[0, ks:ks + DIL_BK, lanes[1]],
                             x_ref[0, ks:ks + DIL_BK, lanes[2]], 3 * p_idx + variant_of(n, n_blk))
            for st, val in zip(states, vals):
                st[i0:i0 + DIL_BQ, :] = val

    p_idx = 0
    n_blk = s_len // DIL_BQ
    for n in range(n_blk):
        i0 = n * DIL_BQ // DIL_CLASSES
        ks = min(max(n * DIL_BQ - DIL_HALF, 0), s_len - DIL_BK) // DIL_CLASSES
        rows_q = [slice(r * cls_len + i0, r * cls_len + i0 + piece_q) for r in range(DIL_CLASSES)]
        rows_k = [slice(r * cls_len + ks, r * cls_len + ks + piece_k) for r in range(DIL_CLASSES)]
        vals = band_tile(jnp.concatenate([x_ref[0, rq, lanes[0]] for rq in rows_q], axis=0),
                         jnp.concatenate([x_ref[0, rk, lanes[1]] for rk in rows_k], axis=0),
                         jnp.concatenate([x_ref[0, rk, lanes[2]] for rk in rows_k], axis=0),
                         3 * p_idx + variant_of(n, n_blk))
        for r, rq in enumerate(rows_q):
            piece = slice(r * piece_q, (r + 1) * piece_q)
            merged = merge(tuple(st[rq, :] for st in states), tuple(val[piece] for val in vals))
            for st, val in zip(states, merged):
                st[rq, :] = val

    p_idx = 2
    ratio = DIL_PATTERNS[p_idx][1] // DIL_CLASSES
    sub_len = cls_len // ratio
    n_blk = sub_len // DIL_BQ
    for r in range(DIL_CLASSES):
        for c in range(ratio):
            for n in range(n_blk):
                ks = min(max(n * DIL_BQ - DIL_HALF, 0), sub_len - DIL_BK)
                rows_q = pl.ds(r * cls_len + c + ratio * n * DIL_BQ, DIL_BQ, stride=ratio)
                rows_k = pl.ds(r * cls_len + c + ratio * ks, DIL_BK, stride=ratio)
                vals = band_tile(qf[rows_q, :].astype(BF16), kf[rows_k, :].astype(BF16),
                                 vf[rows_k, :].astype(BF16), 3 * p_idx + variant_of(n, n_blk))
                merged = merge(tuple(st[rows_q, :] for st in states), vals)
                for st, val in zip(states, merged):
                    st[rows_q, :] = val

    o_ref[0] = (states[-1][...] / states[-2][...]).astype(BF16)


def _dil_call(qkvd, bias_tabs, bounded):
    b, s, width = qkvd.shape
    n_pair = width // DIL_GROUP
    n_tab = bias_tabs.shape[0]
    n_state = 2 if bounded else 3
    return pl.pallas_call(
        functools.partial(_dil_kernel, bounded=bounded),
        grid=(n_pair, b),
        in_specs=[pl.BlockSpec((1, s, DIL_GROUP), lambda hp, i: (i, 0, hp)),
                  pl.BlockSpec((n_tab, 2, DIL_BQ, DIL_BK), lambda hp, i: (0, hp, 0, 0))],
        out_specs=pl.BlockSpec((1, s, LANES), lambda hp, i: (i, 0, hp)),
        out_shape=jax.ShapeDtypeStruct((b, s, n_pair * LANES), BF16),
        scratch_shapes=[pltpu.VMEM((s, LANES), F32)] * (3 + n_state),
        compiler_params=_cparams(("arbitrary", "arbitrary")),
        name="dil_attn_bounded" if bounded else "dil_attn_online",
    )(qkvd, bias_tabs)


def _post_kernel(x_ref, ya_ref, yb_ref, mod_ref, g1_ref, wg_ref, wa_ref, wb_ref, wo_ref, o_ref):
    x = x_ref[0]
    h = _modulated_norm(x, g1_ref[...], mod_ref[0, 0:1, :], mod_ref[0, 1:2, :])
    gates = jnp.dot(h.astype(BF16), wg_ref[...], preferred_element_type=F32)
    d = x.shape[-1]
    a = jnp.dot(ya_ref[0], wa_ref[...], preferred_element_type=F32)
    bb = jnp.dot(yb_ref[0], wb_ref[...], preferred_element_type=F32)
    merged = jax.nn.sigmoid(gates[:, :d]) * a + jax.nn.sigmoid(gates[:, d:]) * bb
    upd = jnp.dot(merged.astype(BF16), wo_ref[...], preferred_element_type=F32)
    o_ref[0] = x + mod_ref[0, 2:3, :] * upd


def _post_call(x, ya, yb, mod_l, g1, wg, wa, wb, wo, tm):
    b, s, d = x.shape
    tok = lambda i, j: (i, j, 0)
    const2 = lambda i, j: (0, 0)
    return pl.pallas_call(
        _post_kernel,
        grid=(b, s // tm),
        in_specs=[
            pl.BlockSpec((1, tm, d), tok),
            pl.BlockSpec((1, tm, ya.shape[-1]), tok),
            pl.BlockSpec((1, tm, yb.shape[-1]), tok),
            pl.BlockSpec((1, 6, d), lambda i, j: (i, 0, 0)),
            _resident((1, d), const2),
            _resident(wg.shape, const2),
            _resident(wa.shape, const2),
            _resident(wb.shape, const2),
            _resident(wo.shape, const2),
        ],
        out_specs=pl.BlockSpec((1, tm, d), tok),
        out_shape=jax.ShapeDtypeStruct(x.shape, F32),
        compiler_params=_cparams(("arbitrary", "arbitrary")),
        name="attn_out",
    )(x, ya, yb, mod_l, g1, wg, wa, wb, wo)


def _ffn_kernel(x_ref, mod_ref, g2_ref, wg_ref, wu_ref, wd_ref, o_ref, *scratch, natural_out):
    x = jnp.concatenate([x_ref[0, c] for c in range(DIL_CLASSES)], axis=0) if natural_out else x_ref[0]
    h = _modulated_norm(x, g2_ref[...], mod_ref[0, 3:4, :], mod_ref[0, 4:5, :]).astype(BF16)
    g = jnp.dot(h, wg_ref[...], preferred_element_type=F32)
    u = jnp.dot(h, wu_ref[...], preferred_element_type=F32)
    act = (g * jax.nn.sigmoid(g)) * u
    upd = jnp.dot(act.astype(BF16), wd_ref[...], preferred_element_type=F32)
    res = x + mod_ref[0, 5:6, :] * upd
    if not natural_out:
        o_ref[0] = res
        return
    slab_ref, = scratch
    cls_rows = res.shape[0] // DIL_CLASSES
    for cls in range(DIL_CLASSES):
        for c in range(slab_ref.shape[0]):
            slab_ref[c, pl.ds(cls, cls_rows, stride=DIL_CLASSES), :] = res[cls * cls_rows:(cls + 1) * cls_rows,
                                                                           c * LANES:(c + 1) * LANES]
    o_ref[0] = jnp.concatenate([slab_ref[c] for c in range(slab_ref.shape[0])], axis=1)


def _ffn_call(x, mod_l, g2, wg, wu, wd, tm, natural_out):
    b, s, d = x.shape
    tok = lambda i, j: (i, j, 0)
    const2 = lambda i, j: (0, 0)
    cls_rows = tm // DIL_CLASSES
    x_in = x.reshape(b, DIL_CLASSES, s // DIL_CLASSES, d) if natural_out else x
    return pl.pallas_call(
        functools.partial(_ffn_kernel, natural_out=natural_out),
        grid=(b, s // tm),
        in_specs=[
            pl.BlockSpec((1, DIL_CLASSES, cls_rows, d), lambda i, j: (i, 0, j, 0)) if natural_out
            else pl.BlockSpec((1, tm, d), tok),
            pl.BlockSpec((1, 6, d), lambda i, j: (i, 0, 0)),
            _resident((1, d), const2),
            _resident(wg.shape, const2),
            _resident(wu.shape, const2),
            _resident(wd.shape, const2),
        ],
        out_specs=pl.BlockSpec((1, tm, d), tok),
        out_shape=jax.ShapeDtypeStruct(x.shape, F32),
        scratch_shapes=[pltpu.VMEM((d // LANES, tm, LANES), F32)] if natural_out else [],
        compiler_params=_cparams(("arbitrary", "arbitrary")),
        name="swiglu_last" if natural_out else "swiglu",
    )(x_in, mod_l, g2, wg, wu, wd)


def _pad_heads(w, width):
    rows = w.shape[0]
    w = w.reshape(rows, MLA_HEADS, width)
    return jnp.pad(w, ((0, 0), (0, 0), (0, HEAD_PAD - width))).reshape(rows, MLA_HEADS * HEAD_PAD)


def _pad_heads_alternating(w):
    rows = w.shape[0]
    w = w.reshape(rows, MLA_HEADS // 2, 2, MLA_V)
    zero = jnp.zeros_like(w[:, :, 0])
    even = jnp.concatenate([w[:, :, 0], zero], axis=-1)
    odd = jnp.concatenate([zero, w[:, :, 1]], axis=-1)
    return jnp.stack([even, odd], axis=2).reshape(rows, MLA_HEADS * HEAD_PAD)


def _pad_lanes(g):
    return jnp.pad(g, (0, LANES - g.shape[0]))


def _swap_rope(g):
    half = MLA_ROPE // 2
    return jnp.concatenate([jnp.zeros((MLA_NOPE,), g.dtype), g[MLA_NOPE + half:], g[MLA_NOPE:MLA_NOPE + half]])


def _class_major(t, n_class):
    b, s = t.shape[:2]
    return jnp.swapaxes(t.reshape(b, s // n_class, n_class, *t.shape[2:]), 1, 2).reshape(t.shape)


def _block_ones(block, size=2 * LANES):
    idx = np.arange(size) // block
    return jnp.asarray(idx[:, None] == idx[None, :], BF16)


def kernel(x, c, positions, rel_bias, norm1_g, norm2_g, ada_w, ada_b, w_in, q_a_norm, w_q_b, kv_a_norm, w_kv_b,
           q_norm_a, k_norm_a, q_norm_b, k_norm_b, w_branch_a, w_branch_b, w_out, w_ffn_gate, w_ffn_up,
           w_ffn_down):
    depth = w_in.shape[0]
    b, s, d = x.shape
    positions = _class_major(positions, DIL_CLASSES)

    mod = _ada_call(c, ada_w, ada_b).reshape(depth, b, 6, d)
    cos_t, sin_t = _rope_call(positions)
    dil_qk_bound = DIL_HEAD_DIM * jnp.max(jnp.abs(q_norm_b * (DIL_HEAD_DIM ** -0.5 * LOG2E))) * jnp.max(jnp.abs(k_norm_b))
    dil_shift = dil_qk_bound + jnp.max(rel_bias) * LOG2E
    dil_range = 2.0 * dil_qk_bound + (jnp.max(rel_bias) - jnp.min(rel_bias)) * LOG2E
    bias_tabs = _bias_call(rel_bias, jnp.zeros((), F32))
    bias_tabs_shifted = _bias_call(rel_bias, dil_shift)
    ones_mla = _block_ones(HEAD_PAD)
    ones_dil = _block_ones(DIL_HEAD_DIM)
    half = MLA_ROPE // 2
    lane2 = np.arange(2 * LANES) % LANES
    spare_qk = jnp.asarray(lane2 == MLA_QK, F32)
    spare_v = jnp.asarray((np.arange(2 * LANES) == MLA_V) | (np.arange(2 * LANES) == LANES), F32)

    for l in range(depth):
        wl = w_in[l]
        k_rope_cols = jnp.pad(wl[:, _C_KR:_C_QKVB], ((0, 0), (MLA_NOPE, LANES - MLA_QK)))
        w_pre = jnp.concatenate([wl[:, _C_QC:_C_KR], k_rope_cols, wl[:, _C_QKVB:_C_GATE]], axis=1).astype(BF16)
        w_gates = wl[:, _C_GATE:].astype(BF16)
        wq3 = w_q_b[l].reshape(MLA_Q_LORA, MLA_HEADS, MLA_QK)
        lo, hi = wq3[..., MLA_NOPE:MLA_NOPE + half], wq3[..., MLA_NOPE + half:]
        wq_sw = jnp.concatenate([jnp.zeros_like(wq3[..., :MLA_NOPE]), hi, lo], axis=-1)
        wqq = jnp.concatenate([_pad_heads(w_q_b[l], MLA_QK),
                               _pad_heads(wq_sw.reshape(MLA_Q_LORA, -1), MLA_QK)], axis=1).astype(BF16)
        wkv3 = w_kv_b[l].reshape(MLA_KV_LORA, MLA_HEADS, MLA_NOPE + MLA_V)
        wkv = jnp.concatenate([_pad_heads(wkv3[:, :, :MLA_NOPE].reshape(MLA_KV_LORA, -1), MLA_NOPE),
                               _pad_heads_alternating(wkv3[:, :, MLA_NOPE:])], axis=1).astype(BF16)
        gq = q_norm_a[l] * (MLA_QK ** -0.5 * LOG2E)
        gk = k_norm_a[l]
        score_bound = MLA_QK * jnp.max(jnp.abs(gq)) * jnp.max(jnp.abs(gk))
        gains = jnp.stack([
            jnp.tile(_pad_lanes(gq), 2), jnp.tile(_pad_lanes(_swap_rope(gq)), 2),
            jnp.tile(_pad_lanes(gk), 2), jnp.tile(_pad_lanes(_swap_rope(gk)), 2),
            jnp.tile(q_norm_b[l] * (DIL_HEAD_DIM ** -0.5 * LOG2E), 4), jnp.tile(k_norm_b[l], 4),
            -score_bound * spare_qk, spare_qk, spare_v] + [jnp.zeros((2 * LANES,), F32)] * 7)
        g1 = norm1_g[l].reshape(1, d)
        g2 = norm2_g[l].reshape(1, d)

        q, k, v, qkvd, *x_cls = _pre_call(x, mod[l], g1, cos_t, sin_t, w_pre, q_a_norm[l].reshape(1, -1), wqq,
                                          kv_a_norm[l].reshape(1, -1), wkv, gains, ones_mla, ones_dil, tm=PRE_TM,
                                          natural_in=(l == 0))
        if x_cls:
            x, = x_cls
        ya = lax.cond(score_bound <= MLA_SAFE_BOUND,
                      functools.partial(_mla_call, tiles=MLA_BOUNDED_TILES, bounded=True),
                      functools.partial(_mla_call, tiles=MLA_ONLINE_TILES, bounded=False), q, k, v)
        yb = lax.cond(dil_range <= DIL_SAFE_RANGE,
                      lambda a, shifted, plain: _dil_call(a, shifted, bounded=True),
                      lambda a, shifted, plain: _dil_call(a, plain, bounded=False),
                      qkvd, bias_tabs_shifted, bias_tabs)
        x = _post_call(x, ya, yb, mod[l], g1, w_gates, w_branch_a[l].astype(BF16), w_branch_b[l].astype(BF16),
                       w_out[l].astype(BF16), tm=POST_TM)
        x = _ffn_call(x, mod[l], g2, w_ffn_gate[l].astype(BF16), w_ffn_up[l].astype(BF16),
                      w_ffn_down[l].astype(BF16), tm=FFN_TM, natural_out=(l == depth - 1))
    return x
```

```python
import functools
import math

import jax
import jax.numpy as jnp
import numpy as np
from jax import lax
from jax.experimental import pallas as pl
from jax.experimental.pallas import tpu as pltpu

F32 = jnp.float32
BF16 = jnp.bfloat16

LANES = 128
V7X_VMEM_BYTES = 64 * 1024 * 1024
VMEM_LIMIT_BYTES = V7X_VMEM_BYTES - 8 * 1024 * 1024

MLA_HEADS = 8
MLA_Q_LORA = 256
MLA_KV_LORA = 128
MLA_NOPE = 64
MLA_ROPE = 32
MLA_V = 64
MLA_QK = MLA_NOPE + MLA_ROPE
ROPE_THETA = 10000.0
DIL_HEADS = 8
DIL_HEAD_DIM = 64
DIL_PATTERNS = ((128, 1), (512, 4), (2048, 16))
DIL_HALF = 64
REL_BUCKETS = 32
REL_MAX_DIST = 1024
EPS = 1e-6
NEG_INF = -1e30
LOG2E = math.log2(math.e)
MLA_SAFE_BOUND = 60.0
MLA_SUB = 512

PRE_TM = 512
POST_TM = 1024
FFN_TM = 1024
MLA_BOUNDED_TILES = (4096, 256)
MLA_ONLINE_TILES = (512, 1024)

_C_QC = 0
_C_KVC = _C_QC + MLA_Q_LORA
_C_KR = _C_KVC + MLA_KV_LORA
_C_QKVB = _C_KR + MLA_ROPE
_C_GATE = _C_QKVB + 3 * DIL_HEADS * DIL_HEAD_DIM
PRE_COLS = MLA_Q_LORA + MLA_KV_LORA + LANES + 3 * DIL_HEADS * DIL_HEAD_DIM
HEAD_PAD = LANES

DIL_BQ = 2 * DIL_HALF
DIL_BK = 4 * DIL_HALF
DIL_CLASSES = 4
DIL_GROUP = 3 * LANES
DIL_SAFE_RANGE = 120.0


def _cparams(sem):
    return pltpu.CompilerParams(dimension_semantics=sem, vmem_limit_bytes=VMEM_LIMIT_BYTES)


def _resident(shape, index_map):
    return pl.BlockSpec(shape, index_map, pipeline_mode=pl.Buffered(1))


def _ada_kernel(c_ref, w_ref, b_ref, o_ref):
    c = c_ref[...]
    act = (c * jax.nn.sigmoid(c)).astype(BF16)
    o_ref[0] = jnp.dot(act, w_ref[0].astype(BF16), preferred_element_type=F32) + b_ref[0]


def _ada_call(c, ada_w, ada_b):
    depth, d, six_d = ada_w.shape
    b = c.shape[0]
    cols = 2 * d
    return pl.pallas_call(
        _ada_kernel,
        grid=(depth, six_d // cols),
        in_specs=[
            pl.BlockSpec((b, d), lambda l, j: (0, 0)),
            pl.BlockSpec((1, d, cols), lambda l, j: (l, 0, j)),
            pl.BlockSpec((1, 1, cols), lambda l, j: (l, 0, j)),
        ],
        out_specs=pl.BlockSpec((1, b, cols), lambda l, j: (l, 0, j)),
        out_shape=jax.ShapeDtypeStruct((depth, b, six_d), F32),
        compiler_params=_cparams(("arbitrary", "arbitrary")),
        name="ada_mod",
    )(c, ada_w, ada_b.reshape(depth, 1, six_d))


ROPE_PACK = LANES // MLA_ROPE


def _rope_kernel(pos_ref, inv_ref, c_ref, s_ref):
    ts = pos_ref.shape[1]
    rows = ts // ROPE_PACK
    half = MLA_ROPE // 2
    lane = lax.broadcasted_iota(jnp.int32, (rows, LANES), 1)
    pos = pos_ref[0].astype(F32)
    ang = jnp.zeros((rows, LANES), F32)
    for g in range(ROPE_PACK):
        ang = jnp.where((lane >= g * MLA_ROPE) & (lane < (g + 1) * MLA_ROPE),
                        pos[g * rows:(g + 1) * rows] * inv_ref[...], ang)
    cosv = jnp.cos(ang)
    sinv = jnp.sin(ang)
    for g in range(ROPE_PACK):
        shift = (MLA_NOPE - g * MLA_ROPE) % LANES
        cg = pltpu.roll(cosv, shift, 1) if shift else cosv
        sg = pltpu.roll(sinv, shift, 1) if shift else sinv
        c_ref[0, g * rows:(g + 1) * rows, :] = jnp.where(lane < MLA_NOPE, 1.0, jnp.where(lane < MLA_QK, cg, 0.0))
        s_ref[0, g * rows:(g + 1) * rows, :] = jnp.where(
            lane < MLA_NOPE, 0.0, jnp.where(lane < MLA_NOPE + half, -sg, jnp.where(lane < MLA_QK, sg, 0.0)))


def _rope_call(positions):
    b, s = positions.shape
    half = MLA_ROPE // 2
    inv = ROPE_THETA ** (-jnp.arange(half, dtype=F32) / half)
    inv_lane = jnp.tile(inv, 2 * ROPE_PACK).reshape(1, LANES)
    ts = 512
    spec = pl.BlockSpec((1, ts, LANES), lambda i, j: (i, j, 0))
    return pl.pallas_call(
        _rope_kernel,
        grid=(b, s // ts),
        in_specs=[pl.BlockSpec((1, ts, 1), lambda i, j: (i, j, 0)),
                  pl.BlockSpec((1, LANES), lambda i, j: (0, 0))],
        out_specs=[spec, spec],
        out_shape=[jax.ShapeDtypeStruct((b, s, LANES), F32)] * 2,
        compiler_params=_cparams(("arbitrary", "arbitrary")),
        name="rope_tables",
    )(positions.reshape(b, s, 1), inv_lane)


def _t5_bucket(rel):
    nb = REL_BUCKETS // 2
    max_exact = nb // 2
    ret = jnp.where(rel > 0, nb, 0)
    n = jnp.abs(rel)
    nf = jnp.maximum(n, 1).astype(F32)
    large = max_exact + (jnp.log(nf / max_exact) / math.log(REL_MAX_DIST / max_exact) * (nb - max_exact)).astype(jnp.int32)
    large = jnp.minimum(large, nb - 1)
    return ret + jnp.where(n < max_exact, n, large)


def _bias_kernel(bucket_ref, rb_ref, shift_ref, o_ref):
    h = pl.program_id(2)
    bucket = bucket_ref[0]
    acc = jnp.full(bucket.shape, NEG_INF, F32)
    for bkt in range(REL_BUCKETS):
        acc = jnp.where(bucket == bkt, rb_ref[bkt, h], acc)
    o_ref[0, 0] = acc * LOG2E - shift_ref[pl.program_id(0)]


def _bias_call(rel_bias, shifts):
    tabs = []
    for _, dil in DIL_PATTERNS:
        qi, kj = np.arange(DIL_BQ), np.arange(DIL_BK)
        if dil == 1:
            qi = DIL_CLASSES * (qi % (DIL_BQ // DIL_CLASSES)) + qi // (DIL_BQ // DIL_CLASSES)
            kj = DIL_CLASSES * (kj % (DIL_BK // DIL_CLASSES)) + kj // (DIL_BK // DIL_CLASSES)
        qi = jnp.asarray(qi, jnp.int32)[:, None]
        kj = jnp.asarray(kj, jnp.int32)[None, :]
        for variant in range(3):
            rel = kj - variant * DIL_HALF - qi
            tabs.append(jnp.where(jnp.abs(rel) <= DIL_HALF, _t5_bucket(rel * dil), -1))
    bucket = jnp.stack(tabs).astype(jnp.int32)
    n_tab = bucket.shape[0]
    n_set = shifts.shape[0]
    return pl.pallas_call(
        _bias_kernel,
        grid=(n_set, n_tab, DIL_HEADS),
        in_specs=[pl.BlockSpec((1, DIL_BQ, DIL_BK), lambda s, t, h: (t, 0, 0)),
                  pl.BlockSpec(memory_space=pltpu.SMEM), pl.BlockSpec(memory_space=pltpu.SMEM)],
        out_specs=pl.BlockSpec((1, 1, DIL_BQ, DIL_BK), lambda s, t, h: (s * n_tab + t, h, 0, 0)),
        out_shape=jax.ShapeDtypeStruct((n_set * n_tab, DIL_HEADS, DIL_BQ, DIL_BK), F32),
        compiler_params=_cparams(("arbitrary", "arbitrary", "arbitrary")),
        name="bias_tables",
    )(bucket, rel_bias, shifts.astype(F32))


def _modulated_norm(x, gain, shift, scale):
    r = lax.rsqrt(jnp.mean(x * x, axis=-1, keepdims=True) + EPS)
    return (x * r) * (gain * (1.0 + scale)) + shift


PRE_SUB = 256


def _group_sumsq(t, ones_ref):
    return jnp.dot((t * t).astype(BF16), ones_ref[...], preferred_element_type=F32)


def _class_slabs_load(slab_ref, tile, cls):
    rows = tile // DIL_CLASSES
    return jnp.concatenate([slab_ref[c, pl.ds(cls, rows, stride=DIL_CLASSES), :] for c in range(slab_ref.shape[0])],
                           axis=1)


def _pre_kernel(x_ref, mod_ref, g1_ref, cos_ref, sin_ref, w_ref, qan_ref, wqq_ref, kvan_ref, wkv_ref,
                gains_ref, ones_mla_ref, ones_dil_ref, *rest, natural_in):
    if natural_in:
        q_out, k_out, v_out, qkvd_out, xp_out, slab_ref = rest
        tm = x_ref.shape[1]
        for c in range(slab_ref.shape[0]):
            slab_ref[c] = x_ref[0, :, c * LANES:(c + 1) * LANES]
    else:
        q_out, k_out, v_out, qkvd_out = rest
        tm = x_ref.shape[1] * x_ref.shape[2]
    cls_rows = tm // DIL_CLASSES
    cls_per_sub = PRE_SUB // cls_rows

    def class_rows(ref, sub):
        return jnp.concatenate([ref[0, sub * cls_per_sub + j] for j in range(cls_per_sub)], axis=0)

    def store(ref, sub, sl, val):
        for j in range(cls_per_sub):
            ref[0, sub * cls_per_sub + j, :, sl] = val[j * cls_rows:(j + 1) * cls_rows]

    hp = MLA_HEADS * HEAD_PAD
    n_dil = DIL_HEADS * DIL_HEAD_DIM
    grp = 2 * LANES
    c0 = MLA_Q_LORA
    c1 = c0 + MLA_KV_LORA
    c2 = c1 + LANES
    half = MLA_ROPE // 2
    inv_qk = 1.0 / MLA_QK
    inv_dh = 1.0 / DIL_HEAD_DIM
    lane = lax.broadcasted_iota(jnp.int32, (PRE_SUB, LANES), 1)
    low_half = lane < MLA_NOPE + half

    for sub in range(tm // PRE_SUB):
        if natural_in:
            pieces = [_class_slabs_load(slab_ref, tm, sub * cls_per_sub + j) for j in range(cls_per_sub)]
            for j, piece in enumerate(pieces):
                xp_out[0, sub * cls_per_sub + j] = piece
            x = jnp.concatenate(pieces, axis=0)
        else:
            x = class_rows(x_ref, sub)
        h = _modulated_norm(x, g1_ref[...], mod_ref[0, 0:1, :], mod_ref[0, 1:2, :])
        proj = jnp.dot(h.astype(BF16), w_ref[...], preferred_element_type=F32)

        qc = proj[:, :c0]
        qcn = (qc * lax.rsqrt(jnp.mean(qc * qc, axis=-1, keepdims=True) + EPS)) * qan_ref[...]
        qq = jnp.dot(qcn.astype(BF16), wqq_ref[...], preferred_element_type=F32)
        kvc = proj[:, c0:c1]
        kvn = (kvc * lax.rsqrt(jnp.mean(kvc * kvc, axis=-1, keepdims=True) + EPS)) * kvan_ref[...]
        kv = jnp.dot(kvn.astype(BF16), wkv_ref[...], preferred_element_type=F32)
        k_rope = proj[:, c1:c2]
        k_rope_sw = jnp.where(low_half, pltpu.roll(k_rope, LANES - half, 1), pltpu.roll(k_rope, half, 1))

        cos2 = jnp.concatenate([class_rows(cos_ref, sub)] * 2, axis=1)
        sin2 = jnp.concatenate([class_rows(sin_ref, sub)] * 2, axis=1)
        aq = gains_ref[0:1, :] * cos2
        bq = gains_ref[1:2, :] * sin2
        ak = gains_ref[2:3, :] * cos2
        bk = gains_ref[3:4, :] * sin2
        kr2 = jnp.concatenate([k_rope] * 2, axis=1)
        krs2 = jnp.concatenate([k_rope_sw] * 2, axis=1)
        for g in range(hp // grp):
            sl = slice(g * grp, (g + 1) * grp)
            qh = qq[:, sl]
            rq = lax.rsqrt(_group_sumsq(qh, ones_mla_ref) * inv_qk + EPS)
            store(q_out, sub, sl, (rq * (qh * aq + qq[:, hp + g * grp:hp + (g + 1) * grp] * bq)
                                   + gains_ref[6:7, :]).astype(BF16))
            kh = kv[:, sl] + kr2
            rk = lax.rsqrt(_group_sumsq(kh, ones_mla_ref) * inv_qk + EPS)
            store(k_out, sub, sl, (rk * (kh * ak + krs2 * bk) + gains_ref[7:8, :]).astype(BF16))
            store(v_out, sub, sl, (kv[:, hp + g * grp:hp + (g + 1) * grp] + gains_ref[8:9, :]).astype(BF16))

        for g in range(n_dil // grp):
            qd = proj[:, c2 + g * grp:c2 + (g + 1) * grp]
            qd = ((qd * lax.rsqrt(_group_sumsq(qd, ones_dil_ref) * inv_dh + EPS)) * gains_ref[4:5, :]).astype(BF16)
            kd = proj[:, c2 + n_dil + g * grp:c2 + n_dil + (g + 1) * grp]
            kd = ((kd * lax.rsqrt(_group_sumsq(kd, ones_dil_ref) * inv_dh + EPS)) * gains_ref[5:6, :]).astype(BF16)
            vd = proj[:, c2 + 2 * n_dil + g * grp:c2 + 2 * n_dil + (g + 1) * grp].astype(BF16)
            for j in range(grp // LANES):
                base = (g * (grp // LANES) + j) * DIL_GROUP
                for t, val in enumerate((qd, kd, vd)):
                    store(qkvd_out, sub, slice(base + t * LANES, base + (t + 1) * LANES), val[:, j * LANES:(j + 1) * LANES])


def _pre_call(x, mod_l, g1, cos_t, sin_t, w_pre, qan, wqq, kvan, wkv, gains, ones_mla, ones_dil, tm, natural_in):
    b, s, d = x.shape
    hp = MLA_HEADS * HEAD_PAD
    n_dil = DIL_HEADS * DIL_HEAD_DIM
    cls_len = s // DIL_CLASSES
    cls_rows = tm // DIL_CLASSES
    cls_view = lambda t: t.reshape(b, DIL_CLASSES, cls_len, t.shape[-1])
    cls_spec = lambda cols: pl.BlockSpec((1, DIL_CLASSES, cls_rows, cols), lambda i, j: (i, 0, j, 0))
    const2 = lambda i, j: (0, 0)
    row = lambda n: _resident((1, n), const2)
    out_cols = [hp, hp, hp, 3 * n_dil] + ([d] if natural_in else [])
    out_dtypes = [BF16] * 4 + ([F32] if natural_in else [])
    outs = pl.pallas_call(
        functools.partial(_pre_kernel, natural_in=natural_in),
        grid=(b, s // tm),
        in_specs=[
            pl.BlockSpec((1, tm, d), lambda i, j: (i, j, 0)) if natural_in else cls_spec(d),
            pl.BlockSpec((1, 6, d), lambda i, j: (i, 0, 0)),
            row(d),
            cls_spec(LANES),
            cls_spec(LANES),
            _resident((d, PRE_COLS), const2),
            row(MLA_Q_LORA),
            _resident((MLA_Q_LORA, 2 * hp), const2),
            row(MLA_KV_LORA),
            _resident((MLA_KV_LORA, 2 * hp), const2),
            _resident(gains.shape, const2),
            _resident(ones_mla.shape, const2),
            _resident(ones_dil.shape, const2),
        ],
        out_specs=[cls_spec(c) for c in out_cols],
        out_shape=[jax.ShapeDtypeStruct((b, DIL_CLASSES, cls_len, c), dt) for c, dt in zip(out_cols, out_dtypes)],
        scratch_shapes=[pltpu.VMEM((d // LANES, tm, LANES), F32)] if natural_in else [],
        compiler_params=_cparams(("arbitrary", "arbitrary")),
        name="pre_proj_first" if natural_in else "pre_proj",
    )(x if natural_in else cls_view(x), mod_l, g1, cls_view(cos_t), cls_view(sin_t), w_pre, qan, wqq, kvan, wkv,
      gains, ones_mla, ones_dil)
    return [o.reshape(b, s, o.shape[-1]) for o in outs]


def _pair_output(outs):
    o0, o1 = outs
    lane = lax.broadcasted_iota(jnp.int32, o0.shape, 1)
    return jnp.where(lane < MLA_V, o0 / o0[:, MLA_V:MLA_V + 1], o1 / o1[:, 0:1]).astype(BF16)


def _mla_online_kernel(q_ref, k_ref, v_ref, o_ref, *, tk):
    tq = q_ref.shape[1]
    n_k = k_ref.shape[1] // tk
    outs = []
    for hh in range(2):
        hl = slice(hh * HEAD_PAD, (hh + 1) * HEAD_PAD)
        q = q_ref[0, :, hl]

        def body(j, carry, q=q, hl=hl):
            m, acc = carry
            ks = pl.multiple_of(j * tk, tk)
            k = k_ref[0, pl.ds(ks, tk), hl]
            v = v_ref[0, pl.ds(ks, tk), hl]
            s = lax.dot_general(q, k, (((1,), (1,)), ((), ())), preferred_element_type=F32)
            m_new = jnp.maximum(m, jnp.max(s, axis=-1, keepdims=True))
            p = jnp.exp2(s - m_new)
            acc = jnp.exp2(m - m_new) * acc + jnp.dot(p.astype(BF16), v, preferred_element_type=F32)
            return m_new, acc

        init = (jnp.full((tq, 1), -jnp.inf, F32), jnp.zeros((tq, HEAD_PAD), F32))
        outs.append(lax.fori_loop(0, n_k, body, init, unroll=True)[1])
    o_ref[0] = _pair_output(outs)


def _mla_bounded_kernel(q_ref, k_ref, v_ref, o_ref, *, tk):
    n_k = k_ref.shape[1] // tk
    for qs in range(q_ref.shape[1] // MLA_SUB):
        rows = slice(qs * MLA_SUB, (qs + 1) * MLA_SUB)
        outs = []
        for hh in range(2):
            hl = slice(hh * HEAD_PAD, (hh + 1) * HEAD_PAD)
            q = q_ref[0, rows, hl]
            acc = jnp.zeros((MLA_SUB, HEAD_PAD), F32)
            for j in range(n_k):
                k = k_ref[0, j * tk:(j + 1) * tk, hl]
                v = v_ref[0, j * tk:(j + 1) * tk, hl]
                s = lax.dot_general(q, k, (((1,), (1,)), ((), ())), preferred_element_type=F32)
                acc = acc + jnp.dot(jnp.exp2(s).astype(BF16), v, preferred_element_type=F32)
            outs.append(acc)
        o_ref[0, rows, :] = _pair_output(outs)


def _mla_call(q, k, v, tiles, bounded):
    tq, tk = tiles
    b, s, width = q.shape
    pair = 2 * HEAD_PAD
    n_pair = width // pair
    return pl.pallas_call(
        functools.partial(_mla_bounded_kernel if bounded else _mla_online_kernel, tk=tk),
        grid=(b, n_pair, s // tq),
        in_specs=[
            pl.BlockSpec((1, tq, pair), lambda i, h, j: (i, j, h)),
            pl.BlockSpec((1, s, pair), lambda i, h, j: (i, 0, h)),
            pl.BlockSpec((1, s, pair), lambda i, h, j: (i, 0, h)),
        ],
        out_specs=pl.BlockSpec((1, tq, 2 * MLA_V), lambda i, h, j: (i, j, h)),
        out_shape=jax.ShapeDtypeStruct((b, s, n_pair * 2 * MLA_V), BF16),
        compiler_params=_cparams(("arbitrary", "arbitrary", "arbitrary")),
        name="mla_attn_bounded" if bounded else "mla_attn_online",
    )(q, k, v)


def _dil_kernel(x_ref, bias_ref, o_ref, qf, kf, vf, *states, bounded):
    s_len = o_ref.shape[1]
    cls_len = s_len // DIL_CLASSES
    piece_q = DIL_BQ // DIL_CLASSES
    piece_k = DIL_BK // DIL_CLASSES
    lanes = (slice(0, LANES), slice(LANES, 2 * LANES), slice(2 * LANES, 3 * LANES))
    lane = lax.broadcasted_iota(jnp.int32, (DIL_BQ, LANES), 1)
    first = lane < DIL_HEAD_DIM
    for dst, sl in zip((qf, kf, vf), lanes):
        dst[...] = x_ref[0, :, sl].astype(F32)

    def band_tile(q_t, k_t, v_t, tab):
        zero = jnp.zeros_like(q_t)
        qm = jnp.concatenate([jnp.where(first, q_t, zero), jnp.where(first, zero, q_t)], axis=0)
        s = lax.dot_general(qm, k_t, (((1,), (1,)), ((), ())), preferred_element_type=F32)
        s = s + bias_ref[tab].reshape(2 * DIL_BQ, DIL_BK)
        if not bounded:
            m = jnp.max(s, axis=-1, keepdims=True)
            s = s - m
        p = jnp.exp2(s)
        l = jnp.sum(p, axis=-1, keepdims=True)
        pv = jnp.dot(p.astype(BF16), v_t, preferred_element_type=F32)
        out = (jnp.where(first, l[:DIL_BQ], l[DIL_BQ:]), jnp.where(first, pv[:DIL_BQ], pv[DIL_BQ:]))
        return out if bounded else (jnp.where(first, m[:DIL_BQ], m[DIL_BQ:]),) + out

    def merge(old, new):
        if bounded:
            return tuple(o + n for o, n in zip(old, new))
        m_n = jnp.maximum(old[0], new[0])
        a = jnp.exp2(old[0] - m_n)
        c = jnp.exp2(new[0] - m_n)
        return (m_n, a * old[1] + c * new[1], a * old[2] + c * new[2])

    def variant_of(n, n_blk):
        return 0 if n == 0 else (2 if n == n_blk - 1 else 1)

    p_idx = 1
    n_blk = cls_len // DIL_BQ
    for r in range(DIL_CLASSES):
        for n in range(n_blk):
            i0 = r * cls_len + n * DIL_BQ
            ks = r * cls_len + min(max(n * DIL_BQ - DIL_HALF, 0), cls_len - DIL_BK)
            vals = band_tile(x_ref[0, i0:i0 + DIL_BQ, lanes[0]], x_ref[0, ks:ks + DIL_BK, lanes[1]],
                             x_ref[0, ks:ks + DIL_BK, lanes[2]], 3 * p_idx + variant_of(n, n_blk))
            for st, val in zip(states, vals):
                st[i0:i0 + DIL_BQ, :] = val

    p_idx = 0
    n_blk = s_len // DIL_BQ
    for n in range(n_blk):
        i0 = n * DIL_BQ // DIL_CLASSES
        ks = min(max(n * DIL_BQ - DIL_HALF, 0), s_len - DIL_BK) // DIL_CLASSES
        rows_q = [slice(r * cls_len + i0, r * cls_len + i0 + piece_q) for r in range(DIL_CLASSES)]
        rows_k = [slice(r * cls_len + ks, r * cls_len + ks + piece_k) for r in range(DIL_CLASSES)]
        vals = band_tile(jnp.concatenate([x_ref[0, rq, lanes[0]] for rq in rows_q], axis=0),
                         jnp.concatenate([x_ref[0, rk, lanes[1]] for rk in rows_k], axis=0),
                         jnp.concatenate([x_ref[0, rk, lanes[2]] for rk in rows_k], axis=0),
                         3 * p_idx + variant_of(n, n_blk))
        for r, rq in enumerate(rows_q):
            piece = slice(r * piece_q, (r + 1) * piece_q)
            merged = merge(tuple(st[rq, :] for st in states), tuple(val[piece] for val in vals))
            for st, val in zip(states, merged):
                st[rq, :] = val

    p_idx = 2
    ratio = DIL_PATTERNS[p_idx][1] // DIL_CLASSES
    sub_len = cls_len // ratio
    n_blk = sub_len // DIL_BQ
    for r in range(DIL_CLASSES):
        for c in range(ratio):
            for n in range(n_blk):
                ks = min(max(n * DIL_BQ - DIL_HALF, 0), sub_len - DIL_BK)
                rows_q = pl.ds(r * cls_len + c + ratio * n * DIL_BQ, DIL_BQ, stride=ratio)
                rows_k = pl.ds(r * cls_len + c + ratio * ks, DIL_BK, stride=ratio)
                vals = band_tile(qf[rows_q, :].astype(BF16), kf[rows_k, :].astype(BF16),
                                 vf[rows_k, :].astype(BF16), 3 * p_idx + variant_of(n, n_blk))
                merged = merge(tuple(st[rows_q, :] for st in states), vals)
                for st, val in zip(states, merged):
                    st[rows_q, :] = val

    o_ref[0] = (states[-1][...] / states[-2][...]).astype(BF16)


def _dil_call(qkvd, bias_tabs, bounded):
    b, s, width = qkvd.shape
    n_pair = width // DIL_GROUP
    n_tab = bias_tabs.shape[0] // 2
    tab_set = 0 if bounded else 1
    n_state = 2 if bounded else 3
    return pl.pallas_call(
        functools.partial(_dil_kernel, bounded=bounded),
        grid=(n_pair, b),
        in_specs=[pl.BlockSpec((1, s, DIL_GROUP), lambda hp, i: (i, 0, hp)),
                  pl.BlockSpec((n_tab, 2, DIL_BQ, DIL_BK), lambda hp, i: (tab_set, hp, 0, 0))],
        out_specs=pl.BlockSpec((1, s, LANES), lambda hp, i: (i, 0, hp)),
        out_shape=jax.ShapeDtypeStruct((b, s, n_pair * LANES), BF16),
        scratch_shapes=[pltpu.VMEM((s, LANES), F32)] * (3 + n_state),
        compiler_params=_cparams(("arbitrary", "arbitrary")),
        name="dil_attn_bounded" if bounded else "dil_attn_online",
    )(qkvd, bias_tabs)


def _post_kernel(x_ref, ya_ref, yb_ref, mod_ref, g1_ref, wg_ref, wa_ref, wb_ref, wo_ref, o_ref):
    x = x_ref[0]
    h = _modulated_norm(x, g1_ref[...], mod_ref[0, 0:1, :], mod_ref[0, 1:2, :])
    gates = jnp.dot(h.astype(BF16), wg_ref[...], preferred_element_type=F32)
    d = x.shape[-1]
    a = jnp.dot(ya_ref[0], wa_ref[...], preferred_element_type=F32)
    bb = jnp.dot(yb_ref[0], wb_ref[...], preferred_element_type=F32)
    merged = jax.nn.sigmoid(gates[:, :d]) * a + jax.nn.sigmoid(gates[:, d:]) * bb
    upd = jnp.dot(merged.astype(BF16), wo_ref[...], preferred_element_type=F32)
    o_ref[0] = x + mod_ref[0, 2:3, :] * upd


def _post_call(x, ya, yb, mod_l, g1, wg, wa, wb, wo, tm):
    b, s, d = x.shape
    tok = lambda i, j: (i, j, 0)
    const2 = lambda i, j: (0, 0)
    return pl.pallas_call(
        _post_kernel,
        grid=(b, s // tm),
        in_specs=[
            pl.BlockSpec((1, tm, d), tok),
            pl.BlockSpec((1, tm, ya.shape[-1]), tok),
            pl.BlockSpec((1, tm, yb.shape[-1]), tok),
            pl.BlockSpec((1, 6, d), lambda i, j: (i, 0, 0)),
            _resident((1, d), const2),
            _resident(wg.shape, const2),
            _resident(wa.shape, const2),
            _resident(wb.shape, const2),
            _resident(wo.shape, const2),
        ],
        out_specs=pl.BlockSpec((1, tm, d), tok),
        out_shape=jax.ShapeDtypeStruct(x.shape, F32),
        compiler_params=_cparams(("arbitrary", "arbitrary")),
        name="attn_out",
    )(x, ya, yb, mod_l, g1, wg, wa, wb, wo)


def _ffn_kernel(x_ref, mod_ref, g2_ref, wg_ref, wu_ref, wd_ref, o_ref, *scratch, natural_out):
    x = jnp.concatenate([x_ref[0, c] for c in range(DIL_CLASSES)], axis=0) if natural_out else x_ref[0]
    h = _modulated_norm(x, g2_ref[...], mod_ref[0, 3:4, :], mod_ref[0, 4:5, :]).astype(BF16)
    g = jnp.dot(h, wg_ref[...], preferred_element_type=F32)
    u = jnp.dot(h, wu_ref[...], preferred_element_type=F32)
    act = (g * jax.nn.sigmoid(g)) * u
    upd = jnp.dot(act.astype(BF16), wd_ref[...], preferred_element_type=F32)
    res = x + mod_ref[0, 5:6, :] * upd
    if not natural_out:
        o_ref[0] = res
        return
    slab_ref, = scratch
    cls_rows = res.shape[0] // DIL_CLASSES
    for cls in range(DIL_CLASSES):
        for c in range(slab_ref.shape[0]):
            slab_ref[c, pl.ds(cls, cls_rows, stride=DIL_CLASSES), :] = res[cls * cls_rows:(cls + 1) * cls_rows,
                                                                           c * LANES:(c + 1) * LANES]
    o_ref[0] = jnp.concatenate([slab_ref[c] for c in range(slab_ref.shape[0])], axis=1)


def _ffn_call(x, mod_l, g2, wg, wu, wd, tm, natural_out):
    b, s, d = x.shape
    tok = lambda i, j: (i, j, 0)
    const2 = lambda i, j: (0, 0)
    cls_rows = tm // DIL_CLASSES
    x_in = x.reshape(b, DIL_CLASSES, s // DIL_CLASSES, d) if natural_out else x
    return pl.pallas_call(
        functools.partial(_ffn_kernel, natural_out=natural_out),
        grid=(b, s // tm),
        in_specs=[
            pl.BlockSpec((1, DIL_CLASSES, cls_rows, d), lambda i, j: (i, 0, j, 0)) if natural_out
            else pl.BlockSpec((1, tm, d), tok),
            pl.BlockSpec((1, 6, d), lambda i, j: (i, 0, 0)),
            _resident((1, d), const2),
            _resident(wg.shape, const2),
            _resident(wu.shape, const2),
            _resident(wd.shape, const2),
        ],
        out_specs=pl.BlockSpec((1, tm, d), tok),
        out_shape=jax.ShapeDtypeStruct(x.shape, F32),
        scratch_shapes=[pltpu.VMEM((d // LANES, tm, LANES), F32)] if natural_out else [],
        compiler_params=_cparams(("arbitrary", "arbitrary")),
        name="swiglu_last" if natural_out else "swiglu",
    )(x_in, mod_l, g2, wg, wu, wd)


def _pad_heads(w, width):
    rows = w.shape[0]
    w = w.reshape(rows, MLA_HEADS, width)
    return jnp.pad(w, ((0, 0), (0, 0), (0, HEAD_PAD - width))).reshape(rows, MLA_HEADS * HEAD_PAD)


def _pad_heads_alternating(w):
    rows = w.shape[0]
    w = w.reshape(rows, MLA_HEADS // 2, 2, MLA_V)
    zero = jnp.zeros_like(w[:, :, 0])
    even = jnp.concatenate([w[:, :, 0], zero], axis=-1)
    odd = jnp.concatenate([zero, w[:, :, 1]], axis=-1)
    return jnp.stack([even, odd], axis=2).reshape(rows, MLA_HEADS * HEAD_PAD)


def _pad_lanes(g):
    return jnp.pad(g, (0, LANES - g.shape[0]))


def _swap_rope(g):
    half = MLA_ROPE // 2
    return jnp.concatenate([jnp.zeros((MLA_NOPE,), g.dtype), g[MLA_NOPE + half:], g[MLA_NOPE:MLA_NOPE + half]])


def _class_major(t, n_class):
    b, s = t.shape[:2]
    return jnp.swapaxes(t.reshape(b, s // n_class, n_class, *t.shape[2:]), 1, 2).reshape(t.shape)


def _block_ones(block, size=2 * LANES):
    idx = np.arange(size) // block
    return jnp.asarray(idx[:, None] == idx[None, :], BF16)


def kernel(x, c, positions, rel_bias, norm1_g, norm2_g, ada_w, ada_b, w_in, q_a_norm, w_q_b, kv_a_norm, w_kv_b,
           q_norm_a, k_norm_a, q_norm_b, k_norm_b, w_branch_a, w_branch_b, w_out, w_ffn_gate, w_ffn_up,
           w_ffn_down):
    depth = w_in.shape[0]
    b, s, d = x.shape
    positions = _class_major(positions, DIL_CLASSES)

    mod = _ada_call(c, ada_w, ada_b).reshape(depth, b, 6, d)
    cos_t, sin_t = _rope_call(positions)
    dil_qk_bound = DIL_HEAD_DIM * jnp.max(jnp.abs(q_norm_b * (DIL_HEAD_DIM ** -0.5 * LOG2E))) * jnp.max(jnp.abs(k_norm_b))
    dil_shift = dil_qk_bound + jnp.max(rel_bias) * LOG2E
    dil_range = 2.0 * dil_qk_bound + (jnp.max(rel_bias) - jnp.min(rel_bias)) * LOG2E
    bias_tabs = _bias_call(rel_bias, jnp.stack([dil_shift, jnp.zeros((), F32)]))
    ones_mla = _block_ones(HEAD_PAD)
    ones_dil = _block_ones(DIL_HEAD_DIM)
    half = MLA_ROPE // 2
    lane2 = np.arange(2 * LANES) % LANES
    spare_qk = jnp.asarray(lane2 == MLA_QK, F32)
    spare_v = jnp.asarray((np.arange(2 * LANES) == MLA_V) | (np.arange(2 * LANES) == LANES), F32)

    for l in range(depth):
        wl = w_in[l]
        k_rope_cols = jnp.pad(wl[:, _C_KR:_C_QKVB], ((0, 0), (MLA_NOPE, LANES - MLA_QK)))
        w_pre = jnp.concatenate([wl[:, _C_QC:_C_KR], k_rope_cols, wl[:, _C_QKVB:_C_GATE]], axis=1).astype(BF16)
        w_gates = wl[:, _C_GATE:].astype(BF16)
        wq3 = w_q_b[l].reshape(MLA_Q_LORA, MLA_HEADS, MLA_QK)
        lo, hi = wq3[..., MLA_NOPE:MLA_NOPE + half], wq3[..., MLA_NOPE + half:]
        wq_sw = jnp.concatenate([jnp.zeros_like(wq3[..., :MLA_NOPE]), hi, lo], axis=-1)
        wqq = jnp.concatenate([_pad_heads(w_q_b[l], MLA_QK),
                               _pad_heads(wq_sw.reshape(MLA_Q_LORA, -1), MLA_QK)], axis=1).astype(BF16)
        wkv3 = w_kv_b[l].reshape(MLA_KV_LORA, MLA_HEADS, MLA_NOPE + MLA_V)
        wkv = jnp.concatenate([_pad_heads(wkv3[:, :, :MLA_NOPE].reshape(MLA_KV_LORA, -1), MLA_NOPE),
                               _pad_heads_alternating(wkv3[:, :, MLA_NOPE:])], axis=1).astype(BF16)
        gq = q_norm_a[l] * (MLA_QK ** -0.5 * LOG2E)
        gk = k_norm_a[l]
        score_bound = MLA_QK * jnp.max(jnp.abs(gq)) * jnp.max(jnp.abs(gk))
        gains = jnp.stack([
            jnp.tile(_pad_lanes(gq), 2), jnp.tile(_pad_lanes(_swap_rope(gq)), 2),
            jnp.tile(_pad_lanes(gk), 2), jnp.tile(_pad_lanes(_swap_rope(gk)), 2),
            jnp.tile(q_norm_b[l] * (DIL_HEAD_DIM ** -0.5 * LOG2E), 4), jnp.tile(k_norm_b[l], 4),
            -score_bound * spare_qk, spare_qk, spare_v] + [jnp.zeros((2 * LANES,), F32)] * 7)
        g1 = norm1_g[l].reshape(1, d)
        g2 = norm2_g[l].reshape(1, d)

        q, k, v, qkvd, *x_cls = _pre_call(x, mod[l], g1, cos_t, sin_t, w_pre, q_a_norm[l].reshape(1, -1), wqq,
                                          kv_a_norm[l].reshape(1, -1), wkv, gains, ones_mla, ones_dil, tm=PRE_TM,
                                          natural_in=(l == 0))
        if x_cls:
            x, = x_cls
        ya = lax.cond(score_bound <= MLA_SAFE_BOUND,
                      functools.partial(_mla_call, tiles=MLA_BOUNDED_TILES, bounded=True),
                      functools.partial(_mla_call, tiles=MLA_ONLINE_TILES, bounded=False), q, k, v)
        yb = lax.cond(dil_range <= DIL_SAFE_RANGE,
                      functools.partial(_dil_call, bounded=True), functools.partial(_dil_call, bounded=False),
                      qkvd, bias_tabs)
        x = _post_call(x, ya, yb, mod[l], g1, w_gates, w_branch_a[l].astype(BF16), w_branch_b[l].astype(BF16),
                       w_out[l].astype(BF16), tm=POST_TM)
        x = _ffn_call(x, mod[l], g2, w_ffn_gate[l].astype(BF16), w_ffn_up[l].astype(BF16),
                      w_ffn_down[l].astype(BF16), tm=FFN_TM, natural_out=(l == depth - 1))
    return x
```

```python
import functools
import math

import jax
import jax.numpy as jnp
import numpy as np
from jax import lax
from jax.experimental import pallas as pl
from jax.experimental.pallas import tpu as pltpu

F32 = jnp.float32
BF16 = jnp.bfloat16

LANES = 128
V7X_VMEM_BYTES = 64 * 1024 * 1024
VMEM_LIMIT_BYTES = V7X_VMEM_BYTES - 8 * 1024 * 1024

MLA_HEADS = 8
MLA_Q_LORA = 256
MLA_KV_LORA = 128
MLA_NOPE = 64
MLA_ROPE = 32
MLA_V = 64
MLA_QK = MLA_NOPE + MLA_ROPE
ROPE_THETA = 10000.0
DIL_HEADS = 8
DIL_HEAD_DIM = 64
DIL_PATTERNS = ((128, 1), (512, 4), (2048, 16))
DIL_HALF = 64
REL_BUCKETS = 32
REL_MAX_DIST = 1024
EPS = 1e-6
NEG_INF = -1e30
LOG2E = math.log2(math.e)
MLA_SAFE_BOUND = 60.0
MLA_SUB = 512

PRE_TM = 512
POST_TM = 1024
FFN_TM = 1024
MLA_BOUNDED_TILES = (4096, 256)
MLA_ONLINE_TILES = (512, 1024)

_C_QC = 0
_C_KVC = _C_QC + MLA_Q_LORA
_C_KR = _C_KVC + MLA_KV_LORA
_C_QKVB = _C_KR + MLA_ROPE
_C_GATE = _C_QKVB + 3 * DIL_HEADS * DIL_HEAD_DIM
PRE_COLS = MLA_Q_LORA + MLA_KV_LORA + LANES + 3 * DIL_HEADS * DIL_HEAD_DIM
HEAD_PAD = LANES

DIL_BQ = 2 * DIL_HALF
DIL_BK = 4 * DIL_HALF
DIL_CLASSES = 4
DIL_GROUP = 3 * LANES
DIL_SAFE_RANGE = 120.0


def _cparams(sem):
    return pltpu.CompilerParams(dimension_semantics=sem, vmem_limit_bytes=VMEM_LIMIT_BYTES)


def _resident(shape, index_map):
    return pl.BlockSpec(shape, index_map, pipeline_mode=pl.Buffered(1))


def _layer_resident(stacked, layer):
    return _resident((None,) + stacked.shape[1:], lambda i, j: (layer, 0, 0))


def _ada_kernel(c_ref, w_ref, b_ref, o_ref):
    c = c_ref[...]
    act = (c * jax.nn.sigmoid(c)).astype(BF16)
    o_ref[0] = jnp.dot(act, w_ref[0].astype(BF16), preferred_element_type=F32) + b_ref[0]


def _ada_call(c, ada_w, ada_b):
    depth, d, six_d = ada_w.shape
    b = c.shape[0]
    cols = 2 * d
    return pl.pallas_call(
        _ada_kernel,
        grid=(depth, six_d // cols),
        in_specs=[
            pl.BlockSpec((b, d), lambda l, j: (0, 0)),
            pl.BlockSpec((1, d, cols), lambda l, j: (l, 0, j)),
            pl.BlockSpec((1, 1, cols), lambda l, j: (l, 0, j)),
        ],
        out_specs=pl.BlockSpec((1, b, cols), lambda l, j: (l, 0, j)),
        out_shape=jax.ShapeDtypeStruct((depth, b, six_d), F32),
        compiler_params=_cparams(("arbitrary", "arbitrary")),
        name="ada_mod",
    )(c, ada_w, ada_b.reshape(depth, 1, six_d))


ROPE_PACK = LANES // MLA_ROPE


def _rope_kernel(pos_ref, inv_ref, c_ref, s_ref):
    ts = pos_ref.shape[1]
    rows = ts // ROPE_PACK
    half = MLA_ROPE // 2
    lane = lax.broadcasted_iota(jnp.int32, (rows, LANES), 1)
    pos = pos_ref[0].astype(F32)
    ang = jnp.zeros((rows, LANES), F32)
    for g in range(ROPE_PACK):
        ang = jnp.where((lane >= g * MLA_ROPE) & (lane < (g + 1) * MLA_ROPE),
                        pos[g * rows:(g + 1) * rows] * inv_ref[...], ang)
    cosv = jnp.cos(ang)
    sinv = jnp.sin(ang)
    for g in range(ROPE_PACK):
        shift = (MLA_NOPE - g * MLA_ROPE) % LANES
        cg = pltpu.roll(cosv, shift, 1) if shift else cosv
        sg = pltpu.roll(sinv, shift, 1) if shift else sinv
        c_ref[0, g * rows:(g + 1) * rows, :] = jnp.where(lane < MLA_NOPE, 1.0, jnp.where(lane < MLA_QK, cg, 0.0))
        s_ref[0, g * rows:(g + 1) * rows, :] = jnp.where(
            lane < MLA_NOPE, 0.0, jnp.where(lane < MLA_NOPE + half, -sg, jnp.where(lane < MLA_QK, sg, 0.0)))


def _rope_call(positions):
    b, s = positions.shape
    half = MLA_ROPE // 2
    inv = ROPE_THETA ** (-jnp.arange(half, dtype=F32) / half)
    inv_lane = jnp.tile(inv, 2 * ROPE_PACK).reshape(1, LANES)
    ts = 512
    spec = pl.BlockSpec((1, ts, LANES), lambda i, j: (i, j, 0))
    return pl.pallas_call(
        _rope_kernel,
        grid=(b, s // ts),
        in_specs=[pl.BlockSpec((1, ts, 1), lambda i, j: (i, j, 0)),
                  pl.BlockSpec((1, LANES), lambda i, j: (0, 0))],
        out_specs=[spec, spec],
        out_shape=[jax.ShapeDtypeStruct((b, s, LANES), F32)] * 2,
        compiler_params=_cparams(("arbitrary", "arbitrary")),
        name="rope_tables",
    )(positions.reshape(b, s, 1), inv_lane)


def _t5_bucket(rel):
    nb = REL_BUCKETS // 2
    max_exact = nb // 2
    ret = jnp.where(rel > 0, nb, 0)
    n = jnp.abs(rel)
    nf = jnp.maximum(n, 1).astype(F32)
    large = max_exact + (jnp.log(nf / max_exact) / math.log(REL_MAX_DIST / max_exact) * (nb - max_exact)).astype(jnp.int32)
    large = jnp.minimum(large, nb - 1)
    return ret + jnp.where(n < max_exact, n, large)


def _bias_kernel(bucket_ref, rb_ref, shift_ref, o_ref):
    h = pl.program_id(2)
    bucket = bucket_ref[0]
    acc = jnp.full(bucket.shape, NEG_INF, F32)
    for bkt in range(REL_BUCKETS):
        acc = jnp.where(bucket == bkt, rb_ref[bkt, h], acc)
    o_ref[0, 0] = acc * LOG2E - shift_ref[pl.program_id(0)]


def _bias_call(rel_bias, shifts):
    tabs = []
    for _, dil in DIL_PATTERNS:
        qi, kj = np.arange(DIL_BQ), np.arange(DIL_BK)
        if dil == 1:
            qi = DIL_CLASSES * (qi % (DIL_BQ // DIL_CLASSES)) + qi // (DIL_BQ // DIL_CLASSES)
            kj = DIL_CLASSES * (kj % (DIL_BK // DIL_CLASSES)) + kj // (DIL_BK // DIL_CLASSES)
        qi = jnp.asarray(qi, jnp.int32)[:, None]
        kj = jnp.asarray(kj, jnp.int32)[None, :]
        for variant in range(3):
            rel = kj - variant * DIL_HALF - qi
            tabs.append(jnp.where(jnp.abs(rel) <= DIL_HALF, _t5_bucket(rel * dil), -1))
    bucket = jnp.stack(tabs).astype(jnp.int32)
    n_tab = bucket.shape[0]
    n_set = shifts.shape[0]
    return pl.pallas_call(
        _bias_kernel,
        grid=(n_set, n_tab, DIL_HEADS),
        in_specs=[pl.BlockSpec((1, DIL_BQ, DIL_BK), lambda s, t, h: (t, 0, 0)),
                  pl.BlockSpec(memory_space=pltpu.SMEM), pl.BlockSpec(memory_space=pltpu.SMEM)],
        out_specs=pl.BlockSpec((1, 1, DIL_BQ, DIL_BK), lambda s, t, h: (s * n_tab + t, h, 0, 0)),
        out_shape=jax.ShapeDtypeStruct((n_set * n_tab, DIL_HEADS, DIL_BQ, DIL_BK), F32),
        compiler_params=_cparams(("arbitrary", "arbitrary", "arbitrary")),
        name="bias_tables",
    )(bucket, rel_bias, shifts.astype(F32))


def _modulated_norm(x, gain, shift, scale):
    r = lax.rsqrt(jnp.mean(x * x, axis=-1, keepdims=True) + EPS)
    return (x * r) * (gain * (1.0 + scale)) + shift


PRE_SUB = 256


def _group_sumsq(t, ones_ref):
    return jnp.dot((t * t).astype(BF16), ones_ref[...], preferred_element_type=F32)


def _class_slabs_load(slab_ref, tile, cls):
    rows = tile // DIL_CLASSES
    return jnp.concatenate([slab_ref[c, pl.ds(cls, rows, stride=DIL_CLASSES), :] for c in range(slab_ref.shape[0])],
                           axis=1)


def _pre_kernel(x_ref, mod_ref, g1_ref, cos_ref, sin_ref, w_ref, qan_ref, wqq_ref, kvan_ref, wkv_ref,
                gains_ref, ones_mla_ref, ones_dil_ref, *rest, natural_in):
    if natural_in:
        q_out, k_out, v_out, qkvd_out, xp_out, slab_ref = rest
        tm = x_ref.shape[1]
        for c in range(slab_ref.shape[0]):
            slab_ref[c] = x_ref[0, :, c * LANES:(c + 1) * LANES]
    else:
        q_out, k_out, v_out, qkvd_out = rest
        tm = x_ref.shape[1] * x_ref.shape[2]
    cls_rows = tm // DIL_CLASSES
    cls_per_sub = PRE_SUB // cls_rows

    def class_rows(ref, sub):
        return jnp.concatenate([ref[0, sub * cls_per_sub + j] for j in range(cls_per_sub)], axis=0)

    def store(ref, sub, sl, val):
        for j in range(cls_per_sub):
            ref[0, sub * cls_per_sub + j, :, sl] = val[j * cls_rows:(j + 1) * cls_rows]

    hp = MLA_HEADS * HEAD_PAD
    n_dil = DIL_HEADS * DIL_HEAD_DIM
    grp = 2 * LANES
    c0 = MLA_Q_LORA
    c1 = c0 + MLA_KV_LORA
    c2 = c1 + LANES
    half = MLA_ROPE // 2
    inv_qk = 1.0 / MLA_QK
    inv_dh = 1.0 / DIL_HEAD_DIM
    lane = lax.broadcasted_iota(jnp.int32, (PRE_SUB, LANES), 1)
    low_half = lane < MLA_NOPE + half

    for sub in range(tm // PRE_SUB):
        if natural_in:
            pieces = [_class_slabs_load(slab_ref, tm, sub * cls_per_sub + j) for j in range(cls_per_sub)]
            for j, piece in enumerate(pieces):
                xp_out[0, sub * cls_per_sub + j] = piece
            x = jnp.concatenate(pieces, axis=0)
        else:
            x = class_rows(x_ref, sub)
        h = _modulated_norm(x, g1_ref[...], mod_ref[0, 0:1, :], mod_ref[0, 1:2, :])
        proj = jnp.dot(h.astype(BF16), w_ref[...], preferred_element_type=F32)

        qc = proj[:, :c0]
        qcn = (qc * lax.rsqrt(jnp.mean(qc * qc, axis=-1, keepdims=True) + EPS)) * qan_ref[...]
        qq = jnp.dot(qcn.astype(BF16), wqq_ref[...], preferred_element_type=F32)
        kvc = proj[:, c0:c1]
        kvn = (kvc * lax.rsqrt(jnp.mean(kvc * kvc, axis=-1, keepdims=True) + EPS)) * kvan_ref[...]
        kv = jnp.dot(kvn.astype(BF16), wkv_ref[...], preferred_element_type=F32)
        k_rope = proj[:, c1:c2]
        k_rope_sw = jnp.where(low_half, pltpu.roll(k_rope, LANES - half, 1), pltpu.roll(k_rope, half, 1))

        cos2 = jnp.concatenate([class_rows(cos_ref, sub)] * 2, axis=1)
        sin2 = jnp.concatenate([class_rows(sin_ref, sub)] * 2, axis=1)
        aq = gains_ref[0:1, :] * cos2
        bq = gains_ref[1:2, :] * sin2
        ak = gains_ref[2:3, :] * cos2
        bk = gains_ref[3:4, :] * sin2
        kr2 = jnp.concatenate([k_rope] * 2, axis=1)
        krs2 = jnp.concatenate([k_rope_sw] * 2, axis=1)
        for g in range(hp // grp):
            sl = slice(g * grp, (g + 1) * grp)
            qh = qq[:, sl]
            rq = lax.rsqrt(_group_sumsq(qh, ones_mla_ref) * inv_qk + EPS)
            store(q_out, sub, sl, (rq * (qh * aq + qq[:, hp + g * grp:hp + (g + 1) * grp] * bq)
                                   + gains_ref[6:7, :]).astype(BF16))
            kh = kv[:, sl] + kr2
            rk = lax.rsqrt(_group_sumsq(kh, ones_mla_ref) * inv_qk + EPS)
            store(k_out, sub, sl, (rk * (kh * ak + krs2 * bk) + gains_ref[7:8, :]).astype(BF16))
            store(v_out, sub, sl, (kv[:, hp + g * grp:hp + (g + 1) * grp] + gains_ref[8:9, :]).astype(BF16))

        for g in range(n_dil // grp):
            qd = proj[:, c2 + g * grp:c2 + (g + 1) * grp]
            qd = ((qd * lax.rsqrt(_group_sumsq(qd, ones_dil_ref) * inv_dh + EPS)) * gains_ref[4:5, :]).astype(BF16)
            kd = proj[:, c2 + n_dil + g * grp:c2 + n_dil + (g + 1) * grp]
            kd = ((kd * lax.rsqrt(_group_sumsq(kd, ones_dil_ref) * inv_dh + EPS)) * gains_ref[5:6, :]).astype(BF16)
            vd = proj[:, c2 + 2 * n_dil + g * grp:c2 + 2 * n_dil + (g + 1) * grp].astype(BF16)
            for j in range(grp // LANES):
                base = (g * (grp // LANES) + j) * DIL_GROUP
                for t, val in enumerate((qd, kd, vd)):
                    store(qkvd_out, sub, slice(base + t * LANES, base + (t + 1) * LANES), val[:, j * LANES:(j + 1) * LANES])


def _pre_call(x, mod_l, g1, cos_t, sin_t, w_pre, qan, wqq, kvan, wkv, gains, ones_mla, ones_dil, tm, natural_in):
    b, s, d = x.shape
    hp = MLA_HEADS * HEAD_PAD
    n_dil = DIL_HEADS * DIL_HEAD_DIM
    cls_len = s // DIL_CLASSES
    cls_rows = tm // DIL_CLASSES
    cls_view = lambda t: t.reshape(b, DIL_CLASSES, cls_len, t.shape[-1])
    cls_spec = lambda cols: pl.BlockSpec((1, DIL_CLASSES, cls_rows, cols), lambda i, j: (i, 0, j, 0))
    const2 = lambda i, j: (0, 0)
    row = lambda n: _resident((1, n), const2)
    out_cols = [hp, hp, hp, 3 * n_dil] + ([d] if natural_in else [])
    out_dtypes = [BF16] * 4 + ([F32] if natural_in else [])
    outs = pl.pallas_call(
        functools.partial(_pre_kernel, natural_in=natural_in),
        grid=(b, s // tm),
        in_specs=[
            pl.BlockSpec((1, tm, d), lambda i, j: (i, j, 0)) if natural_in else cls_spec(d),
            pl.BlockSpec((1, 6, d), lambda i, j: (i, 0, 0)),
            row(d),
            cls_spec(LANES),
            cls_spec(LANES),
            _resident((d, PRE_COLS), const2),
            row(MLA_Q_LORA),
            _resident((MLA_Q_LORA, 2 * hp), const2),
            row(MLA_KV_LORA),
            _resident((MLA_KV_LORA, 2 * hp), const2),
            _resident(gains.shape, const2),
            _resident(ones_mla.shape, const2),
            _resident(ones_dil.shape, const2),
        ],
        out_specs=[cls_spec(c) for c in out_cols],
        out_shape=[jax.ShapeDtypeStruct((b, DIL_CLASSES, cls_len, c), dt) for c, dt in zip(out_cols, out_dtypes)],
        scratch_shapes=[pltpu.VMEM((d // LANES, tm, LANES), F32)] if natural_in else [],
        compiler_params=_cparams(("arbitrary", "arbitrary")),
        name="pre_proj_first" if natural_in else "pre_proj",
    )(x if natural_in else cls_view(x), mod_l, g1, cls_view(cos_t), cls_view(sin_t), w_pre, qan, wqq, kvan, wkv,
      gains, ones_mla, ones_dil)
    return [o.reshape(b, s, o.shape[-1]) for o in outs]


def _pair_output(outs):
    o0, o1 = outs
    lane = lax.broadcasted_iota(jnp.int32, o0.shape, 1)
    return jnp.where(lane < MLA_V, o0 / o0[:, MLA_V:MLA_V + 1], o1 / o1[:, 0:1]).astype(BF16)


def _mla_online_kernel(q_ref, k_ref, v_ref, o_ref, *, tk):
    tq = q_ref.shape[1]
    n_k = k_ref.shape[1] // tk
    outs = []
    for hh in range(2):
        hl = slice(hh * HEAD_PAD, (hh + 1) * HEAD_PAD)
        q = q_ref[0, :, hl]

        def body(j, carry, q=q, hl=hl):
            m, acc = carry
            ks = pl.multiple_of(j * tk, tk)
            k = k_ref[0, pl.ds(ks, tk), hl]
            v = v_ref[0, pl.ds(ks, tk), hl]
            s = lax.dot_general(q, k, (((1,), (1,)), ((), ())), preferred_element_type=F32)
            m_new = jnp.maximum(m, jnp.max(s, axis=-1, keepdims=True))
            p = jnp.exp2(s - m_new)
            acc = jnp.exp2(m - m_new) * acc + jnp.dot(p.astype(BF16), v, preferred_element_type=F32)
            return m_new, acc

        init = (jnp.full((tq, 1), -jnp.inf, F32), jnp.zeros((tq, HEAD_PAD), F32))
        outs.append(lax.fori_loop(0, n_k, body, init, unroll=True)[1])
    o_ref[0] = _pair_output(outs)


def _mla_bounded_kernel(q_ref, k_ref, v_ref, o_ref, *, tk):
    n_k = k_ref.shape[1] // tk
    for qs in range(q_ref.shape[1] // MLA_SUB):
        rows = slice(qs * MLA_SUB, (qs + 1) * MLA_SUB)
        outs = []
        for hh in range(2):
            hl = slice(hh * HEAD_PAD, (hh + 1) * HEAD_PAD)
            q = q_ref[0, rows, hl]
            acc = jnp.zeros((MLA_SUB, HEAD_PAD), F32)
            for j in range(n_k):
                k = k_ref[0, j * tk:(j + 1) * tk, hl]
                v = v_ref[0, j * tk:(j + 1) * tk, hl]
                s = lax.dot_general(q, k, (((1,), (1,)), ((), ())), preferred_element_type=F32)
                acc = acc + jnp.dot(jnp.exp2(s).astype(BF16), v, preferred_element_type=F32)
            outs.append(acc)
        o_ref[0, rows, :] = _pair_output(outs)


def _mla_call(q, k, v, tiles, bounded):
    tq, tk = tiles
    b, s, width = q.shape
    pair = 2 * HEAD_PAD
    n_pair = width // pair
    return pl.pallas_call(
        functools.partial(_mla_bounded_kernel if bounded else _mla_online_kernel, tk=tk),
        grid=(b, n_pair, s // tq),
        in_specs=[
            pl.BlockSpec((1, tq, pair), lambda i, h, j: (i, j, h)),
            pl.BlockSpec((1, s, pair), lambda i, h, j: (i, 0, h)),
            pl.BlockSpec((1, s, pair), lambda i, h, j: (i, 0, h)),
        ],
        out_specs=pl.BlockSpec((1, tq, 2 * MLA_V), lambda i, h, j: (i, j, h)),
        out_shape=jax.ShapeDtypeStruct((b, s, n_pair * 2 * MLA_V), BF16),
        compiler_params=_cparams(("arbitrary", "arbitrary", "arbitrary")),
        name="mla_attn_bounded" if bounded else "mla_attn_online",
    )(q, k, v)


def _dil_kernel(x_ref, bias_ref, o_ref, qf, kf, vf, *states, bounded):
    s_len = o_ref.shape[1]
    cls_len = s_len // DIL_CLASSES
    piece_q = DIL_BQ // DIL_CLASSES
    piece_k = DIL_BK // DIL_CLASSES
    lanes = (slice(0, LANES), slice(LANES, 2 * LANES), slice(2 * LANES, 3 * LANES))
    lane = lax.broadcasted_iota(jnp.int32, (DIL_BQ, LANES), 1)
    first = lane < DIL_HEAD_DIM
    for dst, sl in zip((qf, kf, vf), lanes):
        dst[...] = x_ref[0, :, sl].astype(F32)

    def band_tile(q_t, k_t, v_t, tab):
        zero = jnp.zeros_like(q_t)
        qm = jnp.concatenate([jnp.where(first, q_t, zero), jnp.where(first, zero, q_t)], axis=0)
        s = lax.dot_general(qm, k_t, (((1,), (1,)), ((), ())), preferred_element_type=F32)
        s = s + bias_ref[tab].reshape(2 * DIL_BQ, DIL_BK)
        if not bounded:
            m = jnp.max(s, axis=-1, keepdims=True)
            s = s - m
        p = jnp.exp2(s)
        l = jnp.sum(p, axis=-1, keepdims=True)
        pv = jnp.dot(p.astype(BF16), v_t, preferred_element_type=F32)
        out = (jnp.where(first, l[:DIL_BQ], l[DIL_BQ:]), jnp.where(first, pv[:DIL_BQ], pv[DIL_BQ:]))
        return out if bounded else (jnp.where(first, m[:DIL_BQ], m[DIL_BQ:]),) + out

    def merge(old, new):
        if bounded:
            return tuple(o + n for o, n in zip(old, new))
        m_n = jnp.maximum(old[0], new[0])
        a = jnp.exp2(old[0] - m_n)
        c = jnp.exp2(new[0] - m_n)
        return (m_n, a * old[1] + c * new[1], a * old[2] + c * new[2])

    def variant_of(n, n_blk):
        return 0 if n == 0 else (2 if n == n_blk - 1 else 1)

    p_idx = 1
    n_blk = cls_len // DIL_BQ
    for r in range(DIL_CLASSES):
        for n in range(n_blk):
            i0 = r * cls_len + n * DIL_BQ
            ks = r * cls_len + min(max(n * DIL_BQ - DIL_HALF, 0), cls_len - DIL_BK)
            vals = band_tile(x_ref[0, i0:i0 + DIL_BQ, lanes[0]], x_ref[0, ks:ks + DIL_BK, lanes[1]],
                             x_ref[0, ks:ks + DIL_BK, lanes[2]], 3 * p_idx + variant_of(n, n_blk))
            for st, val in zip(states, vals):
                st[i0:i0 + DIL_BQ, :] = val

    p_idx = 0
    n_blk = s_len // DIL_BQ
    for n in range(n_blk):
        i0 = n * DIL_BQ // DIL_CLASSES
        ks = min(max(n * DIL_BQ - DIL_HALF, 0), s_len - DIL_BK) // DIL_CLASSES
        rows_q = [slice(r * cls_len + i0, r * cls_len + i0 + piece_q) for r in range(DIL_CLASSES)]
        rows_k = [slice(r * cls_len + ks, r * cls_len + ks + piece_k) for r in range(DIL_CLASSES)]
        vals = band_tile(jnp.concatenate([x_ref[0, rq, lanes[0]] for rq in rows_q], axis=0),
                         jnp.concatenate([x_ref[0, rk, lanes[1]] for rk in rows_k], axis=0),
                         jnp.concatenate([x_ref[0, rk, lanes[2]] for rk in rows_k], axis=0),
                         3 * p_idx + variant_of(n, n_blk))
        for r, rq in enumerate(rows_q):
            piece = slice(r * piece_q, (r + 1) * piece_q)
            merged = merge(tuple(st[rq, :] for st in states), tuple(val[piece] for val in vals))
            for st, val in zip(states, merged):
                st[rq, :] = val

    p_idx = 2
    ratio = DIL_PATTERNS[p_idx][1] // DIL_CLASSES
    sub_len = cls_len // ratio
    n_blk = sub_len // DIL_BQ
    for r in range(DIL_CLASSES):
        for c in range(ratio):
            for n in range(n_blk):
                ks = min(max(n * DIL_BQ - DIL_HALF, 0), sub_len - DIL_BK)
                rows_q = pl.ds(r * cls_len + c + ratio * n * DIL_BQ, DIL_BQ, stride=ratio)
                rows_k = pl.ds(r * cls_len + c + ratio * ks, DIL_BK, stride=ratio)
                vals = band_tile(qf[rows_q, :].astype(BF16), kf[rows_k, :].astype(BF16),
                                 vf[rows_k, :].astype(BF16), 3 * p_idx + variant_of(n, n_blk))
                merged = merge(tuple(st[rows_q, :] for st in states), vals)
                for st, val in zip(states, merged):
                    st[rows_q, :] = val

    o_ref[0] = (states[-1][...] / states[-2][...]).astype(BF16)


def _dil_call(qkvd, bias_tabs, bounded):
    b, s, width = qkvd.shape
    n_pair = width // DIL_GROUP
    n_tab = bias_tabs.shape[0] // 2
    tab_set = 0 if bounded else 1
    n_state = 2 if bounded else 3
    return pl.pallas_call(
        functools.partial(_dil_kernel, bounded=bounded),
        grid=(n_pair, b),
        in_specs=[pl.BlockSpec((1, s, DIL_GROUP), lambda hp, i: (i, 0, hp)),
                  pl.BlockSpec((n_tab, 2, DIL_BQ, DIL_BK), lambda hp, i: (tab_set, hp, 0, 0))],
        out_specs=pl.BlockSpec((1, s, LANES), lambda hp, i: (i, 0, hp)),
        out_shape=jax.ShapeDtypeStruct((b, s, n_pair * LANES), BF16),
        scratch_shapes=[pltpu.VMEM((s, LANES), F32)] * (3 + n_state),
        compiler_params=_cparams(("arbitrary", "arbitrary")),
        name="dil_attn_bounded" if bounded else "dil_attn_online",
    )(qkvd, bias_tabs)


def _post_kernel(x_ref, ya_ref, yb_ref, mod_ref, g1_ref, wg_ref, wa_ref, wb_ref, wo_ref, o_ref):
    x = x_ref[0]
    h = _modulated_norm(x, g1_ref[...], mod_ref[0, 0:1, :], mod_ref[0, 1:2, :])
    gates = jnp.dot(h.astype(BF16), wg_ref[...], preferred_element_type=F32)
    d = x.shape[-1]
    a = jnp.dot(ya_ref[0], wa_ref[...], preferred_element_type=F32)
    bb = jnp.dot(yb_ref[0], wb_ref[...], preferred_element_type=F32)
    merged = jax.nn.sigmoid(gates[:, :d]) * a + jax.nn.sigmoid(gates[:, d:]) * bb
    upd = jnp.dot(merged.astype(BF16), wo_ref[...], preferred_element_type=F32)
    o_ref[0] = x + mod_ref[0, 2:3, :] * upd


def _post_call(x, ya, yb, mod_l, g1, wg, wa, wb, wo, layer, tm):
    b, s, d = x.shape
    tok = lambda i, j: (i, j, 0)
    const2 = lambda i, j: (0, 0)
    return pl.pallas_call(
        _post_kernel,
        grid=(b, s // tm),
        in_specs=[
            pl.BlockSpec((1, tm, d), tok),
            pl.BlockSpec((1, tm, ya.shape[-1]), tok),
            pl.BlockSpec((1, tm, yb.shape[-1]), tok),
            pl.BlockSpec((1, 6, d), lambda i, j: (i, 0, 0)),
            _resident((1, d), const2),
            _resident(wg.shape, const2),
            _layer_resident(wa, layer),
            _layer_resident(wb, layer),
            _layer_resident(wo, layer),
        ],
        out_specs=pl.BlockSpec((1, tm, d), tok),
        out_shape=jax.ShapeDtypeStruct(x.shape, F32),
        compiler_params=_cparams(("arbitrary", "arbitrary")),
        name="attn_out",
    )(x, ya, yb, mod_l, g1, wg, wa, wb, wo)


def _ffn_kernel(x_ref, mod_ref, g2_ref, wg_ref, wu_ref, wd_ref, o_ref, *scratch, natural_out):
    x = jnp.concatenate([x_ref[0, c] for c in range(DIL_CLASSES)], axis=0) if natural_out else x_ref[0]
    h = _modulated_norm(x, g2_ref[...], mod_ref[0, 3:4, :], mod_ref[0, 4:5, :]).astype(BF16)
    g = jnp.dot(h, wg_ref[...], preferred_element_type=F32)
    u = jnp.dot(h, wu_ref[...], preferred_element_type=F32)
    act = (g * jax.nn.sigmoid(g)) * u
    upd = jnp.dot(act.astype(BF16), wd_ref[...], preferred_element_type=F32)
    res = x + mod_ref[0, 5:6, :] * upd
    if not natural_out:
        o_ref[0] = res
        return
    slab_ref, = scratch
    cls_rows = res.shape[0] // DIL_CLASSES
    for cls in range(DIL_CLASSES):
        for c in range(slab_ref.shape[0]):
            slab_ref[c, pl.ds(cls, cls_rows, stride=DIL_CLASSES), :] = res[cls * cls_rows:(cls + 1) * cls_rows,
                                                                           c * LANES:(c + 1) * LANES]
    o_ref[0] = jnp.concatenate([slab_ref[c] for c in range(slab_ref.shape[0])], axis=1)


def _ffn_call(x, mod_l, g2, wg, wu, wd, layer, tm, natural_out):
    b, s, d = x.shape
    tok = lambda i, j: (i, j, 0)
    const2 = lambda i, j: (0, 0)
    cls_rows = tm // DIL_CLASSES
    x_in = x.reshape(b, DIL_CLASSES, s // DIL_CLASSES, d) if natural_out else x
    return pl.pallas_call(
        functools.partial(_ffn_kernel, natural_out=natural_out),
        grid=(b, s // tm),
        in_specs=[
            pl.BlockSpec((1, DIL_CLASSES, cls_rows, d), lambda i, j: (i, 0, j, 0)) if natural_out
            else pl.BlockSpec((1, tm, d), tok),
            pl.BlockSpec((1, 6, d), lambda i, j: (i, 0, 0)),
            _resident((1, d), const2),
            _layer_resident(wg, layer),
            _layer_resident(wu, layer),
            _layer_resident(wd, layer),
        ],
        out_specs=pl.BlockSpec((1, tm, d), tok),
        out_shape=jax.ShapeDtypeStruct(x.shape, F32),
        scratch_shapes=[pltpu.VMEM((d // LANES, tm, LANES), F32)] if natural_out else [],
        compiler_params=_cparams(("arbitrary", "arbitrary")),
        name="swiglu_last" if natural_out else "swiglu",
    )(x_in, mod_l, g2, wg, wu, wd)


def _pad_heads(w, width):
    rows = w.shape[0]
    w = w.reshape(rows, MLA_HEADS, width)
    return jnp.pad(w, ((0, 0), (0, 0), (0, HEAD_PAD - width))).reshape(rows, MLA_HEADS * HEAD_PAD)


def _pad_heads_alternating(w):
    rows = w.shape[0]
    w = w.reshape(rows, MLA_HEADS // 2, 2, MLA_V)
    zero = jnp.zeros_like(w[:, :, 0])
    even = jnp.concatenate([w[:, :, 0], zero], axis=-1)
    odd = jnp.concatenate([zero, w[:, :, 1]], axis=-1)
    return jnp.stack([even, odd], axis=2).reshape(rows, MLA_HEADS * HEAD_PAD)


def _pad_lanes(g):
    return jnp.pad(g, (0, LANES - g.shape[0]))


def _swap_rope(g):
    half = MLA_ROPE // 2
    return jnp.concatenate([jnp.zeros((MLA_NOPE,), g.dtype), g[MLA_NOPE + half:], g[MLA_NOPE:MLA_NOPE + half]])


def _class_major(t, n_class):
    b, s = t.shape[:2]
    return jnp.swapaxes(t.reshape(b, s // n_class, n_class, *t.shape[2:]), 1, 2).reshape(t.shape)


def _block_ones(block, size=2 * LANES):
    idx = np.arange(size) // block
    return jnp.asarray(idx[:, None] == idx[None, :], BF16)


def kernel(x, c, positions, rel_bias, norm1_g, norm2_g, ada_w, ada_b, w_in, q_a_norm, w_q_b, kv_a_norm, w_kv_b,
           q_norm_a, k_norm_a, q_norm_b, k_norm_b, w_branch_a, w_branch_b, w_out, w_ffn_gate, w_ffn_up,
           w_ffn_down):
    depth = w_in.shape[0]
    b, s, d = x.shape
    positions = _class_major(positions, DIL_CLASSES)

    mod = _ada_call(c, ada_w, ada_b).reshape(depth, b, 6, d)
    cos_t, sin_t = _rope_call(positions)
    dil_qk_bound = DIL_HEAD_DIM * jnp.max(jnp.abs(q_norm_b * (DIL_HEAD_DIM ** -0.5 * LOG2E))) * jnp.max(jnp.abs(k_norm_b))
    dil_shift = dil_qk_bound + jnp.max(rel_bias) * LOG2E
    dil_range = 2.0 * dil_qk_bound + (jnp.max(rel_bias) - jnp.min(rel_bias)) * LOG2E
    bias_tabs = _bias_call(rel_bias, jnp.stack([dil_shift, jnp.zeros((), F32)]))
    ones_mla = _block_ones(HEAD_PAD)
    ones_dil = _block_ones(DIL_HEAD_DIM)
    half = MLA_ROPE // 2
    lane2 = np.arange(2 * LANES) % LANES
    spare_qk = jnp.asarray(lane2 == MLA_QK, F32)
    spare_v = jnp.asarray((np.arange(2 * LANES) == MLA_V) | (np.arange(2 * LANES) == LANES), F32)

    stacks = [w.astype(BF16) for w in (w_branch_a, w_branch_b, w_out, w_ffn_gate, w_ffn_up, w_ffn_down)]
    wa_all, wb_all, wo_all, wfg_all, wfu_all, wfd_all = stacks

    for l in range(depth):
        wl = w_in[l]
        k_rope_cols = jnp.pad(wl[:, _C_KR:_C_QKVB], ((0, 0), (MLA_NOPE, LANES - MLA_QK)))
        w_pre = jnp.concatenate([wl[:, _C_QC:_C_KR], k_rope_cols, wl[:, _C_QKVB:_C_GATE]], axis=1).astype(BF16)
        w_gates = wl[:, _C_GATE:].astype(BF16)
        wq3 = w_q_b[l].reshape(MLA_Q_LORA, MLA_HEADS, MLA_QK)
        lo, hi = wq3[..., MLA_NOPE:MLA_NOPE + half], wq3[..., MLA_NOPE + half:]
        wq_sw = jnp.concatenate([jnp.zeros_like(wq3[..., :MLA_NOPE]), hi, lo], axis=-1)
        wqq = jnp.concatenate([_pad_heads(w_q_b[l], MLA_QK),
                               _pad_heads(wq_sw.reshape(MLA_Q_LORA, -1), MLA_QK)], axis=1).astype(BF16)
        wkv3 = w_kv_b[l].reshape(MLA_KV_LORA, MLA_HEADS, MLA_NOPE + MLA_V)
        wkv = jnp.concatenate([_pad_heads(wkv3[:, :, :MLA_NOPE].reshape(MLA_KV_LORA, -1), MLA_NOPE),
                               _pad_heads_alternating(wkv3[:, :, MLA_NOPE:])], axis=1).astype(BF16)
        gq = q_norm_a[l] * (MLA_QK ** -0.5 * LOG2E)
        gk = k_norm_a[l]
        score_bound = MLA_QK * jnp.max(jnp.abs(gq)) * jnp.max(jnp.abs(gk))
        gains = jnp.stack([
            jnp.tile(_pad_lanes(gq), 2), jnp.tile(_pad_lanes(_swap_rope(gq)), 2),
            jnp.tile(_pad_lanes(gk), 2), jnp.tile(_pad_lanes(_swap_rope(gk)), 2),
            jnp.tile(q_norm_b[l] * (DIL_HEAD_DIM ** -0.5 * LOG2E), 4), jnp.tile(k_norm_b[l], 4),
            -score_bound * spare_qk, spare_qk, spare_v] + [jnp.zeros((2 * LANES,), F32)] * 7)
        g1 = norm1_g[l].reshape(1, d)
        g2 = norm2_g[l].reshape(1, d)

        q, k, v, qkvd, *x_cls = _pre_call(x, mod[l], g1, cos_t, sin_t, w_pre, q_a_norm[l].reshape(1, -1), wqq,
                                          kv_a_norm[l].reshape(1, -1), wkv, gains, ones_mla, ones_dil, tm=PRE_TM,
                                          natural_in=(l == 0))
        if x_cls:
            x, = x_cls
        ya = lax.cond(score_bound <= MLA_SAFE_BOUND,
                      functools.partial(_mla_call, tiles=MLA_BOUNDED_TILES, bounded=True),
                      functools.partial(_mla_call, tiles=MLA_ONLINE_TILES, bounded=False), q, k, v)
        yb = lax.cond(dil_range <= DIL_SAFE_RANGE,
                      functools.partial(_dil_call, bounded=True), functools.partial(_dil_call, bounded=False),
                      qkvd, bias_tabs)
        x = _post_call(x, ya, yb, mod[l], g1, w_gates, wa_all, wb_all, wo_all, layer=l, tm=POST_TM)
        x = _ffn_call(x, mod[l], g2, wfg_all, wfu_all, wfd_all, layer=l, tm=FFN_TM, natural_out=(l == depth - 1))
    return x
```
